```python
import math
import jax, jax.numpy as jnp
from jax import lax
import numpy as np

D_MODEL = 1024
BATCH = 8
SEQ = 8192
DEPTH = 1
DEC_BATCH = 2
DEC_SEQ = 16384
PAST_LEN = 128

DA_HEADS = 8
DA_HEAD_DIM = 64
DA_Q = DA_HEADS * 2 * DA_HEAD_DIM
DA_K = DA_HEADS * 2 * DA_HEAD_DIM
DA_V = DA_HEADS * 2 * DA_HEAD_DIM
WG_HEADS = 16
WG_KV_HEADS = 4
WG_HEAD_DIM = 64
WG_Q = WG_HEADS * WG_HEAD_DIM
WG_KV = WG_KV_HEADS * WG_HEAD_DIM
WINDOW = 128
BLOCK = 128
N_BRANCH = 2
GATE_COLS = N_BRANCH * D_MODEL
IN_COLS = DA_Q + DA_K + DA_V + WG_Q + 2 * WG_KV + GATE_COLS
IN_SPLITS = (DA_Q, DA_Q + DA_K, DA_Q + DA_K + DA_V, DA_Q + DA_K + DA_V + WG_Q,
             DA_Q + DA_K + DA_V + WG_Q + WG_KV, DA_Q + DA_K + DA_V + WG_Q + 2 * WG_KV)
D_FF = 4 * D_MODEL
DEEPNORM_ALPHA = (2.0 * DEPTH) ** 0.25
DEEPNORM_BETA = (8.0 * DEPTH) ** -0.25
LN_EPS = 1e-5
NEG_BIG = -1e30

kernel_name = "hybrid_diffattn_windowgqa_encoder"


def alibi_slopes(n_heads):
    return jnp.asarray(2.0 ** (-8.0 * np.arange(1, n_heads + 1) / n_heads), dtype=jnp.float32)


def layer_norm(x, g, b):
    xf = x.astype(jnp.float32)
    mu = jnp.mean(xf, axis=-1, keepdims=True)
    xc = xf - mu
    var = jnp.mean(xc * xc, axis=-1, keepdims=True)
    return (xc * lax.rsqrt(var + LN_EPS) * g.astype(jnp.float32) + b.astype(jnp.float32)).astype(x.dtype)


def diff_attention(q, k, v, lam, lam_init, subln_g):
    B, S, H, _, dh = q.shape
    nb = S // BLOCK
    scale = dh ** -0.5
    slopes = alibi_slopes(H)
    kpos = jnp.arange(S, dtype=jnp.float32)
    qb = q.reshape(B, nb, BLOCK, H, 2, dh).transpose(1, 0, 2, 3, 4, 5)

    def one_block(args):
        i, qi = args
        qpos = (i * BLOCK + jnp.arange(BLOCK)).astype(jnp.float32)
        bias = -slopes[:, None, None] * jnp.abs(qpos[:, None] - kpos[None, :])
        s = jnp.einsum('bqhcd,bkhcd->bchqk', qi, k).astype(jnp.float32) * scale + bias
        p = jax.nn.softmax(s, axis=-1)
        a = p[:, 0] - lam * p[:, 1]
        return jnp.einsum('bhqk,bkhe->bqhe', a.astype(v.dtype), v)

    o = lax.map(one_block, (jnp.arange(nb), qb))
    o = o.transpose(1, 0, 2, 3, 4).reshape(B, S, H, 2 * dh)
    of = o.astype(jnp.float32)
    of = of * lax.rsqrt(jnp.mean(of * of, axis=-1, keepdims=True) + LN_EPS)
    of = of * subln_g.astype(jnp.float32) * (1.0 - lam_init)
    return of.astype(v.dtype).reshape(B, S, H * 2 * dh)


def window_attention(q, k, v, sink):
    B, S, H, dh = q.shape
    G = k.shape[2]
    R = H // G
    nb = S // BLOCK
    scale = dh ** -0.5
    qb = q.reshape(B, nb, BLOCK, G, R, dh)
    pad = ((0, 0), (BLOCK, BLOCK), (0, 0), (0, 0))

    def bands(t):
        tp = jnp.pad(t, pad).reshape(B, nb + 2, BLOCK, G, dh)
        return jnp.concatenate([tp[:, :-2], tp[:, 1:-1], tp[:, 2:]], axis=2)

    kb = bands(k)
    vb = bands(v)
    qi = jnp.arange(BLOCK)
    kj = jnp.arange(3 * BLOCK)
    rel = qi[:, None] - kj[None, :] + BLOCK
    dist = jnp.abs(rel).astype(jnp.float32)
    spos = jnp.arange(nb)[:, None] * BLOCK - BLOCK + kj[None, :]
    valid = (jnp.abs(rel) <= WINDOW)[None] & ((spos >= 0) & (spos < S))[:, None, :]
    slopes = alibi_slopes(H).reshape(G, R)
    s = jnp.einsum('bnqgrd,bnkgd->bngrqk', qb, kb).astype(jnp.float32) * scale
    s = s - slopes[:, :, None, None] * dist
    s = jnp.where(valid[None, :, None, None], s, NEG_BIG)
    sk = sink.astype(jnp.float32).reshape(G, R)[None, None, :, :, None, None]
    m = jnp.maximum(jnp.max(s, axis=-1, keepdims=True), sk)
    e = jnp.exp(s - m)
    p = e / (jnp.sum(e, axis=-1, keepdims=True) + jnp.exp(sk - m))
    o = jnp.einsum('bngrqk,bnkgd->bnqgrd', p.astype(v.dtype), vb)
    return o.reshape(B, S, H * dh)


def encoder_layer(x, l, w_in, b_gate, lam_q1, lam_k1, lam_q2, lam_k2, subln_g, sink_logit,
                  w_br_a, w_br_b, w_out, ln1_g, ln1_b, w_ff1, b_ff1, w_ff2, b_ff2, ln2_g, ln2_b):
    B, S, _ = x.shape
    proj = jnp.einsum('bsd,de->bse', x, w_in)
    da_q, da_k, da_v, wg_q, wg_k, wg_v, g_ab = jnp.split(proj, IN_SPLITS, axis=-1)

    lam_init = 0.8 - 0.6 * math.exp(-0.3 * l)
    lam = (jnp.exp(jnp.sum(lam_q1.astype(jnp.float32) * lam_k1.astype(jnp.float32)))
           - jnp.exp(jnp.sum(lam_q2.astype(jnp.float32) * lam_k2.astype(jnp.float32))) + lam_init)
    o_a = diff_attention(da_q.reshape(B, S, DA_HEADS, 2, DA_HEAD_DIM),
                         da_k.reshape(B, S, DA_HEADS, 2, DA_HEAD_DIM),
                         da_v.reshape(B, S, DA_HEADS, 2 * DA_HEAD_DIM),
                         lam, lam_init, subln_g)

    o_b = window_attention(wg_q.reshape(B, S, WG_HEADS, WG_HEAD_DIM),
                           wg_k.reshape(B, S, WG_KV_HEADS, WG_HEAD_DIM),
                           wg_v.reshape(B, S, WG_KV_HEADS, WG_HEAD_DIM),
                           sink_logit)

    gates = jax.nn.sigmoid(g_ab.astype(jnp.float32) + b_gate.astype(jnp.float32)).astype(x.dtype)
    g_a, g_b = jnp.split(gates, 2, axis=-1)
    merged = g_a * jnp.einsum('bse,ed->bsd', o_a, w_br_a) + g_b * jnp.einsum('bse,ed->bsd', o_b, w_br_b)
    mix = jnp.einsum('bsd,de->bse', merged, w_out)
    x = layer_norm(DEEPNORM_ALPHA * x + mix, ln1_g, ln1_b)

    h = jax.nn.relu(jnp.einsum('bsd,df->bsf', x, w_ff1) + b_ff1)
    f = jnp.einsum('bsf,fd->bsd', h * h, w_ff2) + b_ff2
    return layer_norm(DEEPNORM_ALPHA * x + f, ln2_g, ln2_b)


def setup_inputs(seed: int = 0) -> dict:
    key = jax.random.key(seed)
    ks = jax.random.split(key, 24)
    nrm = jax.random.normal
    f32 = jnp.float32
    col_scale = jnp.concatenate([
        jnp.ones((DA_Q + DA_K,), f32),
        jnp.full((DA_V,), DEEPNORM_BETA, f32),
        jnp.ones((WG_Q + WG_KV,), f32),
        jnp.full((WG_KV,), DEEPNORM_BETA, f32),
        jnp.ones((GATE_COLS,), f32)])
    w_in = nrm(ks[2], (DEPTH, D_MODEL, IN_COLS), f32) * (D_MODEL ** -0.5) * col_scale
    return {
        "x_prompt": nrm(ks[0], (BATCH, SEQ, D_MODEL), f32),
        "x_sample": nrm(ks[1], (DEC_BATCH, DEC_SEQ, D_MODEL), f32),
        "w_in": w_in,
        "b_gate": 0.02 * nrm(ks[3], (DEPTH, GATE_COLS), f32),
        "lam_q1": 0.1 * nrm(ks[4], (DEPTH, DA_HEAD_DIM), f32),
        "lam_k1": 0.1 * nrm(ks[5], (DEPTH, DA_HEAD_DIM), f32),
        "lam_q2": 0.1 * nrm(ks[6], (DEPTH, DA_HEAD_DIM), f32),
        "lam_k2": 0.1 * nrm(ks[7], (DEPTH, DA_HEAD_DIM), f32),
        "subln_g": 1.0 + 0.02 * nrm(ks[8], (DEPTH, 2 * DA_HEAD_DIM), f32),
        "sink_logit": 0.5 * nrm(ks[9], (DEPTH, WG_HEADS), f32),
        "w_br_a": nrm(ks[10], (DEPTH, DA_HEADS * 2 * DA_HEAD_DIM, D_MODEL), f32) * ((DA_HEADS * 2 * DA_HEAD_DIM) ** -0.5) * DEEPNORM_BETA,
        "w_br_b": nrm(ks[11], (DEPTH, WG_Q, D_MODEL), f32) * (WG_Q ** -0.5) * DEEPNORM_BETA,
        "w_out": nrm(ks[12], (DEPTH, D_MODEL, D_MODEL), f32) * (D_MODEL ** -0.5) * DEEPNORM_BETA,
        "ln1_g": 1.0 + 0.02 * nrm(ks[13], (DEPTH, D_MODEL), f32),
        "ln1_b": 0.02 * nrm(ks[14], (DEPTH, D_MODEL), f32),
        "w_ff1": nrm(ks[15], (DEPTH, D_MODEL, D_FF), f32) * (D_MODEL ** -0.5) * DEEPNORM_BETA,
        "b_ff1": 0.02 * nrm(ks[16], (DEPTH, D_FF), f32),
        "w_ff2": nrm(ks[17], (DEPTH, D_FF, D_MODEL), f32) * (D_FF ** -0.5) * DEEPNORM_BETA,
        "b_ff2": 0.02 * nrm(ks[18], (DEPTH, D_MODEL), f32),
        "ln2_g": 1.0 + 0.02 * nrm(ks[19], (DEPTH, D_MODEL), f32),
        "ln2_b": 0.02 * nrm(ks[20], (DEPTH, D_MODEL), f32),
    }


def reference(x_prompt, x_sample, w_in, b_gate, lam_q1, lam_k1, lam_q2, lam_k2, subln_g, sink_logit,
              w_br_a, w_br_b, w_out, ln1_g, ln1_b, w_ff1, b_ff1, w_ff2, b_ff2, ln2_g, ln2_b):
    def run(x):
        for l in range(DEPTH):
            x = encoder_layer(x, l, w_in[l], b_gate[l], lam_q1[l], lam_k1[l], lam_q2[l], lam_k2[l],
                              subln_g[l], sink_logit[l], w_br_a[l], w_br_b[l], w_out[l],
                              ln1_g[l], ln1_b[l], w_ff1[l], b_ff1[l], w_ff2[l], b_ff2[l], ln2_g[l], ln2_b[l])
        return x

    y_prompt = run(x_prompt)
    y_sample = run(x_sample)
    return (y_prompt, y_sample)
```

```python
import functools
import math

import jax
import jax.numpy as jnp
import numpy as np
from jax import lax
from jax.experimental import pallas as pl
from jax.experimental.pallas import tpu as pltpu

D_MODEL = 1024
DA_HEADS = 8
DA_HEAD_DIM = 64
DA_WIDTH = DA_HEADS * 2 * DA_HEAD_DIM
WG_HEADS = 16
WG_KV_HEADS = 4
WG_HEAD_DIM = 64
WG_Q = WG_HEADS * WG_HEAD_DIM
WG_KV = WG_KV_HEADS * WG_HEAD_DIM
WINDOW = 128
D_FF = 4 * D_MODEL
DEPTH = 1
DEEPNORM_ALPHA = (2.0 * DEPTH) ** 0.25
LN_EPS = 1e-5
LOG2E = math.log2(math.e)
NEG_BIG = -1e30

OFF_DA_Q = 0
OFF_DA_K = OFF_DA_Q + DA_WIDTH
OFF_DA_V = OFF_DA_K + DA_WIDTH
OFF_WG_Q = OFF_DA_V + DA_WIDTH
OFF_WG_K = OFF_WG_Q + WG_Q
OFF_WG_V = OFF_WG_K + WG_KV
OFF_GATE = OFF_WG_V + WG_KV

LANES = 128
VMEM_LIMIT_BYTES = 56 * 1024 * 1024

ROW_TILE = 512
DA_TQ = 256
DA_TK = ROW_TILE
WG_TQ = 256
WG_BAND = WG_TQ + 2 * WINDOW


def _alibi_slopes(n_heads):
    return 2.0 ** (-8.0 * np.arange(1, n_heads + 1) / n_heads)


def _nt_dot(a, b):
    return lax.dot_general(a, b, (((1,), (1,)), ((), ())), preferred_element_type=jnp.float32)


def _dot(a, b):
    return jnp.dot(a, b, preferred_element_type=jnp.float32)


def _proj_kernel(x_ref, wt_ref, wn_ref, qa_ref, va_ref, qb_ref, vb_ref, ka_ref, kb_ref):
    xb = x_ref[...].astype(jnp.bfloat16)
    da_scale = DA_HEAD_DIM ** -0.5 * LOG2E
    wg_scale = WG_HEAD_DIM ** -0.5 * LOG2E
    qa_ref[...] = (_nt_dot(wt_ref[0:DA_WIDTH, :], xb) * da_scale).astype(jnp.bfloat16)
    va_ref[...] = _nt_dot(wt_ref[DA_WIDTH:2 * DA_WIDTH, :], xb).astype(jnp.bfloat16)
    qb_ref[...] = (_nt_dot(wt_ref[2 * DA_WIDTH:2 * DA_WIDTH + WG_Q, :], xb) * wg_scale).astype(jnp.bfloat16)
    vb = _nt_dot(wt_ref[2 * DA_WIDTH + WG_Q:, :], xb).astype(jnp.bfloat16)
    for t in range(ROW_TILE // LANES):
        vb_ref[t] = vb[:, t * LANES:(t + 1) * LANES]
    kn = _dot(xb, wn_ref[...])
    ka_ref[...] = kn[:, :DA_WIDTH].astype(jnp.bfloat16)
    kb_ref[...] = kn[:, DA_WIDTH:].astype(jnp.bfloat16)


def _project(x, w_t, w_n):
    B, S, D = x.shape
    ns = S // ROW_TILE
    rows_t = w_t.shape[0]
    bf = jnp.bfloat16
    return pl.pallas_call(
        _proj_kernel,
        grid=(B, ns),
        in_specs=[
            pl.BlockSpec((None, ROW_TILE, D), lambda b, s: (b, s, 0)),
            pl.BlockSpec((rows_t, D), lambda b, s: (0, 0)),
            pl.BlockSpec((D, DA_WIDTH + WG_KV), lambda b, s: (0, 0)),
        ],
        out_specs=[
            pl.BlockSpec((None, DA_WIDTH, ROW_TILE), lambda b, s: (b, 0, s)),
            pl.BlockSpec((None, None, DA_WIDTH, ROW_TILE), lambda b, s: (b, s, 0, 0)),
            pl.BlockSpec((None, WG_Q, ROW_TILE), lambda b, s: (b, 0, s)),
            pl.BlockSpec((None, ROW_TILE // LANES, WG_KV, LANES), lambda b, s: (b, s, 0, 0)),
            pl.BlockSpec((None, ROW_TILE, DA_WIDTH), lambda b, s: (b, s, 0)),
            pl.BlockSpec((None, ROW_TILE, WG_KV), lambda b, s: (b, s, 0)),
        ],
        out_shape=[
            jax.ShapeDtypeStruct((B, DA_WIDTH, S), bf),
            jax.ShapeDtypeStruct((B, ns, DA_WIDTH, ROW_TILE), bf),
            jax.ShapeDtypeStruct((B, WG_Q, S), bf),
            jax.ShapeDtypeStruct((B, S // LANES, WG_KV, LANES), bf),
            jax.ShapeDtypeStruct((B, S, DA_WIDTH), bf),
            jax.ShapeDtypeStruct((B, S, WG_KV), bf),
        ],
        compiler_params=pltpu.CompilerParams(
            dimension_semantics=("arbitrary", "arbitrary"), vmem_limit_bytes=VMEM_LIMIT_BYTES),
        name="in_proj",
    )(x, w_t, w_n)


def _diff_attn_kernel(slope_ref, q_ref, k_ref, v_ref, lq1_ref, lk1_ref, lq2_ref, lk2_ref, g_ref,
                      o_ref, bias_ref, acc0_ref, acc1_ref, *, n_k, lam_init):
    tq, tk = DA_TQ, DA_TK
    n_ov = tk // tq
    h = pl.program_id(1)
    qi = pl.program_id(2)
    sig = slope_ref[h]

    @pl.when(qi == 0)
    def _():
        jj = lax.broadcasted_iota(jnp.int32, (tk, tq), 0).astype(jnp.float32)
        ii = lax.broadcasted_iota(jnp.int32, (tk, tq), 1).astype(jnp.float32)
        bias_ref[0] = sig * jj
        bias_ref[1] = -sig * jj
        for o in range(n_ov):
            bias_ref[2 + o] = -sig * jnp.abs(ii - jj + float(o * tq))

    q = q_ref[...]
    row = lax.broadcasted_iota(jnp.int32, q.shape, 0)
    zero = jnp.zeros_like(q)
    q0pad = jnp.where(row < DA_HEAD_DIM, q, zero)
    q1pad = jnp.where(row >= DA_HEAD_DIM, q, zero)
    acc0_ref[...] = jnp.zeros_like(acc0_ref)
    acc1_ref[...] = jnp.zeros_like(acc1_ref)

    q_start = qi * tq
    ii_row = lax.broadcasted_iota(jnp.int32, (1, tq), 1).astype(jnp.float32)

    def body(kt, carry):
        m0, l0, m1, l1 = carry
        k_start = kt * tk
        left = k_start + tk <= q_start
        right = k_start >= q_start + tq
        plane = jnp.where(left, 0, jnp.where(right, 1, 2 + (q_start - k_start) // tq))
        gap = (q_start - k_start).astype(jnp.float32)
        c_a = jnp.where(left, -sig * gap, jnp.where(right, sig * gap, 0.0))
        c_b = jnp.where(left, -sig, jnp.where(right, sig, 0.0))
        c_row = c_a + c_b * ii_row

        kk = k_ref[pl.ds(pl.multiple_of(k_start, tk), tk), :]
        vt = v_ref[kt]
        bias = bias_ref[plane]

        def one_map(qpad, m, l, acc_ref):
            u = _dot(kk, qpad) + bias
            m_new = jnp.maximum(m, jnp.max(u, axis=0, keepdims=True) + c_row)
            e = jnp.exp2(u - (m_new - c_row))
            alpha = jnp.exp2(m - m_new)
            l_new = alpha * l + jnp.sum(e, axis=0, keepdims=True)
            acc_ref[...] = alpha * acc_ref[...] + _dot(vt, e.astype(jnp.bfloat16))
            return m_new, l_new

        m0, l0 = one_map(q0pad, m0, l0, acc0_ref)
        m1, l1 = one_map(q1pad, m1, l1, acc1_ref)
        return m0, l0, m1, l1

    init = (jnp.full((1, tq), NEG_BIG, jnp.float32), jnp.zeros((1, tq), jnp.float32),
            jnp.full((1, tq), NEG_BIG, jnp.float32), jnp.zeros((1, tq), jnp.float32))
    m0, l0, m1, l1 = lax.fori_loop(0, n_k, body, init)

    lam = (jnp.exp(jnp.sum(lq1_ref[...] * lk1_ref[...], axis=1, keepdims=True))
           - jnp.exp(jnp.sum(lq2_ref[...] * lk2_ref[...], axis=1, keepdims=True)) + lam_init)
    o = acc0_ref[...] / l0 - lam * (acc1_ref[...] / l1)
    o = o * lax.rsqrt(jnp.mean(o * o, axis=0, keepdims=True) + LN_EPS)
    o = o * g_ref[...] * (1.0 - lam_init)
    o_ref[...] = o.T.astype(o_ref.dtype)


def _diff_attention(q_t, k, v_t, lam_q1, lam_k1, lam_q2, lam_k2, subln_g, lam_init):
    B, _, S = q_t.shape
    hw = 2 * DA_HEAD_DIM
    n_k = S // DA_TK
    slopes2 = jnp.asarray(_alibi_slopes(DA_HEADS) * LOG2E, jnp.float32)
    vec = lambda a: a.reshape(1, DA_HEAD_DIM)
    small = pl.BlockSpec((1, DA_HEAD_DIM), lambda b, h, i, s: (0, 0))
    grid_spec = pltpu.PrefetchScalarGridSpec(
        num_scalar_prefetch=1,
        grid=(B, DA_HEADS, S // DA_TQ),
        in_specs=[
            pl.BlockSpec((None, hw, DA_TQ), lambda b, h, i, s: (b, h, i)),
            pl.BlockSpec((None, S, hw), lambda b, h, i, s: (b, 0, h)),
            pl.BlockSpec((None, n_k, hw, DA_TK), lambda b, h, i, s: (b, 0, h, 0)),
            small, small, small, small,
            pl.BlockSpec((hw, 1), lambda b, h, i, s: (0, 0)),
        ],
        out_specs=pl.BlockSpec((None, DA_TQ, hw), lambda b, h, i, s: (b, i, h)),
        scratch_shapes=[
            pltpu.VMEM((2 + DA_TK // DA_TQ, DA_TK, DA_TQ), jnp.float32),
            pltpu.VMEM((hw, DA_TQ), jnp.float32),
            pltpu.VMEM((hw, DA_TQ), jnp.float32),
        ],
    )
    return pl.pallas_call(
        functools.partial(_diff_attn_kernel, n_k=n_k, lam_init=lam_init),
        grid_spec=grid_spec,
        out_shape=jax.ShapeDtypeStruct((B, S, DA_WIDTH), jnp.bfloat16),
        compiler_params=pltpu.CompilerParams(
            dimension_semantics=("arbitrary", "arbitrary", "arbitrary"),
            vmem_limit_bytes=VMEM_LIMIT_BYTES),
        name="diff_attn",
    )(slopes2, q_t, k, v_t, vec(lam_q1), vec(lam_k1), vec(lam_q2), vec(lam_k2),
      subln_g.reshape(hw, 1))


def _win_attn_kernel(slope_ref, sink_ref, q_ref, k_ref, v_ref, o_ref, ot_ref, *, seq_len):
    tq, band = WG_TQ, WG_BAND
    n_vt = band // LANES
    rep = WG_HEADS // WG_KV_HEADS
    qi = pl.program_id(1)
    q_start = qi * tq
    tile0 = jnp.clip(q_start // LANES - WINDOW // LANES, 0, seq_len // LANES - n_vt)
    k_start = pl.multiple_of(tile0 * LANES, LANES)

    kband = k_ref[pl.ds(k_start, band), :]
    kpos = k_start + lax.broadcasted_iota(jnp.int32, (band, tq), 0)
    qpos = q_start + lax.broadcasted_iota(jnp.int32, (band, tq), 1)
    dist_i = jnp.abs(qpos - kpos)
    valid = dist_i <= WINDOW
    dist = dist_i.astype(jnp.float32)

    for g in range(WG_KV_HEADS):
        for r in range(rep):
            hd = g * rep + r
            qh = q_ref[hd * WG_HEAD_DIM:(hd + 1) * WG_HEAD_DIM, :]
            parts = []
            if g > 0:
                parts.append(jnp.zeros((g * WG_HEAD_DIM, tq), qh.dtype))
            parts.append(qh)
            if g < WG_KV_HEADS - 1:
                parts.append(jnp.zeros(((WG_KV_HEADS - 1 - g) * WG_HEAD_DIM, tq), qh.dtype))
            qpad = jnp.concatenate(parts, axis=0)
            s = _dot(kband, qpad) - slope_ref[hd] * dist
            s = jnp.where(valid, s, NEG_BIG)
            sk = sink_ref[hd] * LOG2E
            m = jnp.maximum(jnp.max(s, axis=0, keepdims=True), sk)
            e = jnp.exp2(s - m)
            denom = jnp.sum(e, axis=0, keepdims=True) + jnp.exp2(sk - m)
            eb = e.astype(jnp.bfloat16)
            acc = jnp.zeros((WG_HEAD_DIM, tq), jnp.float32)
            for t in range(n_vt):
                vt = v_ref[tile0 + t, g * WG_HEAD_DIM:(g + 1) * WG_HEAD_DIM, :]
                acc = acc + _dot(vt, eb[t * LANES:(t + 1) * LANES, :])
            ot_ref[hd * WG_HEAD_DIM:(hd + 1) * WG_HEAD_DIM, :] = acc / denom
    o_ref[...] = ot_ref[...].T.astype(o_ref.dtype)


def _window_attention(q_t, k, v_t, sink_logit):
    B, _, S = q_t.shape
    slopes2 = jnp.asarray(_alibi_slopes(WG_HEADS) * LOG2E, jnp.float32)
    grid_spec = pltpu.PrefetchScalarGridSpec(
        num_scalar_prefetch=2,
        grid=(B, S // WG_TQ),
        in_specs=[
            pl.BlockSpec((None, WG_Q, WG_TQ), lambda b, i, s1, s2: (b, 0, i)),
            pl.BlockSpec((None, S, WG_KV), lambda b, i, s1, s2: (b, 0, 0)),
            pl.BlockSpec((None, S // LANES, WG_KV, LANES), lambda b, i, s1, s2: (b, 0, 0, 0)),
        ],
        out_specs=pl.BlockSpec((None, WG_TQ, WG_Q), lambda b, i, s1, s2: (b, i, 0)),
        scratch_shapes=[pltpu.VMEM((WG_Q, WG_TQ), jnp.float32)],
    )
    return pl.pallas_call(
        functools.partial(_win_attn_kernel, seq_len=S),
        grid_spec=grid_spec,
        out_shape=jax.ShapeDtypeStruct((B, S, WG_Q), jnp.bfloat16),
        compiler_params=pltpu.CompilerParams(
            dimension_semantics=("arbitrary", "arbitrary"), vmem_limit_bytes=VMEM_LIMIT_BYTES),
        name="win_attn",
    )(slopes2, sink_logit.astype(jnp.float32), q_t, k, v_t)


def _layer_norm(x, g, b):
    mu = jnp.mean(x, axis=-1, keepdims=True)
    xc = x - mu
    var = jnp.mean(xc * xc, axis=-1, keepdims=True)
    return xc * lax.rsqrt(var + LN_EPS) * g + b


def _merge_kernel(x_ref, oa_ref, ob_ref, wg_ref, bg_ref, wa_ref, wb_ref, wo_ref, g_ref, b_ref, y_ref):
    x = x_ref[...]
    gates = jax.nn.sigmoid(_dot(x.astype(jnp.bfloat16), wg_ref[...]) + bg_ref[...])
    merged = (gates[:, :D_MODEL] * _dot(oa_ref[...], wa_ref[...])
              + gates[:, D_MODEL:] * _dot(ob_ref[...], wb_ref[...]))
    mix = _dot(merged.astype(jnp.bfloat16), wo_ref[...])
    y_ref[...] = _layer_norm(DEEPNORM_ALPHA * x + mix, g_ref[...], b_ref[...])


def _merge(x, o_a, o_b, w_gate, b_gate, w_br_a, w_br_b, w_out, ln_g, ln_b):
    B, S, D = x.shape
    tok = lambda b, s: (b, s, 0)
    const = lambda b, s: (0, 0)
    return pl.pallas_call(
        _merge_kernel,
        grid=(B, S // ROW_TILE),
        in_specs=[
            pl.BlockSpec((None, ROW_TILE, D), tok),
            pl.BlockSpec((None, ROW_TILE, DA_WIDTH), tok),
            pl.BlockSpec((None, ROW_TILE, WG_Q), tok),
            pl.BlockSpec((D, 2 * D), const),
            pl.BlockSpec((1, 2 * D), const),
            pl.BlockSpec((DA_WIDTH, D), const),
            pl.BlockSpec((WG_Q, D), const),
            pl.BlockSpec((D, D), const),
            pl.BlockSpec((1, D), const),
            pl.BlockSpec((1, D), const),
        ],
        out_specs=pl.BlockSpec((None, ROW_TILE, D), tok),
        out_shape=jax.ShapeDtypeStruct((B, S, D), jnp.float32),
        compiler_params=pltpu.CompilerParams(
            dimension_semantics=("arbitrary", "arbitrary"), vmem_limit_bytes=VMEM_LIMIT_BYTES),
        name="merge_ln1",
    )(x, o_a, o_b, w_gate, b_gate, w_br_a, w_br_b, w_out, ln_g, ln_b)


def _ffn_kernel(x_ref, w1_ref, b1_ref, w2_ref, b2_ref, g_ref, b_ref, y_ref):
    x = x_ref[...]
    h = jnp.maximum(_dot(x.astype(jnp.bfloat16), w1_ref[...]) + b1_ref[...], 0.0)
    f = _dot((h * h).astype(jnp.bfloat16), w2_ref[...]) + b2_ref[...]
    y_ref[...] = _layer_norm(DEEPNORM_ALPHA * x + f, g_ref[...], b_ref[...])


def _ffn(x, w1, b1, w2, b2, ln_g, ln_b):
    B, S, D = x.shape
    tok = lambda b, s: (b, s, 0)
    const = lambda b, s: (0, 0)
    return pl.pallas_call(
        _ffn_kernel,
        grid=(B, S // ROW_TILE),
        in_specs=[
            pl.BlockSpec((None, ROW_TILE, D), tok),
            pl.BlockSpec((D, D_FF), const),
            pl.BlockSpec((1, D_FF), const),
            pl.BlockSpec((D_FF, D), const),
            pl.BlockSpec((1, D), const),
            pl.BlockSpec((1, D), const),
            pl.BlockSpec((1, D), const),
        ],
        out_specs=pl.BlockSpec((None, ROW_TILE, D), tok),
        out_shape=jax.ShapeDtypeStruct((B, S, D), jnp.float32),
        compiler_params=pltpu.CompilerParams(
            dimension_semantics=("arbitrary", "arbitrary"), vmem_limit_bytes=VMEM_LIMIT_BYTES),
        name="ffn_ln2",
    )(x, w1, b1, w2, b2, ln_g, ln_b)


def _encoder_layer(x, l, w_in, b_gate, lam_q1, lam_k1, lam_q2, lam_k2, subln_g, sink_logit,
                   w_br_a, w_br_b, w_out, ln1_g, ln1_b, w_ff1, b_ff1, w_ff2, b_ff2, ln2_g, ln2_b):
    bf = jnp.bfloat16
    row = lambda a: a.reshape(1, -1)
    w_t = jnp.concatenate([w_in[:, OFF_DA_Q:OFF_DA_K], w_in[:, OFF_DA_V:OFF_WG_Q],
                           w_in[:, OFF_WG_Q:OFF_WG_K], w_in[:, OFF_WG_V:OFF_GATE]], axis=1).T.astype(bf)
    w_n = jnp.concatenate([w_in[:, OFF_DA_K:OFF_DA_V], w_in[:, OFF_WG_K:OFF_WG_V]], axis=1).astype(bf)
    w_gate = w_in[:, OFF_GATE:].astype(bf)
    lam_init = 0.8 - 0.6 * math.exp(-0.3 * l)

    qa_t, va_t, qb_t, vb_t, k_a, k_b = _project(x, w_t, w_n)
    o_a = _diff_attention(qa_t, k_a, va_t, lam_q1, lam_k1, lam_q2, lam_k2, subln_g, lam_init)
    o_b = _window_attention(qb_t, k_b, vb_t, sink_logit)
    x1 = _merge(x, o_a, o_b, w_gate, row(b_gate), w_br_a.astype(bf), w_br_b.astype(bf),
                w_out.astype(bf), row(ln1_g), row(ln1_b))
    return _ffn(x1, w_ff1.astype(bf), row(b_ff1), w_ff2.astype(bf), row(b_ff2), row(ln2_g), row(ln2_b))


def kernel(x_prompt, x_sample, w_in, b_gate, lam_q1, lam_k1, lam_q2, lam_k2, subln_g, sink_logit,
           w_br_a, w_br_b, w_out, ln1_g, ln1_b, w_ff1, b_ff1, w_ff2, b_ff2, ln2_g, ln2_b):
    def run(x):
        for l in range(DEPTH):
            x = _encoder_layer(x, l, w_in[l], b_gate[l], lam_q1[l], lam_k1[l], lam_q2[l], lam_k2[l],
                               subln_g[l], sink_logit[l], w_br_a[l], w_br_b[l], w_out[l],
                               ln1_g[l], ln1_b[l], w_ff1[l], b_ff1[l], w_ff2[l], b_ff2[l],
                               ln2_g[l], ln2_b[l])
        return x

    return (run(x_prompt), run(x_sample))
```

```python
import functools
import math

import jax
import jax.numpy as jnp
import numpy as np
from jax import lax
from jax.experimental import pallas as pl
from jax.experimental.pallas import tpu as pltpu

D_MODEL = 1024
DA_HEADS = 8
DA_HEAD_DIM = 64
DA_WIDTH = DA_HEADS * 2 * DA_HEAD_DIM
WG_HEADS = 16
WG_KV_HEADS = 4
WG_HEAD_DIM = 64
WG_Q = WG_HEADS * WG_HEAD_DIM
WG_KV = WG_KV_HEADS * WG_HEAD_DIM
WINDOW = 128
D_FF = 4 * D_MODEL
DEPTH = 1
DEEPNORM_ALPHA = (2.0 * DEPTH) ** 0.25
LN_EPS = 1e-5
LOG2E = math.log2(math.e)
NEG_BIG = -1e30

OFF_DA_Q = 0
OFF_DA_K = OFF_DA_Q + DA_WIDTH
OFF_DA_V = OFF_DA_K + DA_WIDTH
OFF_WG_Q = OFF_DA_V + DA_WIDTH
OFF_WG_K = OFF_WG_Q + WG_Q
OFF_WG_V = OFF_WG_K + WG_KV
OFF_GATE = OFF_WG_V + WG_KV

LANES = 128
SUBLANES = 8
BF16_ROWS = 16
VMEM_LIMIT_BYTES = 56 * 1024 * 1024

ROW_TILE = 512
DA_TQ = 256
DA_TK = ROW_TILE
WG_TQ = 256
WG_BAND = WG_TQ + 2 * WINDOW

DA_KCOLS = 2 * DA_HEAD_DIM
AUG_ROWS = BF16_ROWS
SKIP_LOG2 = -150.0


def _alibi_slopes(n_heads):
    return 2.0 ** (-8.0 * np.arange(1, n_heads + 1) / n_heads)


def _nt_dot(a, b):
    return lax.dot_general(a, b, (((1,), (1,)), ((), ())), preferred_element_type=jnp.float32)


def _dot(a, b):
    return jnp.dot(a, b, preferred_element_type=jnp.float32)


def _proj_kernel(x_ref, wt_ref, wka_ref, wkb_ref, qa_ref, va_ref, qb_ref, vb_ref, kat_ref, ka_ref, kb_ref):
    xb = x_ref[...].astype(jnp.bfloat16)
    da_scale = DA_HEAD_DIM ** -0.5 * LOG2E
    wg_scale = WG_HEAD_DIM ** -0.5 * LOG2E
    r0, r1, r2, r3 = DA_WIDTH, 2 * DA_WIDTH, 2 * DA_WIDTH + WG_Q, 2 * DA_WIDTH + WG_Q + WG_KV
    qa_ref[...] = (_nt_dot(wt_ref[0:r0, :], xb) * da_scale).astype(jnp.bfloat16)
    va_ref[...] = _nt_dot(wt_ref[r0:r1, :], xb).astype(jnp.bfloat16)
    qb_ref[...] = (_nt_dot(wt_ref[r1:r2, :], xb) * wg_scale).astype(jnp.bfloat16)
    kat_ref[...] = _nt_dot(wt_ref[r3:, :], xb).astype(jnp.bfloat16)
    vb = _nt_dot(wt_ref[r2:r3, :], xb).astype(jnp.bfloat16)
    for t in range(ROW_TILE // LANES):
        vb_ref[t] = vb[:, t * LANES:(t + 1) * LANES]
    kb_ref[...] = _dot(xb, wkb_ref[...]).astype(jnp.bfloat16)
    pos = pl.program_id(1) * ROW_TILE + lax.broadcasted_iota(jnp.int32, (ROW_TILE, DA_KCOLS), 0)
    lane = lax.broadcasted_iota(jnp.int32, (ROW_TILE, DA_KCOLS), 1)
    pos_hi = ((pos // LANES) * LANES).astype(jnp.float32)
    pos_lo = (pos % LANES).astype(jnp.float32)
    c0 = DA_HEAD_DIM
    pat = jnp.where(lane < c0, 0.0,
                    jnp.where(lane < c0 + 3, 1.0,
                              jnp.where(lane < c0 + 6, pos_hi,
                                        jnp.where(lane < c0 + 9, pos_lo, 0.0))))
    for blk in range(2 * DA_HEADS):
        cols = slice(blk * DA_KCOLS, (blk + 1) * DA_KCOLS)
        ka_ref[:, cols] = (_dot(xb, wka_ref[:, cols]) + pat).astype(jnp.bfloat16)


def _project(x, w_t, w_ka, w_kb):
    B, S, D = x.shape
    ns = S // ROW_TILE
    rows_t = w_t.shape[0]
    ka_cols = w_ka.shape[1]
    bf = jnp.bfloat16
    return pl.pallas_call(
        _proj_kernel,
        grid=(B, ns),
        in_specs=[
            pl.BlockSpec((None, ROW_TILE, D), lambda b, s: (b, s, 0)),
            pl.BlockSpec((rows_t, D), lambda b, s: (0, 0)),
            pl.BlockSpec((D, ka_cols), lambda b, s: (0, 0)),
            pl.BlockSpec((D, WG_KV), lambda b, s: (0, 0)),
        ],
        out_specs=[
            pl.BlockSpec((None, DA_WIDTH, ROW_TILE), lambda b, s: (b, 0, s)),
            pl.BlockSpec((None, None, DA_WIDTH, ROW_TILE), lambda b, s: (b, s, 0, 0)),
            pl.BlockSpec((None, WG_Q, ROW_TILE), lambda b, s: (b, 0, s)),
            pl.BlockSpec((None, ROW_TILE // LANES, WG_KV, LANES), lambda b, s: (b, s, 0, 0)),
            pl.BlockSpec((None, DA_WIDTH, ROW_TILE), lambda b, s: (b, 0, s)),
            pl.BlockSpec((None, ROW_TILE, ka_cols), lambda b, s: (b, s, 0)),
            pl.BlockSpec((None, ROW_TILE, WG_KV), lambda b, s: (b, s, 0)),
        ],
        out_shape=[
            jax.ShapeDtypeStruct((B, DA_WIDTH, S), bf),
            jax.ShapeDtypeStruct((B, ns, DA_WIDTH, ROW_TILE), bf),
            jax.ShapeDtypeStruct((B, WG_Q, S), bf),
            jax.ShapeDtypeStruct((B, S // LANES, WG_KV, LANES), bf),
            jax.ShapeDtypeStruct((B, DA_WIDTH, S), bf),
            jax.ShapeDtypeStruct((B, S, ka_cols), bf),
            jax.ShapeDtypeStruct((B, S, WG_KV), bf),
        ],
        compiler_params=pltpu.CompilerParams(
            dimension_semantics=("arbitrary", "arbitrary"), vmem_limit_bytes=VMEM_LIMIT_BYTES),
        name="in_proj",
    )(x, w_t, w_ka, w_kb)


def _split3(x):
    hi = x.astype(jnp.bfloat16).astype(jnp.float32)
    r1 = x - hi
    mid = r1.astype(jnp.bfloat16).astype(jnp.float32)
    return hi, mid, r1 - mid


def _diff_attn_kernel(slope_ref, islope_ref, q_ref, kt_ref, k_ref, v_ref, lq1_ref, lk1_ref, lq2_ref, lk2_ref, g_ref,
                      o_ref, kmax_ref, qaug_ref, sa_ref, sb_ref, acc_ref, *, n_k, lam_init):
    tq, tk, dh = DA_TQ, DA_TK, DA_HEAD_DIM
    h = pl.program_id(1)
    qi = pl.program_id(2)
    sig = slope_ref[h]

    def k_tile(kt):
        return k_ref[pl.ds(pl.multiple_of(kt * tk, tk), tk), :]

    def col_sum8(e):
        return jnp.sum(e.reshape(tk // SUBLANES, SUBLANES, e.shape[1]), axis=0)

    def both_maps(m0, m1):
        z = jnp.zeros_like(m0)
        return jnp.concatenate([jnp.concatenate([m0, z], axis=1), jnp.concatenate([z, m1], axis=1)], axis=0)

    @pl.when(qi == 0)
    def _():
        def kmax_body(kt, m):
            a = jnp.abs(k_tile(kt).astype(jnp.float32)).reshape(tk // SUBLANES, SUBLANES, 2 * DA_KCOLS)
            return jnp.maximum(m, jnp.max(a, axis=0))
        m8 = lax.fori_loop(0, n_k, kmax_body, jnp.zeros((SUBLANES, 2 * DA_KCOLS), jnp.float32))
        kmax_ref[...] = jnp.broadcast_to(jnp.max(m8, axis=0, keepdims=True), kmax_ref.shape)

    q = q_ref[...]
    qc = (q[:dh], q[dh:])
    zpad = jnp.zeros((DA_KCOLS - dh, tq), q.dtype)
    qpad = both_maps(jnp.concatenate([qc[0], zpad], axis=0),
                     jnp.concatenate([qc[1], zpad], axis=0))

    q_start = qi * tq
    kt_ov = q_start // tk

    qk_self = q.astype(jnp.float32) * kt_ref[...].astype(jnp.float32)
    r = jnp.concatenate([jnp.sum(qk_self[:dh], axis=0, keepdims=True),
                         jnp.sum(qk_self[dh:], axis=0, keepdims=True)], axis=1)

    bound = _dot(kmax_ref[...].astype(jnp.bfloat16), jnp.abs(qpad))[0:1] * 1.01
    gap = jnp.max(bound - r, axis=1, keepdims=True)
    reach = jnp.minimum((gap - SKIP_LOG2) * islope_ref[h], float(2 * n_k * tk))
    qs_f = q_start.astype(jnp.float32)
    lo_f = jnp.floor((qs_f - reach - 1.0) * (1.0 / tk))
    hi_f = jnp.floor((qs_f + float(tq) + reach) * (1.0 / tk))
    lo_t = jnp.minimum(jnp.clip(lo_f, 0.0, float(n_k)).astype(jnp.int32)[0, 0], kt_ov)
    hi_t = jnp.maximum(jnp.clip(hi_f, -1.0, float(n_k - 1)).astype(jnp.int32)[0, 0], kt_ov)
    n_left = kt_ov - lo_t
    n_tot = n_left + hi_t - kt_ov

    ipos = (q_start + lax.broadcasted_iota(jnp.int32, (1, tq), 1)).astype(jnp.float32)
    row = lax.broadcasted_iota(jnp.int32, (AUG_ROWS, tq), 0)
    sig_row = jnp.full((1, tq), sig, jnp.float32)
    zrest = jnp.zeros((DA_KCOLS - dh - AUG_ROWS, tq), q.dtype)
    for side, sgn in enumerate((1.0, -1.0)):
        s3 = _split3(sgn * sig_row)
        maps = []
        for c in range(2):
            r3 = _split3(-(r[:, c * tq:(c + 1) * tq] + sgn * sig * ipos))
            slab = jnp.zeros((AUG_ROWS, tq), jnp.float32)
            for i, piece in enumerate(r3 + s3 + s3):
                slab = jnp.where(row == i, piece, slab)
            maps.append(jnp.concatenate([qc[c], slab.astype(q.dtype), zrest], axis=0))
        qaug_ref[side] = both_maps(*maps)

    def tile_of(i):
        i = jnp.clip(i, 0, jnp.maximum(n_tot - 1, 0))
        kt = jnp.where(i < n_left, lo_t + i, kt_ov + 1 + i - n_left)
        return jnp.minimum(kt, n_k - 1), (i >= n_left).astype(jnp.int32)

    def scores(i):
        kt, side = tile_of(i)
        return _dot(k_tile(kt), qaug_ref[side])

    def consume(i, s_ref, l8):
        kt, _ = tile_of(i)
        e = jnp.exp2(s_ref[...])
        acc_ref[...] += _dot(v_ref[kt], e.astype(jnp.bfloat16))
        return l8 + col_sum8(e)

    k_ov = k_tile(kt_ov)
    s_ov = jnp.minimum(_dot(k_ov, qaug_ref[0]), _dot(k_ov, qaug_ref[1]))
    sa_ref[...] = scores(0)
    e = jnp.exp2(s_ov)
    l8 = col_sum8(e)
    acc_ref[...] = _dot(v_ref[kt_ov], e.astype(jnp.bfloat16))

    def pair(p, l8):
        i = 2 * p
        sb_ref[...] = scores(i + 1)
        l8 = consume(i, sa_ref, l8)
        sa_ref[...] = scores(i + 2)
        return consume(i + 1, sb_ref, l8)

    l8 = lax.fori_loop(0, n_tot // 2, pair, l8)
    l8 = lax.cond(n_tot % 2 == 1, lambda: consume(n_tot - 1, sa_ref, l8), lambda: l8)
    l = jnp.sum(l8, axis=0, keepdims=True)

    chk = jnp.sum(acc_ref[...] * 0.0) + jnp.sum(l * 0.0)
    overflowed = jnp.logical_not(chk == 0.0)

    def exact_path():
        dd = (lax.broadcasted_iota(jnp.int32, (tk, tq), 1)
              - lax.broadcasted_iota(jnp.int32, (tk, tq), 0)).astype(jnp.float32)
        acc_ref[...] = jnp.zeros_like(acc_ref)

        def body(kt, carry):
            m, lc = carry
            b = -sig * jnp.abs(dd + (q_start - kt * tk).astype(jnp.float32))
            u = _dot(k_tile(kt), qpad) + jnp.concatenate([b, b], axis=1)
            m_new = jnp.maximum(m, jnp.max(u, axis=0, keepdims=True))
            e = jnp.exp2(u - m_new)
            alpha = jnp.exp2(m - m_new)
            acc_ref[...] = alpha * acc_ref[...] + _dot(v_ref[kt], e.astype(jnp.bfloat16))
            return m_new, alpha * lc + jnp.sum(e, axis=0, keepdims=True)

        init = (jnp.full((1, 2 * tq), NEG_BIG, jnp.float32), jnp.zeros((1, 2 * tq), jnp.float32))
        return lax.fori_loop(0, n_k, body, init)[1]

    l = lax.cond(overflowed, exact_path, lambda: l)

    lam = (jnp.exp(jnp.sum(lq1_ref[...] * lk1_ref[...], axis=1, keepdims=True))
           - jnp.exp(jnp.sum(lq2_ref[...] * lk2_ref[...], axis=1, keepdims=True)) + lam_init)
    on = acc_ref[...] / l
    o = on[:, :tq] - lam * on[:, tq:]
    o = o * lax.rsqrt(jnp.mean(o * o, axis=0, keepdims=True) + LN_EPS)
    o = o * g_ref[...] * (1.0 - lam_init)
    o_ref[...] = o.T.astype(o_ref.dtype)


def _diff_attention(q_t, k_t, k, v_t, lam_q1, lam_k1, lam_q2, lam_k2, subln_g, lam_init):
    B, _, S = q_t.shape
    hw = 2 * DA_HEAD_DIM
    n_k = S // DA_TK
    slopes2 = _alibi_slopes(DA_HEADS) * LOG2E
    vec = lambda a: a.reshape(1, DA_HEAD_DIM)
    small = pl.BlockSpec((1, DA_HEAD_DIM), lambda b, h, i, s1, s2: (0, 0))
    grid_spec = pltpu.PrefetchScalarGridSpec(
        num_scalar_prefetch=2,
        grid=(B, DA_HEADS, S // DA_TQ),
        in_specs=[
            pl.BlockSpec((None, hw, DA_TQ), lambda b, h, i, s1, s2: (b, h, i)),
            pl.BlockSpec((None, hw, DA_TQ), lambda b, h, i, s1, s2: (b, h, i)),
            pl.BlockSpec((None, S, 2 * DA_KCOLS), lambda b, h, i, s1, s2: (b, 0, h)),
            pl.BlockSpec((None, n_k, hw, DA_TK), lambda b, h, i, s1, s2: (b, 0, h, 0)),
            small, small, small, small,
            pl.BlockSpec((hw, 1), lambda b, h, i, s1, s2: (0, 0)),
        ],
        out_specs=pl.BlockSpec((None, DA_TQ, hw), lambda b, h, i, s1, s2: (b, i, h)),
        scratch_shapes=[
            pltpu.VMEM((SUBLANES, 2 * DA_KCOLS), jnp.float32),
            pltpu.VMEM((2, 2 * DA_KCOLS, 2 * DA_TQ), jnp.bfloat16),
            pltpu.VMEM((DA_TK, 2 * DA_TQ), jnp.float32),
            pltpu.VMEM((DA_TK, 2 * DA_TQ), jnp.float32),
            pltpu.VMEM((hw, 2 * DA_TQ), jnp.float32),
        ],
    )
    return pl.pallas_call(
        functools.partial(_diff_attn_kernel, n_k=n_k, lam_init=lam_init),
        grid_spec=grid_spec,
        out_shape=jax.ShapeDtypeStruct((B, S, DA_WIDTH), jnp.bfloat16),
        compiler_params=pltpu.CompilerParams(
            dimension_semantics=("arbitrary", "arbitrary", "arbitrary"),
            vmem_limit_bytes=VMEM_LIMIT_BYTES),
        name="diff_attn",
    )(jnp.asarray(slopes2, jnp.float32), jnp.asarray(1.0 / slopes2, jnp.float32),
      q_t, k_t, k, v_t, vec(lam_q1), vec(lam_k1), vec(lam_q2), vec(lam_k2), subln_g.reshape(hw, 1))


def _win_attn_kernel(slope_ref, sink_ref, q_ref, k_ref, v_ref, o_ref, ot_ref, *, seq_len):
    tq, band = WG_TQ, WG_BAND
    n_vt = band // LANES
    rep = WG_HEADS // WG_KV_HEADS
    qi = pl.program_id(1)
    q_start = qi * tq
    tile0 = jnp.clip(q_start // LANES - WINDOW // LANES, 0, seq_len // LANES - n_vt)
    k_start = pl.multiple_of(tile0 * LANES, LANES)

    kband = k_ref[pl.ds(k_start, band), :]
    kpos = k_start + lax.broadcasted_iota(jnp.int32, (band, tq), 0)
    qpos = q_start + lax.broadcasted_iota(jnp.int32, (band, tq), 1)
    dist_i = jnp.abs(qpos - kpos)
    valid = dist_i <= WINDOW
    dist = dist_i.astype(jnp.float32)

    for g in range(WG_KV_HEADS):
        for r in range(rep):
            hd = g * rep + r
            qh = q_ref[hd * WG_HEAD_DIM:(hd + 1) * WG_HEAD_DIM, :]
            parts = []
            if g > 0:
                parts.append(jnp.zeros((g * WG_HEAD_DIM, tq), qh.dtype))
            parts.append(qh)
            if g < WG_KV_HEADS - 1:
                parts.append(jnp.zeros(((WG_KV_HEADS - 1 - g) * WG_HEAD_DIM, tq), qh.dtype))
            qpad = jnp.concatenate(parts, axis=0)
            s = _dot(kband, qpad) - slope_ref[hd] * dist
            s = jnp.where(valid, s, NEG_BIG)
            sk = sink_ref[hd] * LOG2E
            m = jnp.maximum(jnp.max(s, axis=0, keepdims=True), sk)
            e = jnp.exp2(s - m)
            denom = jnp.sum(e, axis=0, keepdims=True) + jnp.exp2(sk - m)
            eb = e.astype(jnp.bfloat16)
            acc = jnp.zeros((WG_HEAD_DIM, tq), jnp.float32)
            for t in range(n_vt):
                vt = v_ref[tile0 + t, g * WG_HEAD_DIM:(g + 1) * WG_HEAD_DIM, :]
                acc = acc + _dot(vt, eb[t * LANES:(t + 1) * LANES, :])
            ot_ref[hd * WG_HEAD_DIM:(hd + 1) * WG_HEAD_DIM, :] = acc / denom
    o_ref[...] = ot_ref[...].T.astype(o_ref.dtype)


def _window_attention(q_t, k, v_t, sink_logit):
    B, _, S = q_t.shape
    slopes2 = jnp.asarray(_alibi_slopes(WG_HEADS) * LOG2E, jnp.float32)
    grid_spec = pltpu.PrefetchScalarGridSpec(
        num_scalar_prefetch=2,
        grid=(B, S // WG_TQ),
        in_specs=[
            pl.BlockSpec((None, WG_Q, WG_TQ), lambda b, i, s1, s2: (b, 0, i)),
            pl.BlockSpec((None, S, WG_KV), lambda b, i, s1, s2: (b, 0, 0)),
            pl.BlockSpec((None, S // LANES, WG_KV, LANES), lambda b, i, s1, s2: (b, 0, 0, 0)),
        ],
        out_specs=pl.BlockSpec((None, WG_TQ, WG_Q), lambda b, i, s1, s2: (b, i, 0)),
        scratch_shapes=[pltpu.VMEM((WG_Q, WG_TQ), jnp.float32)],
    )
    return pl.pallas_call(
        functools.partial(_win_attn_kernel, seq_len=S),
        grid_spec=grid_spec,
        out_shape=jax.ShapeDtypeStruct((B, S, WG_Q), jnp.bfloat16),
        compiler_params=pltpu.CompilerParams(
            dimension_semantics=("arbitrary", "arbitrary"), vmem_limit_bytes=VMEM_LIMIT_BYTES),
        name="win_attn",
    )(slopes2, sink_logit.astype(jnp.float32), q_t, k, v_t)


def _layer_norm(x, g, b):
    mu = jnp.mean(x, axis=-1, keepdims=True)
    xc = x - mu
    var = jnp.mean(xc * xc, axis=-1, keepdims=True)
    return xc * lax.rsqrt(var + LN_EPS) * g + b


def _merge_kernel(x_ref, oa_ref, ob_ref, wg_ref, bg_ref, wa_ref, wb_ref, wo_ref, g_ref, b_ref, y_ref):
    x = x_ref[...]
    gates = jax.nn.sigmoid(_dot(x.astype(jnp.bfloat16), wg_ref[...]) + bg_ref[...])
    merged = (gates[:, :D_MODEL] * _dot(oa_ref[...], wa_ref[...])
              + gates[:, D_MODEL:] * _dot(ob_ref[...], wb_ref[...]))
    mix = _dot(merged.astype(jnp.bfloat16), wo_ref[...])
    y_ref[...] = _layer_norm(DEEPNORM_ALPHA * x + mix, g_ref[...], b_ref[...])


def _merge(x, o_a, o_b, w_gate, b_gate, w_br_a, w_br_b, w_out, ln_g, ln_b):
    B, S, D = x.shape
    tok = lambda b, s: (b, s, 0)
    const = lambda b, s: (0, 0)
    return pl.pallas_call(
        _merge_kernel,
        grid=(B, S // ROW_TILE),
        in_specs=[
            pl.BlockSpec((None, ROW_TILE, D), tok),
            pl.BlockSpec((None, ROW_TILE, DA_WIDTH), tok),
            pl.BlockSpec((None, ROW_TILE, WG_Q), tok),
            pl.BlockSpec((D, 2 * D), const),
            pl.BlockSpec((1, 2 * D), const),
            pl.BlockSpec((DA_WIDTH, D), const),
            pl.BlockSpec((WG_Q, D), const),
            pl.BlockSpec((D, D), const),
            pl.BlockSpec((1, D), const),
            pl.BlockSpec((1, D), const),
        ],
        out_specs=pl.BlockSpec((None, ROW_TILE, D), tok),
        out_shape=jax.ShapeDtypeStruct((B, S, D), jnp.float32),
        compiler_params=pltpu.CompilerParams(
            dimension_semantics=("arbitrary", "arbitrary"), vmem_limit_bytes=VMEM_LIMIT_BYTES),
        name="merge_ln1",
    )(x, o_a, o_b, w_gate, b_gate, w_br_a, w_br_b, w_out, ln_g, ln_b)


def _ffn_kernel(x_ref, w1_ref, b1_ref, w2_ref, b2_ref, g_ref, b_ref, y_ref):
    x = x_ref[...]
    h = jnp.maximum(_dot(x.astype(jnp.bfloat16), w1_ref[...]) + b1_ref[...], 0.0)
    f = _dot((h * h).astype(jnp.bfloat16), w2_ref[...]) + b2_ref[...]
    y_ref[...] = _layer_norm(DEEPNORM_ALPHA * x + f, g_ref[...], b_ref[...])


def _ffn(x, w1, b1, w2, b2, ln_g, ln_b):
    B, S, D = x.shape
    tok = lambda b, s: (b, s, 0)
    const = lambda b, s: (0, 0)
    return pl.pallas_call(
        _ffn_kernel,
        grid=(B, S // ROW_TILE),
        in_specs=[
            pl.BlockSpec((None, ROW_TILE, D), tok),
            pl.BlockSpec((D, D_FF), const),
            pl.BlockSpec((1, D_FF), const),
            pl.BlockSpec((D_FF, D), const),
            pl.BlockSpec((1, D), const),
            pl.BlockSpec((1, D), const),
            pl.BlockSpec((1, D), const),
        ],
        out_specs=pl.BlockSpec((None, ROW_TILE, D), tok),
        out_shape=jax.ShapeDtypeStruct((B, S, D), jnp.float32),
        compiler_params=pltpu.CompilerParams(
            dimension_semantics=("arbitrary", "arbitrary"), vmem_limit_bytes=VMEM_LIMIT_BYTES),
        name="ffn_ln2",
    )(x, w1, b1, w2, b2, ln_g, ln_b)


def _encoder_layer(x, l, w_in, b_gate, lam_q1, lam_k1, lam_q2, lam_k2, subln_g, sink_logit,
                   w_br_a, w_br_b, w_out, ln1_g, ln1_b, w_ff1, b_ff1, w_ff2, b_ff2, ln2_g, ln2_b):
    bf = jnp.bfloat16
    row = lambda a: a.reshape(1, -1)
    w_t = jnp.concatenate([w_in[:, OFF_DA_Q:OFF_DA_K], w_in[:, OFF_DA_V:OFF_WG_Q],
                           w_in[:, OFF_WG_Q:OFF_WG_K], w_in[:, OFF_WG_V:OFF_GATE],
                           w_in[:, OFF_DA_K:OFF_DA_V]], axis=1).T.astype(bf)
    w_ka = w_in[:, OFF_DA_K:OFF_DA_V].reshape(D_MODEL, 2 * DA_HEADS, DA_HEAD_DIM)
    w_ka = jnp.pad(w_ka, ((0, 0), (0, 0), (0, DA_KCOLS - DA_HEAD_DIM))).reshape(D_MODEL, -1).astype(bf)
    w_kb = w_in[:, OFF_WG_K:OFF_WG_V].astype(bf)
    w_gate = w_in[:, OFF_GATE:].astype(bf)
    lam_init = 0.8 - 0.6 * math.exp(-0.3 * l)

    qa_t, va_t, qb_t, vb_t, ka_t, k_a, k_b = _project(x, w_t, w_ka, w_kb)
    o_a = _diff_attention(qa_t, ka_t, k_a, va_t, lam_q1, lam_k1, lam_q2, lam_k2, subln_g, lam_init)
    o_b = _window_attention(qb_t, k_b, vb_t, sink_logit)
    x1 = _merge(x, o_a, o_b, w_gate, row(b_gate), w_br_a.astype(bf), w_br_b.astype(bf),
                w_out.astype(bf), row(ln1_g), row(ln1_b))
    return _ffn(x1, w_ff1.astype(bf), row(b_ff1), w_ff2.astype(bf), row(b_ff2), row(ln2_g), row(ln2_b))


def kernel(x_prompt, x_sample, w_in, b_gate, lam_q1, lam_k1, lam_q2, lam_k2, subln_g, sink_logit,
           w_br_a, w_br_b, w_out, ln1_g, ln1_b, w_ff1, b_ff1, w_ff2, b_ff2, ln2_g, ln2_b):
    def run(x):
        for l in range(DEPTH):
            x = _encoder_layer(x, l, w_in[l], b_gate[l], lam_q1[l], lam_k1[l], lam_q2[l], lam_k2[l],
                               subln_g[l], sink_logit[l], w_br_a[l], w_br_b[l], w_out[l],
                               ln1_g[l], ln1_b[l], w_ff1[l], b_ff1[l], w_ff2[l], b_ff2[l],
                               ln2_g[l], ln2_b[l])
        return x

    return (run(x_prompt), run(x_sample))
```

```python
import functools
import math

import jax
import jax.numpy as jnp
import numpy as np
from jax import lax
from jax.experimental import pallas as pl
from jax.experimental.pallas import tpu as pltpu

D_MODEL = 1024
DA_HEADS = 8
DA_HEAD_DIM = 64
DA_WIDTH = DA_HEADS * 2 * DA_HEAD_DIM
WG_HEADS = 16
WG_KV_HEADS = 4
WG_HEAD_DIM = 64
WG_Q = WG_HEADS * WG_HEAD_DIM
WG_KV = WG_KV_HEADS * WG_HEAD_DIM
WINDOW = 128
D_FF = 4 * D_MODEL
DEPTH = 1
DEEPNORM_ALPHA = (2.0 * DEPTH) ** 0.25
LN_EPS = 1e-5
LOG2E = math.log2(math.e)
NEG_BIG = -1e30

OFF_DA_Q = 0
OFF_DA_K = OFF_DA_Q + DA_WIDTH
OFF_DA_V = OFF_DA_K + DA_WIDTH
OFF_WG_Q = OFF_DA_V + DA_WIDTH
OFF_WG_K = OFF_WG_Q + WG_Q
OFF_WG_V = OFF_WG_K + WG_KV
OFF_GATE = OFF_WG_V + WG_KV

LANES = 128
SUBLANES = 8
BF16_ROWS = 16
VMEM_LIMIT_BYTES = 56 * 1024 * 1024

ROW_TILE = 512
DA_TQ = 512
DA_TK = ROW_TILE
WG_TQ = 256
WG_BAND = WG_TQ + 2 * WINDOW

DA_KCOLS = 2 * DA_HEAD_DIM
AUG_ROWS = BF16_ROWS
SKIP_LOG2 = -150.0


def _alibi_slopes(n_heads):
    return 2.0 ** (-8.0 * np.arange(1, n_heads + 1) / n_heads)


def _nt_dot(a, b):
    return lax.dot_general(a, b, (((1,), (1,)), ((), ())), preferred_element_type=jnp.float32)


def _dot(a, b):
    return jnp.dot(a, b, preferred_element_type=jnp.float32)


def _proj_kernel(x_ref, wt_ref, wka_ref, wkb_ref, qa_ref, va_ref, qb_ref, vb_ref, kat_ref, ka_ref, kb_ref):
    xb = x_ref[...].astype(jnp.bfloat16)
    da_scale = DA_HEAD_DIM ** -0.5 * LOG2E
    wg_scale = WG_HEAD_DIM ** -0.5 * LOG2E
    r0, r1, r2, r3 = DA_WIDTH, 2 * DA_WIDTH, 2 * DA_WIDTH + WG_Q, 2 * DA_WIDTH + WG_Q + WG_KV
    qa_ref[...] = (_nt_dot(wt_ref[0:r0, :], xb) * da_scale).astype(jnp.bfloat16)
    va_ref[...] = _nt_dot(wt_ref[r0:r1, :], xb).astype(jnp.bfloat16)
    qb_ref[...] = (_nt_dot(wt_ref[r1:r2, :], xb) * wg_scale).astype(jnp.bfloat16)
    kat_ref[...] = _nt_dot(wt_ref[r3:, :], xb).astype(jnp.bfloat16)
    vb = _nt_dot(wt_ref[r2:r3, :], xb).astype(jnp.bfloat16)
    for t in range(ROW_TILE // LANES):
        vb_ref[t] = vb[:, t * LANES:(t + 1) * LANES]
    kb_ref[...] = _dot(xb, wkb_ref[...]).astype(jnp.bfloat16)
    pos = pl.program_id(1) * ROW_TILE + lax.broadcasted_iota(jnp.int32, (ROW_TILE, DA_KCOLS), 0)
    lane = lax.broadcasted_iota(jnp.int32, (ROW_TILE, DA_KCOLS), 1)
    pos_hi = ((pos // LANES) * LANES).astype(jnp.float32)
    pos_lo = (pos % LANES).astype(jnp.float32)
    c0 = DA_HEAD_DIM
    pat = jnp.where(lane < c0, 0.0,
                    jnp.where(lane < c0 + 3, 1.0,
                              jnp.where(lane < c0 + 6, pos_hi,
                                        jnp.where(lane < c0 + 9, pos_lo, 0.0))))
    for blk in range(2 * DA_HEADS):
        cols = slice(blk * DA_KCOLS, (blk + 1) * DA_KCOLS)
        ka_ref[:, cols] = (_dot(xb, wka_ref[:, cols]) + pat).astype(jnp.bfloat16)


def _project(x, w_t, w_ka, w_kb):
    B, S, D = x.shape
    ns = S // ROW_TILE
    rows_t = w_t.shape[0]
    ka_cols = w_ka.shape[1]
    bf = jnp.bfloat16
    return pl.pallas_call(
        _proj_kernel,
        grid=(B, ns),
        in_specs=[
            pl.BlockSpec((None, ROW_TILE, D), lambda b, s: (b, s, 0)),
            pl.BlockSpec((rows_t, D), lambda b, s: (0, 0)),
            pl.BlockSpec((D, ka_cols), lambda b, s: (0, 0)),
            pl.BlockSpec((D, WG_KV), lambda b, s: (0, 0)),
        ],
        out_specs=[
            pl.BlockSpec((None, DA_WIDTH, ROW_TILE), lambda b, s: (b, 0, s)),
            pl.BlockSpec((None, None, DA_WIDTH, ROW_TILE), lambda b, s: (b, s, 0, 0)),
            pl.BlockSpec((None, WG_Q, ROW_TILE), lambda b, s: (b, 0, s)),
            pl.BlockSpec((None, ROW_TILE // LANES, WG_KV, LANES), lambda b, s: (b, s, 0, 0)),
            pl.BlockSpec((None, DA_WIDTH, ROW_TILE), lambda b, s: (b, 0, s)),
            pl.BlockSpec((None, ROW_TILE, ka_cols), lambda b, s: (b, s, 0)),
            pl.BlockSpec((None, ROW_TILE, WG_KV), lambda b, s: (b, s, 0)),
        ],
        out_shape=[
            jax.ShapeDtypeStruct((B, DA_WIDTH, S), bf),
            jax.ShapeDtypeStruct((B, ns, DA_WIDTH, ROW_TILE), bf),
            jax.ShapeDtypeStruct((B, WG_Q, S), bf),
            jax.ShapeDtypeStruct((B, S // LANES, WG_KV, LANES), bf),
            jax.ShapeDtypeStruct((B, DA_WIDTH, S), bf),
            jax.ShapeDtypeStruct((B, S, ka_cols), bf),
            jax.ShapeDtypeStruct((B, S, WG_KV), bf),
        ],
        compiler_params=pltpu.CompilerParams(
            dimension_semantics=("arbitrary", "arbitrary"), vmem_limit_bytes=VMEM_LIMIT_BYTES),
        name="in_proj",
    )(x, w_t, w_ka, w_kb)


def _split3(x):
    hi = x.astype(jnp.bfloat16).astype(jnp.float32)
    r1 = x - hi
    mid = r1.astype(jnp.bfloat16).astype(jnp.float32)
    return hi, mid, r1 - mid


def _diff_attn_kernel(slope_ref, islope_ref, q_ref, kt_ref, k_ref, v_ref, lq1_ref, lk1_ref, lq2_ref, lk2_ref, g_ref,
                      o_ref, kmax_ref, qaug_ref, sa_ref, sb_ref, ea_ref, eb_ref, acc_ref, *, n_k, lam_init):
    tq, tk, dh = DA_TQ, DA_TK, DA_HEAD_DIM
    h = pl.program_id(1)
    qi = pl.program_id(2)
    sig = slope_ref[h]

    def k_tile(kt):
        return k_ref[pl.ds(pl.multiple_of(kt * tk, tk), tk), :]

    def col_sum8(e):
        return jnp.sum(e.reshape(tk // SUBLANES, SUBLANES, e.shape[1]), axis=0)

    def both_maps(m0, m1):
        z = jnp.zeros_like(m0)
        return jnp.concatenate([jnp.concatenate([m0, z], axis=1), jnp.concatenate([z, m1], axis=1)], axis=0)

    @pl.when(qi == 0)
    def _():
        def kmax_body(kt, m):
            a = jnp.abs(k_tile(kt).astype(jnp.float32)).reshape(tk // SUBLANES, SUBLANES, 2 * DA_KCOLS)
            return jnp.maximum(m, jnp.max(a, axis=0))
        m8 = lax.fori_loop(0, n_k, kmax_body, jnp.zeros((SUBLANES, 2 * DA_KCOLS), jnp.float32))
        kmax_ref[...] = jnp.broadcast_to(jnp.max(m8, axis=0, keepdims=True), kmax_ref.shape)

    q = q_ref[...]
    qc = (q[:dh], q[dh:])
    zpad = jnp.zeros((DA_KCOLS - dh, tq), q.dtype)
    qpad = both_maps(jnp.concatenate([qc[0], zpad], axis=0),
                     jnp.concatenate([qc[1], zpad], axis=0))

    q_start = qi * tq
    kt_ov = q_start // tk

    qk_self = q.astype(jnp.float32) * kt_ref[...].astype(jnp.float32)
    r = jnp.concatenate([jnp.sum(qk_self[:dh], axis=0, keepdims=True),
                         jnp.sum(qk_self[dh:], axis=0, keepdims=True)], axis=1)

    bound = _dot(kmax_ref[...].astype(jnp.bfloat16), jnp.abs(qpad))[0:1] * 1.01
    gap = jnp.max(bound - r, axis=1, keepdims=True)
    reach = jnp.minimum((gap - SKIP_LOG2) * islope_ref[h], float(2 * n_k * tk))
    qs_f = q_start.astype(jnp.float32)
    lo_f = jnp.floor((qs_f - reach - 1.0) * (1.0 / tk))
    hi_f = jnp.floor((qs_f + float(tq) + reach) * (1.0 / tk))
    lo_t = jnp.minimum(jnp.clip(lo_f, 0.0, float(n_k)).astype(jnp.int32)[0, 0], kt_ov)
    hi_t = jnp.maximum(jnp.clip(hi_f, -1.0, float(n_k - 1)).astype(jnp.int32)[0, 0], kt_ov)
    n_left = kt_ov - lo_t
    n_tot = n_left + hi_t - kt_ov

    ipos = (q_start + lax.broadcasted_iota(jnp.int32, (1, tq), 1)).astype(jnp.float32)
    row = lax.broadcasted_iota(jnp.int32, (AUG_ROWS, tq), 0)
    sig_row = jnp.full((1, tq), sig, jnp.float32)
    zrest = jnp.zeros((DA_KCOLS - dh - AUG_ROWS, tq), q.dtype)
    for side, sgn in enumerate((1.0, -1.0)):
        s3 = _split3(sgn * sig_row)
        maps = []
        for c in range(2):
            r3 = _split3(-(r[:, c * tq:(c + 1) * tq] + sgn * sig * ipos))
            slab = jnp.zeros((AUG_ROWS, tq), jnp.float32)
            for i, piece in enumerate(r3 + s3 + s3):
                slab = jnp.where(row == i, piece, slab)
            maps.append(jnp.concatenate([qc[c], slab.astype(q.dtype), zrest], axis=0))
        qaug_ref[side] = both_maps(*maps)

    def tile_of(i):
        i = jnp.clip(i, 0, jnp.maximum(n_tot - 1, 0))
        kt = jnp.where(i < n_left, lo_t + i, kt_ov + 1 + i - n_left)
        return jnp.minimum(kt, n_k - 1), (i >= n_left).astype(jnp.int32)

    def scores(i):
        kt, side = tile_of(i)
        return _dot(k_tile(kt), qaug_ref[side])

    def exp_to(s_ref, e_ref):
        e = jnp.exp2(s_ref[...])
        e_ref[...] = e.astype(e_ref.dtype)
        return col_sum8(e)

    def add_av(i, e_ref):
        kt, _ = tile_of(i)
        acc_ref[...] += _dot(v_ref[kt], e_ref[...])

    k_ov = k_tile(kt_ov)
    s_ov = jnp.minimum(_dot(k_ov, qaug_ref[0]), _dot(k_ov, qaug_ref[1]))
    sa_ref[...] = scores(0)
    sb_ref[...] = scores(1)
    e = jnp.exp2(s_ov)
    l8 = col_sum8(e)
    acc_ref[...] = _dot(v_ref[kt_ov], e.astype(jnp.bfloat16))
    pend = exp_to(sa_ref, ea_ref)

    def pair(p, carry):
        l8, pend = carry
        i = 2 * p
        sa_ref[...] = scores(i + 2)
        add_av(i, ea_ref)
        pend_b = exp_to(sb_ref, eb_ref)
        sb_ref[...] = scores(i + 3)
        add_av(i + 1, eb_ref)
        return l8 + pend + pend_b, exp_to(sa_ref, ea_ref)

    l8, pend = lax.fori_loop(0, n_tot // 2, pair, (l8, pend))

    def last_tile():
        add_av(n_tot - 1, ea_ref)
        return l8 + pend

    l8 = lax.cond(n_tot % 2 == 1, last_tile, lambda: l8)
    l = jnp.sum(l8, axis=0, keepdims=True)

    chk = jnp.sum(acc_ref[...] * 0.0) + jnp.sum(l * 0.0)
    overflowed = jnp.logical_not(chk == 0.0)

    def exact_path():
        dd = (lax.broadcasted_iota(jnp.int32, (tk, tq), 1)
              - lax.broadcasted_iota(jnp.int32, (tk, tq), 0)).astype(jnp.float32)
        acc_ref[...] = jnp.zeros_like(acc_ref)

        def body(kt, carry):
            m, lc = carry
            b = -sig * jnp.abs(dd + (q_start - kt * tk).astype(jnp.float32))
            u = _dot(k_tile(kt), qpad) + jnp.concatenate([b, b], axis=1)
            m_new = jnp.maximum(m, jnp.max(u, axis=0, keepdims=True))
            e = jnp.exp2(u - m_new)
            alpha = jnp.exp2(m - m_new)
            acc_ref[...] = alpha * acc_ref[...] + _dot(v_ref[kt], e.astype(jnp.bfloat16))
            return m_new, alpha * lc + jnp.sum(e, axis=0, keepdims=True)

        init = (jnp.full((1, 2 * tq), NEG_BIG, jnp.float32), jnp.zeros((1, 2 * tq), jnp.float32))
        return lax.fori_loop(0, n_k, body, init)[1]

    l = lax.cond(overflowed, exact_path, lambda: l)

    lam = (jnp.exp(jnp.sum(lq1_ref[...] * lk1_ref[...], axis=1, keepdims=True))
           - jnp.exp(jnp.sum(lq2_ref[...] * lk2_ref[...], axis=1, keepdims=True)) + lam_init)
    on = acc_ref[...] / l
    o = on[:, :tq] - lam * on[:, tq:]
    o = o * lax.rsqrt(jnp.mean(o * o, axis=0, keepdims=True) + LN_EPS)
    o = o * g_ref[...] * (1.0 - lam_init)
    o_ref[...] = o.T.astype(o_ref.dtype)


def _diff_attention(q_t, k_t, k, v_t, lam_q1, lam_k1, lam_q2, lam_k2, subln_g, lam_init):
    B, _, S = q_t.shape
    hw = 2 * DA_HEAD_DIM
    n_k = S // DA_TK
    slopes2 = _alibi_slopes(DA_HEADS) * LOG2E
    vec = lambda a: a.reshape(1, DA_HEAD_DIM)
    small = pl.BlockSpec((1, DA_HEAD_DIM), lambda b, h, i, s1, s2: (0, 0))
    grid_spec = pltpu.PrefetchScalarGridSpec(
        num_scalar_prefetch=2,
        grid=(B, DA_HEADS, S // DA_TQ),
        in_specs=[
            pl.BlockSpec((None, hw, DA_TQ), lambda b, h, i, s1, s2: (b, h, i)),
            pl.BlockSpec((None, hw, DA_TQ), lambda b, h, i, s1, s2: (b, h, i)),
            pl.BlockSpec((None, S, 2 * DA_KCOLS), lambda b, h, i, s1, s2: (b, 0, h)),
            pl.BlockSpec((None, n_k, hw, DA_TK), lambda b, h, i, s1, s2: (b, 0, h, 0)),
            small, small, small, small,
            pl.BlockSpec((hw, 1), lambda b, h, i, s1, s2: (0, 0)),
        ],
        out_specs=pl.BlockSpec((None, DA_TQ, hw), lambda b, h, i, s1, s2: (b, i, h)),
        scratch_shapes=[
            pltpu.VMEM((SUBLANES, 2 * DA_KCOLS), jnp.float32),
            pltpu.VMEM((2, 2 * DA_KCOLS, 2 * DA_TQ), jnp.bfloat16),
            pltpu.VMEM((DA_TK, 2 * DA_TQ), jnp.float32),
            pltpu.VMEM((DA_TK, 2 * DA_TQ), jnp.float32),
            pltpu.VMEM((DA_TK, 2 * DA_TQ), jnp.bfloat16),
            pltpu.VMEM((DA_TK, 2 * DA_TQ), jnp.bfloat16),
            pltpu.VMEM((hw, 2 * DA_TQ), jnp.float32),
        ],
    )
    return pl.pallas_call(
        functools.partial(_diff_attn_kernel, n_k=n_k, lam_init=lam_init),
        grid_spec=grid_spec,
        out_shape=jax.ShapeDtypeStruct((B, S, DA_WIDTH), jnp.bfloat16),
        compiler_params=pltpu.CompilerParams(
            dimension_semantics=("arbitrary", "arbitrary", "arbitrary"),
            vmem_limit_bytes=VMEM_LIMIT_BYTES),
        name="diff_attn",
    )(jnp.asarray(slopes2, jnp.float32), jnp.asarray(1.0 / slopes2, jnp.float32),
      q_t, k_t, k, v_t, vec(lam_q1), vec(lam_k1), vec(lam_q2), vec(lam_k2), subln_g.reshape(hw, 1))


def _win_attn_kernel(slope_ref, sink_ref, q_ref, k_ref, v_ref, o_ref, ot_ref, *, seq_len):
    tq, band = WG_TQ, WG_BAND
    n_vt = band // LANES
    rep = WG_HEADS // WG_KV_HEADS
    qi = pl.program_id(1)
    q_start = qi * tq
    tile0 = jnp.clip(q_start // LANES - WINDOW // LANES, 0, seq_len // LANES - n_vt)
    k_start = pl.multiple_of(tile0 * LANES, LANES)

    kband = k_ref[pl.ds(k_start, band), :]
    kpos = k_start + lax.broadcasted_iota(jnp.int32, (band, tq), 0)
    qpos = q_start + lax.broadcasted_iota(jnp.int32, (band, tq), 1)
    dist_i = jnp.abs(qpos - kpos)
    valid = dist_i <= WINDOW
    dist = dist_i.astype(jnp.float32)

    for g in range(WG_KV_HEADS):
        for r in range(rep):
            hd = g * rep + r
            qh = q_ref[hd * WG_HEAD_DIM:(hd + 1) * WG_HEAD_DIM, :]
            parts = []
            if g > 0:
                parts.append(jnp.zeros((g * WG_HEAD_DIM, tq), qh.dtype))
            parts.append(qh)
            if g < WG_KV_HEADS - 1:
                parts.append(jnp.zeros(((WG_KV_HEADS - 1 - g) * WG_HEAD_DIM, tq), qh.dtype))
            qpad = jnp.concatenate(parts, axis=0)
            s = _dot(kband, qpad) - slope_ref[hd] * dist
            s = jnp.where(valid, s, NEG_BIG)
            sk = sink_ref[hd] * LOG2E
            m = jnp.maximum(jnp.max(s, axis=0, keepdims=True), sk)
            e = jnp.exp2(s - m)
            denom = jnp.sum(e, axis=0, keepdims=True) + jnp.exp2(sk - m)
            eb = e.astype(jnp.bfloat16)
            acc = jnp.zeros((WG_HEAD_DIM, tq), jnp.float32)
            for t in range(n_vt):
                vt = v_ref[tile0 + t, g * WG_HEAD_DIM:(g + 1) * WG_HEAD_DIM, :]
                acc = acc + _dot(vt, eb[t * LANES:(t + 1) * LANES, :])
            ot_ref[hd * WG_HEAD_DIM:(hd + 1) * WG_HEAD_DIM, :] = acc / denom
    o_ref[...] = ot_ref[...].T.astype(o_ref.dtype)


def _window_attention(q_t, k, v_t, sink_logit):
    B, _, S = q_t.shape
    slopes2 = jnp.asarray(_alibi_slopes(WG_HEADS) * LOG2E, jnp.float32)
    grid_spec = pltpu.PrefetchScalarGridSpec(
        num_scalar_prefetch=2,
        grid=(B, S // WG_TQ),
        in_specs=[
            pl.BlockSpec((None, WG_Q, WG_TQ), lambda b, i, s1, s2: (b, 0, i)),
            pl.BlockSpec((None, S, WG_KV), lambda b, i, s1, s2: (b, 0, 0)),
            pl.BlockSpec((None, S // LANES, WG_KV, LANES), lambda b, i, s1, s2: (b, 0, 0, 0)),
        ],
        out_specs=pl.BlockSpec((None, WG_TQ, WG_Q), lambda b, i, s1, s2: (b, i, 0)),
        scratch_shapes=[pltpu.VMEM((WG_Q, WG_TQ), jnp.float32)],
    )
    return pl.pallas_call(
        functools.partial(_win_attn_kernel, seq_len=S),
        grid_spec=grid_spec,
        out_shape=jax.ShapeDtypeStruct((B, S, WG_Q), jnp.bfloat16),
        compiler_params=pltpu.CompilerParams(
            dimension_semantics=("arbitrary", "arbitrary"), vmem_limit_bytes=VMEM_LIMIT_BYTES),
        name="win_attn",
    )(slopes2, sink_logit.astype(jnp.float32), q_t, k, v_t)


def _layer_norm(x, g, b):
    mu = jnp.mean(x, axis=-1, keepdims=True)
    xc = x - mu
    var = jnp.mean(xc * xc, axis=-1, keepdims=True)
    return xc * lax.rsqrt(var + LN_EPS) * g + b


def _merge_kernel(x_ref, oa_ref, ob_ref, wg_ref, bg_ref, wa_ref, wb_ref, wo_ref, g_ref, b_ref, y_ref):
    x = x_ref[...]
    gates = jax.nn.sigmoid(_dot(x.astype(jnp.bfloat16), wg_ref[...]) + bg_ref[...])
    merged = (gates[:, :D_MODEL] * _dot(oa_ref[...], wa_ref[...])
              + gates[:, D_MODEL:] * _dot(ob_ref[...], wb_ref[...]))
    mix = _dot(merged.astype(jnp.bfloat16), wo_ref[...])
    y_ref[...] = _layer_norm(DEEPNORM_ALPHA * x + mix, g_ref[...], b_ref[...])


def _merge(x, o_a, o_b, w_gate, b_gate, w_br_a, w_br_b, w_out, ln_g, ln_b):
    B, S, D = x.shape
    tok = lambda b, s: (b, s, 0)
    const = lambda b, s: (0, 0)
    return pl.pallas_call(
        _merge_kernel,
        grid=(B, S // ROW_TILE),
        in_specs=[
            pl.BlockSpec((None, ROW_TILE, D), tok),
            pl.BlockSpec((None, ROW_TILE, DA_WIDTH), tok),
            pl.BlockSpec((None, ROW_TILE, WG_Q), tok),
            pl.BlockSpec((D, 2 * D), const),
            pl.BlockSpec((1, 2 * D), const),
            pl.BlockSpec((DA_WIDTH, D), const),
            pl.BlockSpec((WG_Q, D), const),
            pl.BlockSpec((D, D), const),
            pl.BlockSpec((1, D), const),
            pl.BlockSpec((1, D), const),
        ],
        out_specs=pl.BlockSpec((None, ROW_TILE, D), tok),
        out_shape=jax.ShapeDtypeStruct((B, S, D), jnp.float32),
        compiler_params=pltpu.CompilerParams(
            dimension_semantics=("arbitrary", "arbitrary"), vmem_limit_bytes=VMEM_LIMIT_BYTES),
        name="merge_ln1",
    )(x, o_a, o_b, w_gate, b_gate, w_br_a, w_br_b, w_out, ln_g, ln_b)


def _ffn_kernel(x_ref, w1_ref, b1_ref, w2_ref, b2_ref, g_ref, b_ref, y_ref):
    x = x_ref[...]
    h = jnp.maximum(_dot(x.astype(jnp.bfloat16), w1_ref[...]) + b1_ref[...], 0.0)
    f = _dot((h * h).astype(jnp.bfloat16), w2_ref[...]) + b2_ref[...]
    y_ref[...] = _layer_norm(DEEPNORM_ALPHA * x + f, g_ref[...], b_ref[...])


def _ffn(x, w1, b1, w2, b2, ln_g, ln_b):
    B, S, D = x.shape
    tok = lambda b, s: (b, s, 0)
    const = lambda b, s: (0, 0)
    return pl.pallas_call(
        _ffn_kernel,
        grid=(B, S // ROW_TILE),
        in_specs=[
            pl.BlockSpec((None, ROW_TILE, D), tok),
            pl.BlockSpec((D, D_FF), const),
            pl.BlockSpec((1, D_FF), const),
            pl.BlockSpec((D_FF, D), const),
            pl.BlockSpec((1, D), const),
            pl.BlockSpec((1, D), const),
            pl.BlockSpec((1, D), const),
        ],
        out_specs=pl.BlockSpec((None, ROW_TILE, D), tok),
        out_shape=jax.ShapeDtypeStruct((B, S, D), jnp.float32),
        compiler_params=pltpu.CompilerParams(
            dimension_semantics=("arbitrary", "arbitrary"), vmem_limit_bytes=VMEM_LIMIT_BYTES),
        name="ffn_ln2",
    )(x, w1, b1, w2, b2, ln_g, ln_b)


def _encoder_layer(x, l, w_in, b_gate, lam_q1, lam_k1, lam_q2, lam_k2, subln_g, sink_logit,
                   w_br_a, w_br_b, w_out, ln1_g, ln1_b, w_ff1, b_ff1, w_ff2, b_ff2, ln2_g, ln2_b):
    bf = jnp.bfloat16
    row = lambda a: a.reshape(1, -1)
    w_t = jnp.concatenate([w_in[:, OFF_DA_Q:OFF_DA_K], w_in[:, OFF_DA_V:OFF_WG_Q],
                           w_in[:, OFF_WG_Q:OFF_WG_K], w_in[:, OFF_WG_V:OFF_GATE],
                           w_in[:, OFF_DA_K:OFF_DA_V]], axis=1).T.astype(bf)
    w_ka = w_in[:, OFF_DA_K:OFF_DA_V].reshape(D_MODEL, 2 * DA_HEADS, DA_HEAD_DIM)
    w_ka = jnp.pad(w_ka, ((0, 0), (0, 0), (0, DA_KCOLS - DA_HEAD_DIM))).reshape(D_MODEL, -1).astype(bf)
    w_kb = w_in[:, OFF_WG_K:OFF_WG_V].astype(bf)
    w_gate = w_in[:, OFF_GATE:].astype(bf)
    lam_init = 0.8 - 0.6 * math.exp(-0.3 * l)

    qa_t, va_t, qb_t, vb_t, ka_t, k_a, k_b = _project(x, w_t, w_ka, w_kb)
    o_a = _diff_attention(qa_t, ka_t, k_a, va_t, lam_q1, lam_k1, lam_q2, lam_k2, subln_g, lam_init)
    o_b = _window_attention(qb_t, k_b, vb_t, sink_logit)
    x1 = _merge(x, o_a, o_b, w_gate, row(b_gate), w_br_a.astype(bf), w_br_b.astype(bf),
                w_out.astype(bf), row(ln1_g), row(ln1_b))
    return _ffn(x1, w_ff1.astype(bf), row(b_ff1), w_ff2.astype(bf), row(b_ff2), row(ln2_g), row(ln2_b))


def kernel(x_prompt, x_sample, w_in, b_gate, lam_q1, lam_k1, lam_q2, lam_k2, subln_g, sink_logit,
           w_br_a, w_br_b, w_out, ln1_g, ln1_b, w_ff1, b_ff1, w_ff2, b_ff2, ln2_g, ln2_b):
    def run(x):
        for l in range(DEPTH):
            x = _encoder_layer(x, l, w_in[l], b_gate[l], lam_q1[l], lam_k1[l], lam_q2[l], lam_k2[l],
                               subln_g[l], sink_logit[l], w_br_a[l], w_br_b[l], w_out[l],
                               ln1_g[l], ln1_b[l], w_ff1[l], b_ff1[l], w_ff2[l], b_ff2[l],
                               ln2_g[l], ln2_b[l])
        return x

    return (run(x_prompt), run(x_sample))
```

```python
import functools
import math

import jax
import jax.numpy as jnp
import numpy as np
from jax import lax
from jax.experimental import pallas as pl
from jax.experimental.pallas import tpu as pltpu

D_MODEL = 1024
DA_HEADS = 8
DA_HEAD_DIM = 64
DA_WIDTH = DA_HEADS * 2 * DA_HEAD_DIM
WG_HEADS = 16
WG_KV_HEADS = 4
WG_HEAD_DIM = 64
WG_Q = WG_HEADS * WG_HEAD_DIM
WG_KV = WG_KV_HEADS * WG_HEAD_DIM
WINDOW = 128
D_FF = 4 * D_MODEL
DEPTH = 1
DEEPNORM_ALPHA = (2.0 * DEPTH) ** 0.25
LN_EPS = 1e-5
LOG2E = math.log2(math.e)
NEG_BIG = -1e30

OFF_DA_Q = 0
OFF_DA_K = OFF_DA_Q + DA_WIDTH
OFF_DA_V = OFF_DA_K + DA_WIDTH
OFF_WG_Q = OFF_DA_V + DA_WIDTH
OFF_WG_K = OFF_WG_Q + WG_Q
OFF_WG_V = OFF_WG_K + WG_KV
OFF_GATE = OFF_WG_V + WG_KV

LANES = 128
SUBLANES = 8
BF16_ROWS = 16
VMEM_LIMIT_BYTES = 56 * 1024 * 1024

ROW_TILE = 512
DA_TQ = 512
DA_TK = ROW_TILE
WG_TQ = 256
WG_BAND = WG_TQ + 2 * WINDOW

DA_KCOLS = 2 * DA_HEAD_DIM
AUG_ROWS = BF16_ROWS
SKIP_LOG2 = -150.0


def _alibi_slopes(n_heads):
    return 2.0 ** (-8.0 * np.arange(1, n_heads + 1) / n_heads)


def _nt_dot(a, b):
    return lax.dot_general(a, b, (((1,), (1,)), ((), ())), preferred_element_type=jnp.float32)


def _dot(a, b):
    return jnp.dot(a, b, preferred_element_type=jnp.float32)


def _proj_kernel(x_ref, wt_ref, wka_ref, wkb_ref, qa_ref, va_ref, qb_ref, vb_ref, kat_ref, ka_ref, kb_ref):
    xb = x_ref[...].astype(jnp.bfloat16)
    da_scale = DA_HEAD_DIM ** -0.5 * LOG2E
    wg_scale = WG_HEAD_DIM ** -0.5 * LOG2E
    r0, r1, r2, r3 = DA_WIDTH, 2 * DA_WIDTH, 2 * DA_WIDTH + WG_Q, 2 * DA_WIDTH + WG_Q + WG_KV
    qa_ref[...] = (_nt_dot(wt_ref[0:r0, :], xb) * da_scale).astype(jnp.bfloat16)
    va_ref[...] = _nt_dot(wt_ref[r0:r1, :], xb).astype(jnp.bfloat16)
    qb_ref[...] = (_nt_dot(wt_ref[r1:r2, :], xb) * wg_scale).astype(jnp.bfloat16)
    kat_ref[...] = _nt_dot(wt_ref[r3:, :], xb).astype(jnp.bfloat16)
    vb = _nt_dot(wt_ref[r2:r3, :], xb).astype(jnp.bfloat16)
    for t in range(ROW_TILE // LANES):
        vb_ref[t] = vb[:, t * LANES:(t + 1) * LANES]
    kb_ref[...] = _dot(xb, wkb_ref[...]).astype(jnp.bfloat16)
    pos = pl.program_id(1) * ROW_TILE + lax.broadcasted_iota(jnp.int32, (ROW_TILE, DA_KCOLS), 0)
    lane = lax.broadcasted_iota(jnp.int32, (ROW_TILE, DA_KCOLS), 1)
    pos_hi = ((pos // LANES) * LANES).astype(jnp.float32)
    pos_lo = (pos % LANES).astype(jnp.float32)
    c0 = DA_HEAD_DIM
    pat = jnp.where(lane < c0, 0.0,
                    jnp.where(lane < c0 + 3, 1.0,
                              jnp.where(lane < c0 + 6, pos_hi,
                                        jnp.where(lane < c0 + 9, pos_lo, 0.0))))
    for blk in range(2 * DA_HEADS):
        cols = slice(blk * DA_KCOLS, (blk + 1) * DA_KCOLS)
        ka_ref[:, cols] = (_dot(xb, wka_ref[:, cols]) + pat).astype(jnp.bfloat16)


def _project(x, w_t, w_ka, w_kb):
    B, S, D = x.shape
    ns = S // ROW_TILE
    rows_t = w_t.shape[0]
    ka_cols = w_ka.shape[1]
    bf = jnp.bfloat16
    return pl.pallas_call(
        _proj_kernel,
        grid=(B, ns),
        in_specs=[
            pl.BlockSpec((None, ROW_TILE, D), lambda b, s: (b, s, 0)),
            pl.BlockSpec((rows_t, D), lambda b, s: (0, 0)),
            pl.BlockSpec((D, ka_cols), lambda b, s: (0, 0)),
            pl.BlockSpec((D, WG_KV), lambda b, s: (0, 0)),
        ],
        out_specs=[
            pl.BlockSpec((None, DA_WIDTH, ROW_TILE), lambda b, s: (b, 0, s)),
            pl.BlockSpec((None, None, DA_WIDTH, ROW_TILE), lambda b, s: (b, s, 0, 0)),
            pl.BlockSpec((None, WG_Q, ROW_TILE), lambda b, s: (b, 0, s)),
            pl.BlockSpec((None, ROW_TILE // LANES, WG_KV, LANES), lambda b, s: (b, s, 0, 0)),
            pl.BlockSpec((None, DA_WIDTH, ROW_TILE), lambda b, s: (b, 0, s)),
            pl.BlockSpec((None, ROW_TILE, ka_cols), lambda b, s: (b, s, 0)),
            pl.BlockSpec((None, ROW_TILE, WG_KV), lambda b, s: (b, s, 0)),
        ],
        out_shape=[
            jax.ShapeDtypeStruct((B, DA_WIDTH, S), bf),
            jax.ShapeDtypeStruct((B, ns, DA_WIDTH, ROW_TILE), bf),
            jax.ShapeDtypeStruct((B, WG_Q, S), bf),
            jax.ShapeDtypeStruct((B, S // LANES, WG_KV, LANES), bf),
            jax.ShapeDtypeStruct((B, DA_WIDTH, S), bf),
            jax.ShapeDtypeStruct((B, S, ka_cols), bf),
            jax.ShapeDtypeStruct((B, S, WG_KV), bf),
        ],
        compiler_params=pltpu.CompilerParams(
            dimension_semantics=("arbitrary", "arbitrary"), vmem_limit_bytes=VMEM_LIMIT_BYTES),
        name="in_proj",
    )(x, w_t, w_ka, w_kb)


def _split3(x):
    hi = x.astype(jnp.bfloat16).astype(jnp.float32)
    r1 = x - hi
    mid = r1.astype(jnp.bfloat16).astype(jnp.float32)
    return hi, mid, r1 - mid


def _diff_attn_kernel(slope_ref, islope_ref, q_ref, kt_ref, k_ref, v_ref, lq1_ref, lk1_ref, lq2_ref, lk2_ref, g_ref,
                      o_ref, bias_ref, kmax_ref, qaug_ref, sa_ref, sb_ref, ea_ref, eb_ref, acc_ref,
                      *, n_k, lam_init):
    tq, tk, dh = DA_TQ, DA_TK, DA_HEAD_DIM
    h = pl.program_id(1)
    qi = pl.program_id(2)
    sig = slope_ref[h]

    def k_tile(kt):
        return k_ref[pl.ds(pl.multiple_of(kt * tk, tk), tk), :]

    def col_sum8(e):
        return jnp.sum(e.reshape(tk // SUBLANES, SUBLANES, e.shape[1]), axis=0)

    def both_maps(m0, m1):
        z = jnp.zeros_like(m0)
        return jnp.concatenate([jnp.concatenate([m0, z], axis=1), jnp.concatenate([z, m1], axis=1)], axis=0)

    @pl.when(qi == 0)
    def _():
        dd = (lax.broadcasted_iota(jnp.int32, (tk, tq), 1) - lax.broadcasted_iota(jnp.int32, (tk, tq), 0))
        b = -sig * jnp.abs(dd).astype(jnp.float32)
        bias_ref[...] = jnp.concatenate([b, b], axis=1)

        def kmax_body(kt, m):
            a = jnp.abs(k_tile(kt).astype(jnp.float32)).reshape(tk // SUBLANES, SUBLANES, 2 * DA_KCOLS)
            return jnp.maximum(m, jnp.max(a, axis=0))
        m8 = lax.fori_loop(0, n_k, kmax_body, jnp.zeros((SUBLANES, 2 * DA_KCOLS), jnp.float32))
        kmax_ref[...] = jnp.broadcast_to(jnp.max(m8, axis=0, keepdims=True), kmax_ref.shape)

    q = q_ref[...]
    qc = (q[:dh], q[dh:])
    zpad = jnp.zeros((DA_KCOLS - dh, tq), q.dtype)
    qpad = both_maps(jnp.concatenate([qc[0], zpad], axis=0),
                     jnp.concatenate([qc[1], zpad], axis=0))

    q_start = qi * tq
    kt_ov = q_start // tk

    qk_self = q.astype(jnp.float32) * kt_ref[...].astype(jnp.float32)
    r = jnp.concatenate([jnp.sum(qk_self[:dh], axis=0, keepdims=True),
                         jnp.sum(qk_self[dh:], axis=0, keepdims=True)], axis=1)

    bound = _dot(kmax_ref[...].astype(jnp.bfloat16), jnp.abs(qpad))[0:1] * 1.01
    gap = jnp.max(bound - r, axis=1, keepdims=True)
    reach = jnp.minimum((gap - SKIP_LOG2) * islope_ref[h], float(2 * n_k * tk))
    qs_f = q_start.astype(jnp.float32)
    lo_f = jnp.floor((qs_f - reach - 1.0) * (1.0 / tk))
    hi_f = jnp.floor((qs_f + float(tq) + reach) * (1.0 / tk))
    lo_t = jnp.minimum(jnp.clip(lo_f, 0.0, float(n_k)).astype(jnp.int32)[0, 0], kt_ov)
    hi_t = jnp.maximum(jnp.clip(hi_f, -1.0, float(n_k - 1)).astype(jnp.int32)[0, 0], kt_ov)
    n_left = kt_ov - lo_t
    n_tot = n_left + hi_t - kt_ov

    ipos = (q_start + lax.broadcasted_iota(jnp.int32, (1, tq), 1)).astype(jnp.float32)
    row = lax.broadcasted_iota(jnp.int32, (AUG_ROWS, tq), 0)
    sig_row = jnp.full((1, tq), sig, jnp.float32)
    zrest = jnp.zeros((DA_KCOLS - dh - AUG_ROWS, tq), q.dtype)
    for side, sgn in enumerate((1.0, -1.0, 0.0)):
        s3 = _split3(sgn * sig_row)
        maps = []
        for c in range(2):
            r3 = _split3(-(r[:, c * tq:(c + 1) * tq] + sgn * sig * ipos))
            slab = jnp.zeros((AUG_ROWS, tq), jnp.float32)
            for i, piece in enumerate(r3 + s3 + s3):
                slab = jnp.where(row == i, piece, slab)
            maps.append(jnp.concatenate([qc[c], slab.astype(q.dtype), zrest], axis=0))
        qaug_ref[side] = both_maps(*maps)

    def tile_of(i):
        i = jnp.clip(i, 0, jnp.maximum(n_tot - 1, 0))
        kt = jnp.where(i < n_left, lo_t + i, kt_ov + 1 + i - n_left)
        return jnp.minimum(kt, n_k - 1), (i >= n_left).astype(jnp.int32)

    def scores(i):
        kt, side = tile_of(i)
        return _dot(k_tile(kt), qaug_ref[side])

    def exp_to(s_ref, e_ref):
        e = jnp.exp2(s_ref[...])
        e_ref[...] = e.astype(e_ref.dtype)
        return col_sum8(e)

    def add_av(i, e_ref):
        kt, _ = tile_of(i)
        acc_ref[...] += _dot(v_ref[kt], e_ref[...])

    s_ov = _dot(k_tile(kt_ov), qaug_ref[2]) + bias_ref[...]
    sa_ref[...] = scores(0)
    sb_ref[...] = scores(1)
    e = jnp.exp2(s_ov)
    l8 = col_sum8(e)
    acc_ref[...] = _dot(v_ref[kt_ov], e.astype(jnp.bfloat16))
    pend = exp_to(sa_ref, ea_ref)

    def pair(p, carry):
        l8, pend = carry
        i = 2 * p
        sa_ref[...] = scores(i + 2)
        add_av(i, ea_ref)
        pend_b = exp_to(sb_ref, eb_ref)
        sb_ref[...] = scores(i + 3)
        add_av(i + 1, eb_ref)
        return l8 + pend + pend_b, exp_to(sa_ref, ea_ref)

    l8, pend = lax.fori_loop(0, n_tot // 2, pair, (l8, pend))

    def last_tile():
        add_av(n_tot - 1, ea_ref)
        return l8 + pend

    l8 = lax.cond(n_tot % 2 == 1, last_tile, lambda: l8)
    l = jnp.sum(l8, axis=0, keepdims=True)

    chk = jnp.sum(acc_ref[...] * 0.0) + jnp.sum(l * 0.0)
    overflowed = jnp.logical_not(chk == 0.0)

    def exact_path():
        dd = (lax.broadcasted_iota(jnp.int32, (tk, tq), 1)
              - lax.broadcasted_iota(jnp.int32, (tk, tq), 0)).astype(jnp.float32)
        acc_ref[...] = jnp.zeros_like(acc_ref)

        def body(kt, carry):
            m, lc = carry
            b = -sig * jnp.abs(dd + (q_start - kt * tk).astype(jnp.float32))
            u = _dot(k_tile(kt), qpad) + jnp.concatenate([b, b], axis=1)
            m_new = jnp.maximum(m, jnp.max(u, axis=0, keepdims=True))
            e = jnp.exp2(u - m_new)
            alpha = jnp.exp2(m - m_new)
            acc_ref[...] = alpha * acc_ref[...] + _dot(v_ref[kt], e.astype(jnp.bfloat16))
            return m_new, alpha * lc + jnp.sum(e, axis=0, keepdims=True)

        init = (jnp.full((1, 2 * tq), NEG_BIG, jnp.float32), jnp.zeros((1, 2 * tq), jnp.float32))
        return lax.fori_loop(0, n_k, body, init)[1]

    l = lax.cond(overflowed, exact_path, lambda: l)

    lam = (jnp.exp(jnp.sum(lq1_ref[...] * lk1_ref[...], axis=1, keepdims=True))
           - jnp.exp(jnp.sum(lq2_ref[...] * lk2_ref[...], axis=1, keepdims=True)) + lam_init)
    on = acc_ref[...] / l
    o = on[:, :tq] - lam * on[:, tq:]
    o = o * lax.rsqrt(jnp.mean(o * o, axis=0, keepdims=True) + LN_EPS)
    o = o * g_ref[...] * (1.0 - lam_init)
    o_ref[...] = o.T.astype(o_ref.dtype)


def _diff_attention(q_t, k_t, k, v_t, lam_q1, lam_k1, lam_q2, lam_k2, subln_g, lam_init):
    B, _, S = q_t.shape
    assert DA_TQ == DA_TK, "the diagonal-tile bias assumes square tiles"
    hw = 2 * DA_HEAD_DIM
    n_k = S // DA_TK
    slopes2 = _alibi_slopes(DA_HEADS) * LOG2E
    vec = lambda a: a.reshape(1, DA_HEAD_DIM)
    small = pl.BlockSpec((1, DA_HEAD_DIM), lambda b, h, i, s1, s2: (0, 0))
    grid_spec = pltpu.PrefetchScalarGridSpec(
        num_scalar_prefetch=2,
        grid=(B, DA_HEADS, S // DA_TQ),
        in_specs=[
            pl.BlockSpec((None, hw, DA_TQ), lambda b, h, i, s1, s2: (b, h, i)),
            pl.BlockSpec((None, hw, DA_TQ), lambda b, h, i, s1, s2: (b, h, i)),
            pl.BlockSpec((None, S, 2 * DA_KCOLS), lambda b, h, i, s1, s2: (b, 0, h)),
            pl.BlockSpec((None, n_k, hw, DA_TK), lambda b, h, i, s1, s2: (b, 0, h, 0)),
            small, small, small, small,
            pl.BlockSpec((hw, 1), lambda b, h, i, s1, s2: (0, 0)),
        ],
        out_specs=pl.BlockSpec((None, DA_TQ, hw), lambda b, h, i, s1, s2: (b, i, h)),
        scratch_shapes=[
            pltpu.VMEM((DA_TK, 2 * DA_TQ), jnp.float32),
            pltpu.VMEM((SUBLANES, 2 * DA_KCOLS), jnp.float32),
            pltpu.VMEM((3, 2 * DA_KCOLS, 2 * DA_TQ), jnp.bfloat16),
            pltpu.VMEM((DA_TK, 2 * DA_TQ), jnp.float32),
            pltpu.VMEM((DA_TK, 2 * DA_TQ), jnp.float32),
            pltpu.VMEM((DA_TK, 2 * DA_TQ), jnp.bfloat16),
            pltpu.VMEM((DA_TK, 2 * DA_TQ), jnp.bfloat16),
            pltpu.VMEM((hw, 2 * DA_TQ), jnp.float32),
        ],
    )
    return pl.pallas_call(
        functools.partial(_diff_attn_kernel, n_k=n_k, lam_init=lam_init),
        grid_spec=grid_spec,
        out_shape=jax.ShapeDtypeStruct((B, S, DA_WIDTH), jnp.bfloat16),
        compiler_params=pltpu.CompilerParams(
            dimension_semantics=("arbitrary", "arbitrary", "arbitrary"),
            vmem_limit_bytes=VMEM_LIMIT_BYTES),
        name="diff_attn",
    )(jnp.asarray(slopes2, jnp.float32), jnp.asarray(1.0 / slopes2, jnp.float32),
      q_t, k_t, k, v_t, vec(lam_q1), vec(lam_k1), vec(lam_q2), vec(lam_k2), subln_g.reshape(hw, 1))


def _win_attn_kernel(slope_ref, sink_ref, q_ref, k_ref, v_ref, o_ref, ot_ref, *, seq_len):
    tq, band = WG_TQ, WG_BAND
    n_vt = band // LANES
    rep = WG_HEADS // WG_KV_HEADS
    qi = pl.program_id(1)
    q_start = qi * tq
    tile0 = jnp.clip(q_start // LANES - WINDOW // LANES, 0, seq_len // LANES - n_vt)
    k_start = pl.multiple_of(tile0 * LANES, LANES)

    kband = k_ref[pl.ds(k_start, band), :]
    kpos = k_start + lax.broadcasted_iota(jnp.int32, (band, tq), 0)
    qpos = q_start + lax.broadcasted_iota(jnp.int32, (band, tq), 1)
    dist_i = jnp.abs(qpos - kpos)
    dist = jnp.where(dist_i <= WINDOW, dist_i.astype(jnp.float32), -NEG_BIG)

    for g in range(WG_KV_HEADS):
        rows = slice(g * WG_HEAD_DIM, (g + 1) * WG_HEAD_DIM)
        blocks = []
        for r in range(rep):
            hd = g * rep + r
            parts = []
            if g > 0:
                parts.append(jnp.zeros((g * WG_HEAD_DIM, tq), q_ref.dtype))
            parts.append(q_ref[hd * WG_HEAD_DIM:(hd + 1) * WG_HEAD_DIM, :])
            if g < WG_KV_HEADS - 1:
                parts.append(jnp.zeros(((WG_KV_HEADS - 1 - g) * WG_HEAD_DIM, tq), q_ref.dtype))
            blocks.append(jnp.concatenate(parts, axis=0))
        s = _dot(kband, jnp.concatenate(blocks, axis=1))
        es, denoms = [], []
        for r in range(rep):
            hd = g * rep + r
            sr = s[:, r * tq:(r + 1) * tq] - slope_ref[hd] * dist
            sk = sink_ref[hd] * LOG2E
            m = jnp.maximum(jnp.max(sr, axis=0, keepdims=True), sk)
            e = jnp.exp2(sr - m)
            denoms.append(jnp.sum(e, axis=0, keepdims=True) + jnp.exp2(sk - m))
            es.append(e.astype(jnp.bfloat16))
        vg = jnp.concatenate([v_ref[tile0 + t, rows, :] for t in range(n_vt)], axis=1)
        og = _dot(vg, jnp.concatenate(es, axis=1)) / jnp.concatenate(denoms, axis=1)
        for r in range(rep):
            hd = g * rep + r
            ot_ref[hd * WG_HEAD_DIM:(hd + 1) * WG_HEAD_DIM, :] = og[:, r * tq:(r + 1) * tq]
    o_ref[...] = ot_ref[...].T.astype(o_ref.dtype)


def _window_attention(q_t, k, v_t, sink_logit):
    B, _, S = q_t.shape
    slopes2 = jnp.asarray(_alibi_slopes(WG_HEADS) * LOG2E, jnp.float32)
    grid_spec = pltpu.PrefetchScalarGridSpec(
        num_scalar_prefetch=2,
        grid=(B, S // WG_TQ),
        in_specs=[
            pl.BlockSpec((None, WG_Q, WG_TQ), lambda b, i, s1, s2: (b, 0, i)),
            pl.BlockSpec((None, S, WG_KV), lambda b, i, s1, s2: (b, 0, 0)),
            pl.BlockSpec((None, S // LANES, WG_KV, LANES), lambda b, i, s1, s2: (b, 0, 0, 0)),
        ],
        out_specs=pl.BlockSpec((None, WG_TQ, WG_Q), lambda b, i, s1, s2: (b, i, 0)),
        scratch_shapes=[pltpu.VMEM((WG_Q, WG_TQ), jnp.float32)],
    )
    return pl.pallas_call(
        functools.partial(_win_attn_kernel, seq_len=S),
        grid_spec=grid_spec,
        out_shape=jax.ShapeDtypeStruct((B, S, WG_Q), jnp.bfloat16),
        compiler_params=pltpu.CompilerParams(
            dimension_semantics=("arbitrary", "arbitrary"), vmem_limit_bytes=VMEM_LIMIT_BYTES),
        name="win_attn",
    )(slopes2, sink_logit.astype(jnp.float32), q_t, k, v_t)


def _layer_norm(x, g, b):
    mu = jnp.mean(x, axis=-1, keepdims=True)
    xc = x - mu
    var = jnp.mean(xc * xc, axis=-1, keepdims=True)
    return xc * lax.rsqrt(var + LN_EPS) * g + b


def _merge_kernel(x_ref, oa_ref, ob_ref, wg_ref, bg_ref, wa_ref, wb_ref, wo_ref, g_ref, b_ref, y_ref):
    x = x_ref[...]
    gates = jax.nn.sigmoid(_dot(x.astype(jnp.bfloat16), wg_ref[...]) + bg_ref[...])
    merged = (gates[:, :D_MODEL] * _dot(oa_ref[...], wa_ref[...])
              + gates[:, D_MODEL:] * _dot(ob_ref[...], wb_ref[...]))
    mix = _dot(merged.astype(jnp.bfloat16), wo_ref[...])
    y_ref[...] = _layer_norm(DEEPNORM_ALPHA * x + mix, g_ref[...], b_ref[...])


def _merge(x, o_a, o_b, w_gate, b_gate, w_br_a, w_br_b, w_out, ln_g, ln_b):
    B, S, D = x.shape
    tok = lambda b, s: (b, s, 0)
    const = lambda b, s: (0, 0)
    return pl.pallas_call(
        _merge_kernel,
        grid=(B, S // ROW_TILE),
        in_specs=[
            pl.BlockSpec((None, ROW_TILE, D), tok),
            pl.BlockSpec((None, ROW_TILE, DA_WIDTH), tok),
            pl.BlockSpec((None, ROW_TILE, WG_Q), tok),
            pl.BlockSpec((D, 2 * D), const),
            pl.BlockSpec((1, 2 * D), const),
            pl.BlockSpec((DA_WIDTH, D), const),
            pl.BlockSpec((WG_Q, D), const),
            pl.BlockSpec((D, D), const),
            pl.BlockSpec((1, D), const),
            pl.BlockSpec((1, D), const),
        ],
        out_specs=pl.BlockSpec((None, ROW_TILE, D), tok),
        out_shape=jax.ShapeDtypeStruct((B, S, D), jnp.float32),
        compiler_params=pltpu.CompilerParams(
            dimension_semantics=("arbitrary", "arbitrary"), vmem_limit_bytes=VMEM_LIMIT_BYTES),
        name="merge_ln1",
    )(x, o_a, o_b, w_gate, b_gate, w_br_a, w_br_b, w_out, ln_g, ln_b)


def _ffn_kernel(x_ref, w1_ref, b1_ref, w2_ref, b2_ref, g_ref, b_ref, y_ref):
    x = x_ref[...]
    h = jnp.maximum(_dot(x.astype(jnp.bfloat16), w1_ref[...]) + b1_ref[...], 0.0)
    f = _dot((h * h).astype(jnp.bfloat16), w2_ref[...]) + b2_ref[...]
    y_ref[...] = _layer_norm(DEEPNORM_ALPHA * x + f, g_ref[...], b_ref[...])


def _ffn(x, w1, b1, w2, b2, ln_g, ln_b):
    B, S, D = x.shape
    tok = lambda b, s: (b, s, 0)
    const = lambda b, s: (0, 0)
    return pl.pallas_call(
        _ffn_kernel,
        grid=(B, S // ROW_TILE),
        in_specs=[
            pl.BlockSpec((None, ROW_TILE, D), tok),
            pl.BlockSpec((D, D_FF), const),
            pl.BlockSpec((1, D_FF), const),
            pl.BlockSpec((D_FF, D), const),
            pl.BlockSpec((1, D), const),
            pl.BlockSpec((1, D), const),
            pl.BlockSpec((1, D), const),
        ],
        out_specs=pl.BlockSpec((None, ROW_TILE, D), tok),
        out_shape=jax.ShapeDtypeStruct((B, S, D), jnp.float32),
        compiler_params=pltpu.CompilerParams(
            dimension_semantics=("arbitrary", "arbitrary"), vmem_limit_bytes=VMEM_LIMIT_BYTES),
        name="ffn_ln2",
    )(x, w1, b1, w2, b2, ln_g, ln_b)


def _encoder_layer(x, l, w_in, b_gate, lam_q1, lam_k1, lam_q2, lam_k2, subln_g, sink_logit,
                   w_br_a, w_br_b, w_out, ln1_g, ln1_b, w_ff1, b_ff1, w_ff2, b_ff2, ln2_g, ln2_b):
    bf = jnp.bfloat16
    row = lambda a: a.reshape(1, -1)
    w_t = jnp.concatenate([w_in[:, OFF_DA_Q:OFF_DA_K], w_in[:, OFF_DA_V:OFF_WG_Q],
                           w_in[:, OFF_WG_Q:OFF_WG_K], w_in[:, OFF_WG_V:OFF_GATE],
                           w_in[:, OFF_DA_K:OFF_DA_V]], axis=1).T.astype(bf)
    w_ka = w_in[:, OFF_DA_K:OFF_DA_V].reshape(D_MODEL, 2 * DA_HEADS, DA_HEAD_DIM)
    w_ka = jnp.pad(w_ka, ((0, 0), (0, 0), (0, DA_KCOLS - DA_HEAD_DIM))).reshape(D_MODEL, -1).astype(bf)
    w_kb = w_in[:, OFF_WG_K:OFF_WG_V].astype(bf)
    w_gate = w_in[:, OFF_GATE:].astype(bf)
    lam_init = 0.8 - 0.6 * math.exp(-0.3 * l)

    qa_t, va_t, qb_t, vb_t, ka_t, k_a, k_b = _project(x, w_t, w_ka, w_kb)
    o_a = _diff_attention(qa_t, ka_t, k_a, va_t, lam_q1, lam_k1, lam_q2, lam_k2, subln_g, lam_init)
    o_b = _window_attention(qb_t, k_b, vb_t, sink_logit)
    x1 = _merge(x, o_a, o_b, w_gate, row(b_gate), w_br_a.astype(bf), w_br_b.astype(bf),
                w_out.astype(bf), row(ln1_g), row(ln1_b))
    return _ffn(x1, w_ff1.astype(bf), row(b_ff1), w_ff2.astype(bf), row(b_ff2), row(ln2_g), row(ln2_b))


def kernel(x_prompt, x_sample, w_in, b_gate, lam_q1, lam_k1, lam_q2, lam_k2, subln_g, sink_logit,
           w_br_a, w_br_b, w_out, ln1_g, ln1_b, w_ff1, b_ff1, w_ff2, b_ff2, ln2_g, ln2_b):
    def run(x):
        for l in range(DEPTH):
            x = _encoder_layer(x, l, w_in[l], b_gate[l], lam_q1[l], lam_k1[l], lam_q2[l], lam_k2[l],
                               subln_g[l], sink_logit[l], w_br_a[l], w_br_b[l], w_out[l],
                               ln1_g[l], ln1_b[l], w_ff1[l], b_ff1[l], w_ff2[l], b_ff2[l],
                               ln2_g[l], ln2_b[l])
        return x

    return (run(x_prompt), run(x_sample))
```

```python
import functools
import math

import jax
import jax.numpy as jnp
import numpy as np
from jax import lax
from jax.experimental import pallas as pl
from jax.experimental.pallas import tpu as pltpu

D_MODEL = 1024
DA_HEADS = 8
DA_HEAD_DIM = 64
DA_WIDTH = DA_HEADS * 2 * DA_HEAD_DIM
WG_HEADS = 16
WG_KV_HEADS = 4
WG_HEAD_DIM = 64
WG_Q = WG_HEADS * WG_HEAD_DIM
WG_KV = WG_KV_HEADS * WG_HEAD_DIM
WINDOW = 128
D_FF = 4 * D_MODEL
DEPTH = 1
DEEPNORM_ALPHA = (2.0 * DEPTH) ** 0.25
LN_EPS = 1e-5
LOG2E = math.log2(math.e)
NEG_BIG = -1e30

OFF_DA_Q = 0
OFF_DA_K = OFF_DA_Q + DA_WIDTH
OFF_DA_V = OFF_DA_K + DA_WIDTH
OFF_WG_Q = OFF_DA_V + DA_WIDTH
OFF_WG_K = OFF_WG_Q + WG_Q
OFF_WG_V = OFF_WG_K + WG_KV
OFF_GATE = OFF_WG_V + WG_KV

LANES = 128
SUBLANES = 8
BF16_ROWS = 16
VMEM_LIMIT_BYTES = 56 * 1024 * 1024

ROW_TILE = 512
DA_TQ = 512
DA_TK = ROW_TILE
WG_TQ = 256
WG_BAND = WG_TQ + 2 * WINDOW

DA_KCOLS = 2 * DA_HEAD_DIM
AUG_ROWS = BF16_ROWS
SKIP_LOG2 = -150.0


def _alibi_slopes(n_heads):
    return 2.0 ** (-8.0 * np.arange(1, n_heads + 1) / n_heads)


def _nt_dot(a, b):
    return lax.dot_general(a, b, (((1,), (1,)), ((), ())), preferred_element_type=jnp.float32)


def _dot(a, b):
    return jnp.dot(a, b, preferred_element_type=jnp.float32)


def _proj_kernel(x_ref, wt_ref, wka_ref, wkb_ref, qa_ref, va_ref, qb_ref, vb_ref, kat_ref, ka_ref, kb_ref):
    xb = x_ref[...].astype(jnp.bfloat16)
    da_scale = DA_HEAD_DIM ** -0.5 * LOG2E
    wg_scale = WG_HEAD_DIM ** -0.5 * LOG2E
    r0, r1, r2, r3 = DA_WIDTH, 2 * DA_WIDTH, 2 * DA_WIDTH + WG_Q, 2 * DA_WIDTH + WG_Q + WG_KV
    qa_ref[...] = (_nt_dot(wt_ref[0:r0, :], xb) * da_scale).astype(jnp.bfloat16)
    va_ref[...] = _nt_dot(wt_ref[r0:r1, :], xb).astype(jnp.bfloat16)
    qb_ref[...] = (_nt_dot(wt_ref[r1:r2, :], xb) * wg_scale).astype(jnp.bfloat16)
    kat_ref[...] = _nt_dot(wt_ref[r3:, :], xb).astype(jnp.bfloat16)
    vb = _nt_dot(wt_ref[r2:r3, :], xb).astype(jnp.bfloat16)
    for t in range(ROW_TILE // LANES):
        vb_ref[t] = vb[:, t * LANES:(t + 1) * LANES]
    kb_ref[...] = _dot(xb, wkb_ref[...]).astype(jnp.bfloat16)
    pos = pl.program_id(1) * ROW_TILE + lax.broadcasted_iota(jnp.int32, (ROW_TILE, DA_KCOLS), 0)
    lane = lax.broadcasted_iota(jnp.int32, (ROW_TILE, DA_KCOLS), 1)
    pos_hi = ((pos // LANES) * LANES).astype(jnp.float32)
    pos_lo = (pos % LANES).astype(jnp.float32)
    lower = lane < DA_HEAD_DIM

    def position_cols(c0):
        return jnp.where(lane < c0, 0.0,
                         jnp.where(lane < c0 + 3, 1.0,
                                   jnp.where(lane < c0 + 6, pos_hi,
                                             jnp.where(lane < c0 + 9, pos_lo, 0.0))))
    pat0, pat1 = position_cols(DA_HEAD_DIM), position_cols(0)
    for hd in range(DA_HEADS):
        kk = _dot(xb, wka_ref[:, hd * DA_KCOLS:(hd + 1) * DA_KCOLS])
        ka_ref[:, (2 * hd) * DA_KCOLS:(2 * hd + 1) * DA_KCOLS] = jnp.where(lower, kk, pat0).astype(jnp.bfloat16)
        ka_ref[:, (2 * hd + 1) * DA_KCOLS:(2 * hd + 2) * DA_KCOLS] = jnp.where(lower, pat1, kk).astype(jnp.bfloat16)


def _project(x, w_t, w_ka, w_kb):
    B, S, D = x.shape
    ns = S // ROW_TILE
    rows_t = w_t.shape[0]
    ka_cols = 2 * DA_HEADS * DA_KCOLS
    bf = jnp.bfloat16
    return pl.pallas_call(
        _proj_kernel,
        grid=(B, ns),
        in_specs=[
            pl.BlockSpec((None, ROW_TILE, D), lambda b, s: (b, s, 0)),
            pl.BlockSpec((rows_t, D), lambda b, s: (0, 0)),
            pl.BlockSpec((D, DA_WIDTH), lambda b, s: (0, 0)),
            pl.BlockSpec((D, WG_KV), lambda b, s: (0, 0)),
        ],
        out_specs=[
            pl.BlockSpec((None, DA_WIDTH, ROW_TILE), lambda b, s: (b, 0, s)),
            pl.BlockSpec((None, None, DA_WIDTH, ROW_TILE), lambda b, s: (b, s, 0, 0)),
            pl.BlockSpec((None, WG_Q, ROW_TILE), lambda b, s: (b, 0, s)),
            pl.BlockSpec((None, ROW_TILE // LANES, WG_KV, LANES), lambda b, s: (b, s, 0, 0)),
            pl.BlockSpec((None, DA_WIDTH, ROW_TILE), lambda b, s: (b, 0, s)),
            pl.BlockSpec((None, ROW_TILE, ka_cols), lambda b, s: (b, s, 0)),
            pl.BlockSpec((None, ROW_TILE, WG_KV), lambda b, s: (b, s, 0)),
        ],
        out_shape=[
            jax.ShapeDtypeStruct((B, DA_WIDTH, S), bf),
            jax.ShapeDtypeStruct((B, ns, DA_WIDTH, ROW_TILE), bf),
            jax.ShapeDtypeStruct((B, WG_Q, S), bf),
            jax.ShapeDtypeStruct((B, S // LANES, WG_KV, LANES), bf),
            jax.ShapeDtypeStruct((B, DA_WIDTH, S), bf),
            jax.ShapeDtypeStruct((B, S, ka_cols), bf),
            jax.ShapeDtypeStruct((B, S, WG_KV), bf),
        ],
        compiler_params=pltpu.CompilerParams(
            dimension_semantics=("arbitrary", "arbitrary"), vmem_limit_bytes=VMEM_LIMIT_BYTES),
        name="in_proj",
    )(x, w_t, w_ka, w_kb)


def _split3(x):
    hi = x.astype(jnp.bfloat16).astype(jnp.float32)
    r1 = x - hi
    mid = r1.astype(jnp.bfloat16).astype(jnp.float32)
    return hi, mid, r1 - mid


def _diff_attn_kernel(slope_ref, islope_ref, q_ref, kt_ref, k_ref, v_ref, lq1_ref, lk1_ref, lq2_ref, lk2_ref, g_ref,
                      o_ref, bias_ref, kmax_ref, qaug_ref, sa_ref, sb_ref, ea_ref, eb_ref, acc_ref,
                      *, n_k, lam_init):
    tq, tk, dh = DA_TQ, DA_TK, DA_HEAD_DIM
    h = pl.program_id(1)
    qi = pl.program_id(2)
    sig = slope_ref[h]

    def k_tile(kt):
        return k_ref[pl.ds(pl.multiple_of(kt * tk, tk), tk), :]

    def col_sum8(e):
        return jnp.sum(e.reshape(tk // SUBLANES, SUBLANES, e.shape[1]), axis=0)

    def both_maps(m0, m1):
        z = jnp.zeros_like(m0)
        return jnp.concatenate([jnp.concatenate([m0, z], axis=1), jnp.concatenate([z, m1], axis=1)], axis=0)

    @pl.when(qi == 0)
    def _():
        dd = (lax.broadcasted_iota(jnp.int32, (tk, tq), 1) - lax.broadcasted_iota(jnp.int32, (tk, tq), 0))
        b = -sig * jnp.abs(dd).astype(jnp.float32)
        bias_ref[...] = jnp.concatenate([b, b], axis=1)

        def kmax_body(kt, m):
            a = jnp.abs(k_tile(kt).astype(jnp.float32)).reshape(tk // SUBLANES, SUBLANES, 2 * DA_KCOLS)
            return jnp.maximum(m, jnp.max(a, axis=0))
        m8 = lax.fori_loop(0, n_k, kmax_body, jnp.zeros((SUBLANES, 2 * DA_KCOLS), jnp.float32))
        kmax_ref[...] = jnp.broadcast_to(jnp.max(m8, axis=0, keepdims=True), kmax_ref.shape)

    q = q_ref[...]
    qc = (q[:dh], q[dh:])
    zpad = jnp.zeros((DA_KCOLS - dh, tq), q.dtype)
    qpad = both_maps(jnp.concatenate([qc[0], zpad], axis=0),
                     jnp.concatenate([zpad, qc[1]], axis=0))

    q_start = qi * tq
    kt_ov = q_start // tk

    qk_self = q.astype(jnp.float32) * kt_ref[...].astype(jnp.float32)
    r = jnp.concatenate([jnp.sum(qk_self[:dh], axis=0, keepdims=True),
                         jnp.sum(qk_self[dh:], axis=0, keepdims=True)], axis=1)

    bound = _dot(kmax_ref[...].astype(jnp.bfloat16), jnp.abs(qpad))[0:1] * 1.01
    gap = jnp.max(bound - r, axis=1, keepdims=True)
    reach = jnp.minimum((gap - SKIP_LOG2) * islope_ref[h], float(2 * n_k * tk))
    qs_f = q_start.astype(jnp.float32)
    lo_f = jnp.floor((qs_f - reach - 1.0) * (1.0 / tk))
    hi_f = jnp.floor((qs_f + float(tq) + reach) * (1.0 / tk))
    lo_t = jnp.minimum(jnp.clip(lo_f, 0.0, float(n_k)).astype(jnp.int32)[0, 0], kt_ov)
    hi_t = jnp.maximum(jnp.clip(hi_f, -1.0, float(n_k - 1)).astype(jnp.int32)[0, 0], kt_ov)
    n_left = kt_ov - lo_t
    n_tot = n_left + hi_t - kt_ov

    ipos = (q_start + lax.broadcasted_iota(jnp.int32, (1, tq), 1)).astype(jnp.float32)
    row = lax.broadcasted_iota(jnp.int32, (AUG_ROWS, tq), 0)
    sig_row = jnp.full((1, tq), sig, jnp.float32)
    zrest = jnp.zeros((DA_KCOLS - dh - AUG_ROWS, tq), q.dtype)
    for side, sgn in enumerate((1.0, -1.0, 0.0)):
        s3 = _split3(sgn * sig_row)
        maps = []
        for c in range(2):
            r3 = _split3(-(r[:, c * tq:(c + 1) * tq] + sgn * sig * ipos))
            slab = jnp.zeros((AUG_ROWS, tq), jnp.float32)
            for i, piece in enumerate(r3 + s3 + s3):
                slab = jnp.where(row == i, piece, slab)
            parts = [qc[c], slab.astype(q.dtype), zrest] if c == 0 else [slab.astype(q.dtype), zrest, qc[c]]
            maps.append(jnp.concatenate(parts, axis=0))
        qaug_ref[side] = both_maps(*maps)

    def tile_of(i):
        i = jnp.clip(i, 0, jnp.maximum(n_tot - 1, 0))
        kt = jnp.where(i < n_left, lo_t + i, kt_ov + 1 + i - n_left)
        return jnp.minimum(kt, n_k - 1), (i >= n_left).astype(jnp.int32)

    def scores(i):
        kt, side = tile_of(i)
        return _dot(k_tile(kt), qaug_ref[side])

    def exp_to(s_ref, e_ref):
        e = jnp.exp2(s_ref[...])
        e_ref[...] = e.astype(e_ref.dtype)
        return col_sum8(e)

    def add_av(i, e_ref):
        kt, _ = tile_of(i)
        acc_ref[...] += _dot(v_ref[kt], e_ref[...])

    s_ov = _dot(k_tile(kt_ov), qaug_ref[2]) + bias_ref[...]
    sa_ref[...] = scores(0)
    sb_ref[...] = scores(1)
    e = jnp.exp2(s_ov)
    l8 = col_sum8(e)
    acc_ref[...] = _dot(v_ref[kt_ov], e.astype(jnp.bfloat16))
    pend = exp_to(sa_ref, ea_ref)

    def pair(p, carry):
        l8, pend = carry
        i = 2 * p
        sa_ref[...] = scores(i + 2)
        add_av(i, ea_ref)
        pend_b = exp_to(sb_ref, eb_ref)
        sb_ref[...] = scores(i + 3)
        add_av(i + 1, eb_ref)
        return l8 + pend + pend_b, exp_to(sa_ref, ea_ref)

    n_pairs = n_tot // 2
    carry = lax.fori_loop(0, n_pairs // 2, lambda t, c: pair(2 * t + 1, pair(2 * t, c)), (l8, pend))
    l8, pend = lax.cond(n_pairs % 2 == 1, lambda c: pair(n_pairs - 1, c), lambda c: c, carry)

    def last_tile():
        add_av(n_tot - 1, ea_ref)
        return l8 + pend

    l8 = lax.cond(n_tot % 2 == 1, last_tile, lambda: l8)
    l = jnp.sum(l8, axis=0, keepdims=True)

    lam = (jnp.exp(jnp.sum(lq1_ref[...] * lk1_ref[...], axis=1, keepdims=True))
           - jnp.exp(jnp.sum(lq2_ref[...] * lk2_ref[...], axis=1, keepdims=True)) + lam_init)

    def finish(l):
        on = acc_ref[...] / l
        o = on[:, :tq] - lam * on[:, tq:]
        o = o * lax.rsqrt(jnp.mean(o * o, axis=0, keepdims=True) + LN_EPS)
        o = o * g_ref[...] * (1.0 - lam_init)
        o_ref[...] = o.T.astype(o_ref.dtype)

    finish(l)

    chk = jnp.sum(acc_ref[...] * 0.0) + jnp.sum(l * 0.0)
    overflowed = jnp.logical_not(chk == 0.0)

    def exact_path():
        dd = (lax.broadcasted_iota(jnp.int32, (tk, tq), 1)
              - lax.broadcasted_iota(jnp.int32, (tk, tq), 0)).astype(jnp.float32)
        acc_ref[...] = jnp.zeros_like(acc_ref)

        def body(kt, carry):
            m, lc = carry
            b = -sig * jnp.abs(dd + (q_start - kt * tk).astype(jnp.float32))
            u = _dot(k_tile(kt), qpad) + jnp.concatenate([b, b], axis=1)
            m_new = jnp.maximum(m, jnp.max(u, axis=0, keepdims=True))
            e = jnp.exp2(u - m_new)
            alpha = jnp.exp2(m - m_new)
            acc_ref[...] = alpha * acc_ref[...] + _dot(v_ref[kt], e.astype(jnp.bfloat16))
            return m_new, alpha * lc + jnp.sum(e, axis=0, keepdims=True)

        init = (jnp.full((1, 2 * tq), NEG_BIG, jnp.float32), jnp.zeros((1, 2 * tq), jnp.float32))
        finish(lax.fori_loop(0, n_k, body, init)[1])

    pl.when(overflowed)(exact_path)


def _diff_attention(q_t, k_t, k, v_t, lam_q1, lam_k1, lam_q2, lam_k2, subln_g, lam_init):
    B, _, S = q_t.shape
    assert DA_TQ == DA_TK, "the diagonal-tile bias assumes square tiles"
    hw = 2 * DA_HEAD_DIM
    n_k = S // DA_TK
    slopes2 = _alibi_slopes(DA_HEADS) * LOG2E
    vec = lambda a: a.reshape(1, DA_HEAD_DIM)
    small = pl.BlockSpec((1, DA_HEAD_DIM), lambda b, h, i, s1, s2: (0, 0))
    grid_spec = pltpu.PrefetchScalarGridSpec(
        num_scalar_prefetch=2,
        grid=(B, DA_HEADS, S // DA_TQ),
        in_specs=[
            pl.BlockSpec((None, hw, DA_TQ), lambda b, h, i, s1, s2: (b, h, i)),
            pl.BlockSpec((None, hw, DA_TQ), lambda b, h, i, s1, s2: (b, h, i)),
            pl.BlockSpec((None, S, 2 * DA_KCOLS), lambda b, h, i, s1, s2: (b, 0, h)),
            pl.BlockSpec((None, n_k, hw, DA_TK), lambda b, h, i, s1, s2: (b, 0, h, 0)),
            small, small, small, small,
            pl.BlockSpec((hw, 1), lambda b, h, i, s1, s2: (0, 0)),
        ],
        out_specs=pl.BlockSpec((None, DA_TQ, hw), lambda b, h, i, s1, s2: (b, i, h)),
        scratch_shapes=[
            pltpu.VMEM((DA_TK, 2 * DA_TQ), jnp.float32),
            pltpu.VMEM((SUBLANES, 2 * DA_KCOLS), jnp.float32),
            pltpu.VMEM((3, 2 * DA_KCOLS, 2 * DA_TQ), jnp.bfloat16),
            pltpu.VMEM((DA_TK, 2 * DA_TQ), jnp.float32),
            pltpu.VMEM((DA_TK, 2 * DA_TQ), jnp.float32),
            pltpu.VMEM((DA_TK, 2 * DA_TQ), jnp.bfloat16),
            pltpu.VMEM((DA_TK, 2 * DA_TQ), jnp.bfloat16),
            pltpu.VMEM((hw, 2 * DA_TQ), jnp.float32),
        ],
    )
    return pl.pallas_call(
        functools.partial(_diff_attn_kernel, n_k=n_k, lam_init=lam_init),
        grid_spec=grid_spec,
        out_shape=jax.ShapeDtypeStruct((B, S, DA_WIDTH), jnp.bfloat16),
        compiler_params=pltpu.CompilerParams(
            dimension_semantics=("arbitrary", "arbitrary", "arbitrary"),
            vmem_limit_bytes=VMEM_LIMIT_BYTES),
        name="diff_attn",
    )(jnp.asarray(slopes2, jnp.float32), jnp.asarray(1.0 / slopes2, jnp.float32),
      q_t, k_t, k, v_t, vec(lam_q1), vec(lam_k1), vec(lam_q2), vec(lam_k2), subln_g.reshape(hw, 1))


def _win_attn_kernel(slope_ref, sink_ref, q_ref, k_ref, v_ref, o_ref, ot_ref, *, seq_len):
    tq, band = WG_TQ, WG_BAND
    n_vt = band // LANES
    rep = WG_HEADS // WG_KV_HEADS
    qi = pl.program_id(1)
    q_start = qi * tq
    tile0 = jnp.clip(q_start // LANES - WINDOW // LANES, 0, seq_len // LANES - n_vt)
    k_start = pl.multiple_of(tile0 * LANES, LANES)

    kband = k_ref[pl.ds(k_start, band), :]
    kpos = k_start + lax.broadcasted_iota(jnp.int32, (band, tq), 0)
    qpos = q_start + lax.broadcasted_iota(jnp.int32, (band, tq), 1)
    dist_i = jnp.abs(qpos - kpos)
    dist = jnp.where(dist_i <= WINDOW, dist_i.astype(jnp.float32), -NEG_BIG)

    for g in range(WG_KV_HEADS):
        rows = slice(g * WG_HEAD_DIM, (g + 1) * WG_HEAD_DIM)
        blocks = []
        for r in range(rep):
            hd = g * rep + r
            parts = []
            if g > 0:
                parts.append(jnp.zeros((g * WG_HEAD_DIM, tq), q_ref.dtype))
            parts.append(q_ref[hd * WG_HEAD_DIM:(hd + 1) * WG_HEAD_DIM, :])
            if g < WG_KV_HEADS - 1:
                parts.append(jnp.zeros(((WG_KV_HEADS - 1 - g) * WG_HEAD_DIM, tq), q_ref.dtype))
            blocks.append(jnp.concatenate(parts, axis=0))
        s = _dot(kband, jnp.concatenate(blocks, axis=1))
        es, denoms = [], []
        for r in range(rep):
            hd = g * rep + r
            sr = s[:, r * tq:(r + 1) * tq] - slope_ref[hd] * dist
            sk = sink_ref[hd] * LOG2E
            m = jnp.maximum(jnp.max(sr, axis=0, keepdims=True), sk)
            e = jnp.exp2(sr - m)
            denoms.append(jnp.sum(e, axis=0, keepdims=True) + jnp.exp2(sk - m))
            es.append(e.astype(jnp.bfloat16))
        vg = jnp.concatenate([v_ref[tile0 + t, rows, :] for t in range(n_vt)], axis=1)
        og = _dot(vg, jnp.concatenate(es, axis=1)) / jnp.concatenate(denoms, axis=1)
        for r in range(rep):
            hd = g * rep + r
            ot_ref[hd * WG_HEAD_DIM:(hd + 1) * WG_HEAD_DIM, :] = og[:, r * tq:(r + 1) * tq]
    o_ref[...] = ot_ref[...].T.astype(o_ref.dtype)


def _window_attention(q_t, k, v_t, sink_logit):
    B, _, S = q_t.shape
    slopes2 = jnp.asarray(_alibi_slopes(WG_HEADS) * LOG2E, jnp.float32)
    grid_spec = pltpu.PrefetchScalarGridSpec(
        num_scalar_prefetch=2,
        grid=(B, S // WG_TQ),
        in_specs=[
            pl.BlockSpec((None, WG_Q, WG_TQ), lambda b, i, s1, s2: (b, 0, i)),
            pl.BlockSpec((None, S, WG_KV), lambda b, i, s1, s2: (b, 0, 0)),
            pl.BlockSpec((None, S // LANES, WG_KV, LANES), lambda b, i, s1, s2: (b, 0, 0, 0)),
        ],
        out_specs=pl.BlockSpec((None, WG_TQ, WG_Q), lambda b, i, s1, s2: (b, i, 0)),
        scratch_shapes=[pltpu.VMEM((WG_Q, WG_TQ), jnp.float32)],
    )
    return pl.pallas_call(
        functools.partial(_win_attn_kernel, seq_len=S),
        grid_spec=grid_spec,
        out_shape=jax.ShapeDtypeStruct((B, S, WG_Q), jnp.bfloat16),
        compiler_params=pltpu.CompilerParams(
            dimension_semantics=("arbitrary", "arbitrary"), vmem_limit_bytes=VMEM_LIMIT_BYTES),
        name="win_attn",
    )(slopes2, sink_logit.astype(jnp.float32), q_t, k, v_t)


def _layer_norm(x, g, b):
    mu = jnp.mean(x, axis=-1, keepdims=True)
    xc = x - mu
    var = jnp.mean(xc * xc, axis=-1, keepdims=True)
    return xc * lax.rsqrt(var + LN_EPS) * g + b


def _merge_kernel(x_ref, oa_ref, ob_ref, wg_ref, bg_ref, wa_ref, wb_ref, wo_ref, g_ref, b_ref, y_ref):
    x = x_ref[...]
    gates = jax.nn.sigmoid(_dot(x.astype(jnp.bfloat16), wg_ref[...]) + bg_ref[...])
    merged = (gates[:, :D_MODEL] * _dot(oa_ref[...], wa_ref[...])
              + gates[:, D_MODEL:] * _dot(ob_ref[...], wb_ref[...]))
    mix = _dot(merged.astype(jnp.bfloat16), wo_ref[...])
    y_ref[...] = _layer_norm(DEEPNORM_ALPHA * x + mix, g_ref[...], b_ref[...])


def _merge(x, o_a, o_b, w_gate, b_gate, w_br_a, w_br_b, w_out, ln_g, ln_b):
    B, S, D = x.shape
    tok = lambda b, s: (b, s, 0)
    const = lambda b, s: (0, 0)
    return pl.pallas_call(
        _merge_kernel,
        grid=(B, S // ROW_TILE),
        in_specs=[
            pl.BlockSpec((None, ROW_TILE, D), tok),
            pl.BlockSpec((None, ROW_TILE, DA_WIDTH), tok),
            pl.BlockSpec((None, ROW_TILE, WG_Q), tok),
            pl.BlockSpec((D, 2 * D), const),
            pl.BlockSpec((1, 2 * D), const),
            pl.BlockSpec((DA_WIDTH, D), const),
            pl.BlockSpec((WG_Q, D), const),
            pl.BlockSpec((D, D), const),
            pl.BlockSpec((1, D), const),
            pl.BlockSpec((1, D), const),
        ],
        out_specs=pl.BlockSpec((None, ROW_TILE, D), tok),
        out_shape=jax.ShapeDtypeStruct((B, S, D), jnp.float32),
        compiler_params=pltpu.CompilerParams(
            dimension_semantics=("arbitrary", "arbitrary"), vmem_limit_bytes=VMEM_LIMIT_BYTES),
        name="merge_ln1",
    )(x, o_a, o_b, w_gate, b_gate, w_br_a, w_br_b, w_out, ln_g, ln_b)


def _ffn_kernel(x_ref, w1_ref, b1_ref, w2_ref, b2_ref, g_ref, b_ref, y_ref):
    x = x_ref[...]
    h = jnp.maximum(_dot(x.astype(jnp.bfloat16), w1_ref[...]) + b1_ref[...], 0.0)
    f = _dot((h * h).astype(jnp.bfloat16), w2_ref[...]) + b2_ref[...]
    y_ref[...] = _layer_norm(DEEPNORM_ALPHA * x + f, g_ref[...], b_ref[...])


def _ffn(x, w1, b1, w2, b2, ln_g, ln_b):
    B, S, D = x.shape
    tok = lambda b, s: (b, s, 0)
    const = lambda b, s: (0, 0)
    return pl.pallas_call(
        _ffn_kernel,
        grid=(B, S // ROW_TILE),
        in_specs=[
            pl.BlockSpec((None, ROW_TILE, D), tok),
            pl.BlockSpec((D, D_FF), const),
            pl.BlockSpec((1, D_FF), const),
            pl.BlockSpec((D_FF, D), const),
            pl.BlockSpec((1, D), const),
            pl.BlockSpec((1, D), const),
            pl.BlockSpec((1, D), const),
        ],
        out_specs=pl.BlockSpec((None, ROW_TILE, D), tok),
        out_shape=jax.ShapeDtypeStruct((B, S, D), jnp.float32),
        compiler_params=pltpu.CompilerParams(
            dimension_semantics=("arbitrary", "arbitrary"), vmem_limit_bytes=VMEM_LIMIT_BYTES),
        name="ffn_ln2",
    )(x, w1, b1, w2, b2, ln_g, ln_b)


def _encoder_layer(x, l, w_in, b_gate, lam_q1, lam_k1, lam_q2, lam_k2, subln_g, sink_logit,
                   w_br_a, w_br_b, w_out, ln1_g, ln1_b, w_ff1, b_ff1, w_ff2, b_ff2, ln2_g, ln2_b):
    bf = jnp.bfloat16
    row = lambda a: a.reshape(1, -1)
    w_t = jnp.concatenate([w_in[:, OFF_DA_Q:OFF_DA_K], w_in[:, OFF_DA_V:OFF_WG_Q],
                           w_in[:, OFF_WG_Q:OFF_WG_K], w_in[:, OFF_WG_V:OFF_GATE],
                           w_in[:, OFF_DA_K:OFF_DA_V]], axis=1).T.astype(bf)
    w_ka = w_in[:, OFF_DA_K:OFF_DA_V].astype(bf)
    w_kb = w_in[:, OFF_WG_K:OFF_WG_V].astype(bf)
    w_gate = w_in[:, OFF_GATE:].astype(bf)
    lam_init = 0.8 - 0.6 * math.exp(-0.3 * l)

    qa_t, va_t, qb_t, vb_t, ka_t, k_a, k_b = _project(x, w_t, w_ka, w_kb)
    o_a = _diff_attention(qa_t, ka_t, k_a, va_t, lam_q1, lam_k1, lam_q2, lam_k2, subln_g, lam_init)
    o_b = _window_attention(qb_t, k_b, vb_t, sink_logit)
    x1 = _merge(x, o_a, o_b, w_gate, row(b_gate), w_br_a.astype(bf), w_br_b.astype(bf),
                w_out.astype(bf), row(ln1_g), row(ln1_b))
    return _ffn(x1, w_ff1.astype(bf), row(b_ff1), w_ff2.astype(bf), row(b_ff2), row(ln2_g), row(ln2_b))


def kernel(x_prompt, x_sample, w_in, b_gate, lam_q1, lam_k1, lam_q2, lam_k2, subln_g, sink_logit,
           w_br_a, w_br_b, w_out, ln1_g, ln1_b, w_ff1, b_ff1, w_ff2, b_ff2, ln2_g, ln2_b):
    def run(x):
        for l in range(DEPTH):
            x = _encoder_layer(x, l, w_in[l], b_gate[l], lam_q1[l], lam_k1[l], lam_q2[l], lam_k2[l],
                               subln_g[l], sink_logit[l], w_br_a[l], w_br_b[l], w_out[l],
                               ln1_g[l], ln1_b[l], w_ff1[l], b_ff1[l], w_ff2[l], b_ff2[l],
                               ln2_g[l], ln2_b[l])
        return x

    return (run(x_prompt), run(x_sample))
```

```python
import functools
import math

import jax
import jax.numpy as jnp
import numpy as np
from jax import lax
from jax.experimental import pallas as pl
from jax.experimental.pallas import tpu as pltpu

D_MODEL = 1024
DA_HEADS = 8
DA_HEAD_DIM = 64
DA_WIDTH = DA_HEADS * 2 * DA_HEAD_DIM
WG_HEADS = 16
WG_KV_HEADS = 4
WG_HEAD_DIM = 64
WG_Q = WG_HEADS * WG_HEAD_DIM
WG_KV = WG_KV_HEADS * WG_HEAD_DIM
WINDOW = 128
D_FF = 4 * D_MODEL
DEPTH = 1
DEEPNORM_ALPHA = (2.0 * DEPTH) ** 0.25
LN_EPS = 1e-5
LOG2E = math.log2(math.e)
NEG_BIG = -1e30

OFF_DA_Q = 0
OFF_DA_K = OFF_DA_Q + DA_WIDTH
OFF_DA_V = OFF_DA_K + DA_WIDTH
OFF_WG_Q = OFF_DA_V + DA_WIDTH
OFF_WG_K = OFF_WG_Q + WG_Q
OFF_WG_V = OFF_WG_K + WG_KV
OFF_GATE = OFF_WG_V + WG_KV

LANES = 128
SUBLANES = 8
BF16_ROWS = 16
VMEM_LIMIT_BYTES = 56 * 1024 * 1024

ROW_TILE = 512
DA_TQ = 512
DA_TK = ROW_TILE
WG_TQ = 256
WG_BAND = WG_TQ + 2 * WINDOW

DA_KCOLS = 2 * DA_HEAD_DIM
AUG_ROWS = BF16_ROWS
SKIP_LOG2 = -150.0


def _alibi_slopes(n_heads):
    return 2.0 ** (-8.0 * np.arange(1, n_heads + 1) / n_heads)


def _nt_dot(a, b):
    return lax.dot_general(a, b, (((1,), (1,)), ((), ())), preferred_element_type=jnp.float32)


def _dot(a, b):
    return jnp.dot(a, b, preferred_element_type=jnp.float32)


def _proj_kernel(x_ref, wt_ref, wka_ref, wkb_ref, qa_ref, va_ref, qb_ref, vb_ref, kat_ref, ka_ref, kb_ref):
    xb = x_ref[...].astype(jnp.bfloat16)
    da_scale = DA_HEAD_DIM ** -0.5 * LOG2E
    wg_scale = WG_HEAD_DIM ** -0.5 * LOG2E
    r0, r1, r2, r3 = DA_WIDTH, 2 * DA_WIDTH, 2 * DA_WIDTH + WG_Q, 2 * DA_WIDTH + WG_Q + WG_KV
    qa_ref[...] = (_nt_dot(wt_ref[0:r0, :], xb) * da_scale).astype(jnp.bfloat16)
    va_ref[...] = _nt_dot(wt_ref[r0:r1, :], xb).astype(jnp.bfloat16)
    qb_ref[...] = (_nt_dot(wt_ref[r1:r2, :], xb) * wg_scale).astype(jnp.bfloat16)
    kat_ref[...] = _nt_dot(wt_ref[r3:, :], xb).astype(jnp.bfloat16)
    vb = _nt_dot(wt_ref[r2:r3, :], xb).astype(jnp.bfloat16)
    for t in range(ROW_TILE // LANES):
        vb_ref[t] = vb[:, t * LANES:(t + 1) * LANES]
    kb_ref[...] = _dot(xb, wkb_ref[...]).astype(jnp.bfloat16)
    pos = pl.program_id(1) * ROW_TILE + lax.broadcasted_iota(jnp.int32, (ROW_TILE, DA_KCOLS), 0)
    lane = lax.broadcasted_iota(jnp.int32, (ROW_TILE, DA_KCOLS), 1)
    pos_hi = ((pos // LANES) * LANES).astype(jnp.float32)
    pos_lo = (pos % LANES).astype(jnp.float32)
    lower = lane < DA_HEAD_DIM

    def position_cols(c0):
        return jnp.where(lane < c0, 0.0,
                         jnp.where(lane < c0 + 3, 1.0,
                                   jnp.where(lane < c0 + 6, pos_hi,
                                             jnp.where(lane < c0 + 9, pos_lo, 0.0))))
    pat0, pat1 = position_cols(DA_HEAD_DIM), position_cols(0)
    for hd in range(DA_HEADS):
        kk = _dot(xb, wka_ref[:, hd * DA_KCOLS:(hd + 1) * DA_KCOLS])
        ka_ref[:, (2 * hd) * DA_KCOLS:(2 * hd + 1) * DA_KCOLS] = jnp.where(lower, kk, pat0).astype(jnp.bfloat16)
        ka_ref[:, (2 * hd + 1) * DA_KCOLS:(2 * hd + 2) * DA_KCOLS] = jnp.where(lower, pat1, kk).astype(jnp.bfloat16)


def _project(x, w_t, w_ka, w_kb):
    B, S, D = x.shape
    ns = S // ROW_TILE
    rows_t = w_t.shape[0]
    ka_cols = 2 * DA_HEADS * DA_KCOLS
    bf = jnp.bfloat16
    return pl.pallas_call(
        _proj_kernel,
        grid=(B, ns),
        in_specs=[
            pl.BlockSpec((None, ROW_TILE, D), lambda b, s: (b, s, 0)),
            pl.BlockSpec((rows_t, D), lambda b, s: (0, 0)),
            pl.BlockSpec((D, DA_WIDTH), lambda b, s: (0, 0)),
            pl.BlockSpec((D, WG_KV), lambda b, s: (0, 0)),
        ],
        out_specs=[
            pl.BlockSpec((None, DA_WIDTH, ROW_TILE), lambda b, s: (b, 0, s)),
            pl.BlockSpec((None, None, DA_WIDTH, ROW_TILE), lambda b, s: (b, s, 0, 0)),
            pl.BlockSpec((None, WG_Q, ROW_TILE), lambda b, s: (b, 0, s)),
            pl.BlockSpec((None, ROW_TILE // LANES, WG_KV, LANES), lambda b, s: (b, s, 0, 0)),
            pl.BlockSpec((None, DA_WIDTH, ROW_TILE), lambda b, s: (b, 0, s)),
            pl.BlockSpec((None, ROW_TILE, ka_cols), lambda b, s: (b, s, 0)),
            pl.BlockSpec((None, ROW_TILE, WG_KV), lambda b, s: (b, s, 0)),
        ],
        out_shape=[
            jax.ShapeDtypeStruct((B, DA_WIDTH, S), bf),
            jax.ShapeDtypeStruct((B, ns, DA_WIDTH, ROW_TILE), bf),
            jax.ShapeDtypeStruct((B, WG_Q, S), bf),
            jax.ShapeDtypeStruct((B, S // LANES, WG_KV, LANES), bf),
            jax.ShapeDtypeStruct((B, DA_WIDTH, S), bf),
            jax.ShapeDtypeStruct((B, S, ka_cols), bf),
            jax.ShapeDtypeStruct((B, S, WG_KV), bf),
        ],
        compiler_params=pltpu.CompilerParams(
            dimension_semantics=("arbitrary", "arbitrary"), vmem_limit_bytes=VMEM_LIMIT_BYTES),
        name="in_proj",
    )(x, w_t, w_ka, w_kb)


def _split3(x):
    hi = x.astype(jnp.bfloat16).astype(jnp.float32)
    r1 = x - hi
    mid = r1.astype(jnp.bfloat16).astype(jnp.float32)
    return hi, mid, r1 - mid


def _diff_attn_kernel(slope_ref, islope_ref, q_ref, kt_ref, k_ref, v_ref, lq1_ref, lk1_ref, lq2_ref, lk2_ref, g_ref,
                      o_ref, bias_ref, kmax_ref, qaug_ref, ea_ref, eb_ref, acc_ref,
                      *, n_k, lam_init):
    tq, tk, dh = DA_TQ, DA_TK, DA_HEAD_DIM
    h = pl.program_id(1)
    qi = pl.program_id(2)
    sig = slope_ref[h]

    def k_tile(kt):
        return k_ref[pl.ds(pl.multiple_of(kt * tk, tk), tk), :]

    def col_sum8(e):
        return jnp.sum(e.reshape(tk // SUBLANES, SUBLANES, e.shape[1]), axis=0)

    def both_maps(m0, m1):
        z = jnp.zeros_like(m0)
        return jnp.concatenate([jnp.concatenate([m0, z], axis=1), jnp.concatenate([z, m1], axis=1)], axis=0)

    @pl.when(qi == 0)
    def _():
        dd = (lax.broadcasted_iota(jnp.int32, (tk, tq), 1) - lax.broadcasted_iota(jnp.int32, (tk, tq), 0))
        b = -sig * jnp.abs(dd).astype(jnp.float32)
        bias_ref[...] = jnp.concatenate([b, b], axis=1)

        def kmax_body(kt, m):
            a = jnp.abs(k_tile(kt).astype(jnp.float32)).reshape(tk // SUBLANES, SUBLANES, 2 * DA_KCOLS)
            return jnp.maximum(m, jnp.max(a, axis=0))
        m8 = lax.fori_loop(0, n_k, kmax_body, jnp.zeros((SUBLANES, 2 * DA_KCOLS), jnp.float32))
        kmax_ref[...] = jnp.broadcast_to(jnp.max(m8, axis=0, keepdims=True), kmax_ref.shape)

    q = q_ref[...]
    qc = (q[:dh], q[dh:])
    zpad = jnp.zeros((DA_KCOLS - dh, tq), q.dtype)
    qpad = both_maps(jnp.concatenate([qc[0], zpad], axis=0),
                     jnp.concatenate([zpad, qc[1]], axis=0))

    q_start = qi * tq
    kt_ov = q_start // tk

    qk_self = q.astype(jnp.float32) * kt_ref[...].astype(jnp.float32)
    r = jnp.concatenate([jnp.sum(qk_self[:dh], axis=0, keepdims=True),
                         jnp.sum(qk_self[dh:], axis=0, keepdims=True)], axis=1)

    bound = _dot(kmax_ref[...].astype(jnp.bfloat16), jnp.abs(qpad))[0:1] * 1.01
    gap = jnp.max(bound - r, axis=1, keepdims=True)
    reach = jnp.minimum((gap - SKIP_LOG2) * islope_ref[h], float(2 * n_k * tk))
    qs_f = q_start.astype(jnp.float32)
    lo_f = jnp.floor((qs_f - reach - 1.0) * (1.0 / tk))
    hi_f = jnp.floor((qs_f + float(tq) + reach) * (1.0 / tk))
    lo_t = jnp.minimum(jnp.clip(lo_f, 0.0, float(n_k)).astype(jnp.int32)[0, 0], kt_ov)
    hi_t = jnp.maximum(jnp.clip(hi_f, -1.0, float(n_k - 1)).astype(jnp.int32)[0, 0], kt_ov)
    n_left = kt_ov - lo_t
    n_tot = n_left + hi_t - kt_ov

    ipos = (q_start + lax.broadcasted_iota(jnp.int32, (1, tq), 1)).astype(jnp.float32)
    row = lax.broadcasted_iota(jnp.int32, (AUG_ROWS, tq), 0)
    sig_row = jnp.full((1, tq), sig, jnp.float32)
    zrest = jnp.zeros((DA_KCOLS - dh - AUG_ROWS, tq), q.dtype)
    for side, sgn in enumerate((1.0, -1.0, 0.0)):
        s3 = _split3(sgn * sig_row)
        maps = []
        for c in range(2):
            r3 = _split3(-(r[:, c * tq:(c + 1) * tq] + sgn * sig * ipos))
            slab = jnp.zeros((AUG_ROWS, tq), jnp.float32)
            for i, piece in enumerate(r3 + s3 + s3):
                slab = jnp.where(row == i, piece, slab)
            parts = [qc[c], slab.astype(q.dtype), zrest] if c == 0 else [slab.astype(q.dtype), zrest, qc[c]]
            maps.append(jnp.concatenate(parts, axis=0))
        qaug_ref[side] = both_maps(*maps)

    def tile_of(i):
        i = jnp.clip(i, 0, jnp.maximum(n_tot - 1, 0))
        kt = jnp.where(i < n_left, lo_t + i, kt_ov + 1 + i - n_left)
        return jnp.minimum(kt, n_k - 1), (i >= n_left).astype(jnp.int32)

    def scores(i):
        kt, side = tile_of(i)
        return _dot(k_tile(kt), qaug_ref[side])

    def exp_scores(i, e_ref):
        e = jnp.exp2(scores(i))
        e_ref[...] = e.astype(e_ref.dtype)
        return col_sum8(e)

    def add_av(i, e_ref):
        kt, _ = tile_of(i)
        acc_ref[...] += _dot(v_ref[kt], e_ref[...])

    e = jnp.exp2(_dot(k_tile(kt_ov), qaug_ref[2]) + bias_ref[...])
    pend = exp_scores(0, ea_ref)
    l8 = col_sum8(e)
    acc_ref[...] = _dot(v_ref[kt_ov], e.astype(jnp.bfloat16))

    def pair(p, carry):
        l8, pend = carry
        i = 2 * p
        pend_b = exp_scores(i + 1, eb_ref)
        add_av(i, ea_ref)
        pend_a = exp_scores(i + 2, ea_ref)
        add_av(i + 1, eb_ref)
        return l8 + pend + pend_b, pend_a

    n_pairs = n_tot // 2
    carry = lax.fori_loop(0, n_pairs // 2, lambda t, c: pair(2 * t + 1, pair(2 * t, c)), (l8, pend))
    l8, pend = lax.cond(n_pairs % 2 == 1, lambda c: pair(n_pairs - 1, c), lambda c: c, carry)

    def last_tile():
        add_av(n_tot - 1, ea_ref)
        return l8 + pend

    l8 = lax.cond(n_tot % 2 == 1, last_tile, lambda: l8)
    l = jnp.sum(l8, axis=0, keepdims=True)

    lam = (jnp.exp(jnp.sum(lq1_ref[...] * lk1_ref[...], axis=1, keepdims=True))
           - jnp.exp(jnp.sum(lq2_ref[...] * lk2_ref[...], axis=1, keepdims=True)) + lam_init)

    def finish(l):
        on = acc_ref[...] / l
        o = on[:, :tq] - lam * on[:, tq:]
        o = o * lax.rsqrt(jnp.mean(o * o, axis=0, keepdims=True) + LN_EPS)
        o = o * g_ref[...] * (1.0 - lam_init)
        o_ref[...] = o.T.astype(o_ref.dtype)

    finish(l)

    chk = jnp.sum(acc_ref[...] * 0.0) + jnp.sum(l * 0.0)
    overflowed = jnp.logical_not(chk == 0.0)

    def exact_path():
        dd = (lax.broadcasted_iota(jnp.int32, (tk, tq), 1)
              - lax.broadcasted_iota(jnp.int32, (tk, tq), 0)).astype(jnp.float32)
        acc_ref[...] = jnp.zeros_like(acc_ref)

        def body(kt, carry):
            m, lc = carry
            b = -sig * jnp.abs(dd + (q_start - kt * tk).astype(jnp.float32))
            u = _dot(k_tile(kt), qpad) + jnp.concatenate([b, b], axis=1)
            m_new = jnp.maximum(m, jnp.max(u, axis=0, keepdims=True))
            e = jnp.exp2(u - m_new)
            alpha = jnp.exp2(m - m_new)
            acc_ref[...] = alpha * acc_ref[...] + _dot(v_ref[kt], e.astype(jnp.bfloat16))
            return m_new, alpha * lc + jnp.sum(e, axis=0, keepdims=True)

        init = (jnp.full((1, 2 * tq), NEG_BIG, jnp.float32), jnp.zeros((1, 2 * tq), jnp.float32))
        finish(lax.fori_loop(0, n_k, body, init)[1])

    pl.when(overflowed)(exact_path)


def _diff_attention(q_t, k_t, k, v_t, lam_q1, lam_k1, lam_q2, lam_k2, subln_g, lam_init):
    B, _, S = q_t.shape
    assert DA_TQ == DA_TK, "the diagonal-tile bias assumes square tiles"
    hw = 2 * DA_HEAD_DIM
    n_k = S // DA_TK
    slopes2 = _alibi_slopes(DA_HEADS) * LOG2E
    vec = lambda a: a.reshape(1, DA_HEAD_DIM)
    small = pl.BlockSpec((1, DA_HEAD_DIM), lambda b, h, i, s1, s2: (0, 0))
    grid_spec = pltpu.PrefetchScalarGridSpec(
        num_scalar_prefetch=2,
        grid=(B, DA_HEADS, S // DA_TQ),
        in_specs=[
            pl.BlockSpec((None, hw, DA_TQ), lambda b, h, i, s1, s2: (b, h, i)),
            pl.BlockSpec((None, hw, DA_TQ), lambda b, h, i, s1, s2: (b, h, i)),
            pl.BlockSpec((None, S, 2 * DA_KCOLS), lambda b, h, i, s1, s2: (b, 0, h)),
            pl.BlockSpec((None, n_k, hw, DA_TK), lambda b, h, i, s1, s2: (b, 0, h, 0)),
            small, small, small, small,
            pl.BlockSpec((hw, 1), lambda b, h, i, s1, s2: (0, 0)),
        ],
        out_specs=pl.BlockSpec((None, DA_TQ, hw), lambda b, h, i, s1, s2: (b, i, h)),
        scratch_shapes=[
            pltpu.VMEM((DA_TK, 2 * DA_TQ), jnp.float32),
            pltpu.VMEM((SUBLANES, 2 * DA_KCOLS), jnp.float32),
            pltpu.VMEM((3, 2 * DA_KCOLS, 2 * DA_TQ), jnp.bfloat16),
            pltpu.VMEM((DA_TK, 2 * DA_TQ), jnp.bfloat16),
            pltpu.VMEM((DA_TK, 2 * DA_TQ), jnp.bfloat16),
            pltpu.VMEM((hw, 2 * DA_TQ), jnp.float32),
        ],
    )
    return pl.pallas_call(
        functools.partial(_diff_attn_kernel, n_k=n_k, lam_init=lam_init),
        grid_spec=grid_spec,
        out_shape=jax.ShapeDtypeStruct((B, S, DA_WIDTH), jnp.bfloat16),
        compiler_params=pltpu.CompilerParams(
            dimension_semantics=("arbitrary", "arbitrary", "arbitrary"),
            vmem_limit_bytes=VMEM_LIMIT_BYTES),
        name="diff_attn",
    )(jnp.asarray(slopes2, jnp.float32), jnp.asarray(1.0 / slopes2, jnp.float32),
      q_t, k_t, k, v_t, vec(lam_q1), vec(lam_k1), vec(lam_q2), vec(lam_k2), subln_g.reshape(hw, 1))


def _win_attn_kernel(slope_ref, sink_ref, q_ref, k_ref, v_ref, o_ref, ot_ref, *, seq_len):
    tq, band = WG_TQ, WG_BAND
    n_vt = band // LANES
    rep = WG_HEADS // WG_KV_HEADS
    qi = pl.program_id(1)
    q_start = qi * tq
    tile0 = jnp.clip(q_start // LANES - WINDOW // LANES, 0, seq_len // LANES - n_vt)
    k_start = pl.multiple_of(tile0 * LANES, LANES)

    kband = k_ref[pl.ds(k_start, band), :]
    kpos = k_start + lax.broadcasted_iota(jnp.int32, (band, tq), 0)
    qpos = q_start + lax.broadcasted_iota(jnp.int32, (band, tq), 1)
    dist_i = jnp.abs(qpos - kpos)
    dist = jnp.where(dist_i <= WINDOW, dist_i.astype(jnp.float32), -NEG_BIG)

    for g in range(WG_KV_HEADS):
        rows = slice(g * WG_HEAD_DIM, (g + 1) * WG_HEAD_DIM)
        blocks = []
        for r in range(rep):
            hd = g * rep + r
            parts = []
            if g > 0:
                parts.append(jnp.zeros((g * WG_HEAD_DIM, tq), q_ref.dtype))
            parts.append(q_ref[hd * WG_HEAD_DIM:(hd + 1) * WG_HEAD_DIM, :])
            if g < WG_KV_HEADS - 1:
                parts.append(jnp.zeros(((WG_KV_HEADS - 1 - g) * WG_HEAD_DIM, tq), q_ref.dtype))
            blocks.append(jnp.concatenate(parts, axis=0))
        s = _dot(kband, jnp.concatenate(blocks, axis=1))
        es, denoms = [], []
        for r in range(rep):
            hd = g * rep + r
            sr = s[:, r * tq:(r + 1) * tq] - slope_ref[hd] * dist
            sk = sink_ref[hd] * LOG2E
            m = jnp.maximum(jnp.max(sr, axis=0, keepdims=True), sk)
            e = jnp.exp2(sr - m)
            denoms.append(jnp.sum(e, axis=0, keepdims=True) + jnp.exp2(sk - m))
            es.append(e.astype(jnp.bfloat16))
        vg = jnp.concatenate([v_ref[tile0 + t, rows, :] for t in range(n_vt)], axis=1)
        og = _dot(vg, jnp.concatenate(es, axis=1)) / jnp.concatenate(denoms, axis=1)
        for r in range(rep):
            hd = g * rep + r
            ot_ref[hd * WG_HEAD_DIM:(hd + 1) * WG_HEAD_DIM, :] = og[:, r * tq:(r + 1) * tq]
    o_ref[...] = ot_ref[...].T.astype(o_ref.dtype)


def _window_attention(q_t, k, v_t, sink_logit):
    B, _, S = q_t.shape
    slopes2 = jnp.asarray(_alibi_slopes(WG_HEADS) * LOG2E, jnp.float32)
    grid_spec = pltpu.PrefetchScalarGridSpec(
        num_scalar_prefetch=2,
        grid=(B, S // WG_TQ),
        in_specs=[
            pl.BlockSpec((None, WG_Q, WG_TQ), lambda b, i, s1, s2: (b, 0, i)),
            pl.BlockSpec((None, S, WG_KV), lambda b, i, s1, s2: (b, 0, 0)),
            pl.BlockSpec((None, S // LANES, WG_KV, LANES), lambda b, i, s1, s2: (b, 0, 0, 0)),
        ],
        out_specs=pl.BlockSpec((None, WG_TQ, WG_Q), lambda b, i, s1, s2: (b, i, 0)),
        scratch_shapes=[pltpu.VMEM((WG_Q, WG_TQ), jnp.float32)],
    )
    return pl.pallas_call(
        functools.partial(_win_attn_kernel, seq_len=S),
        grid_spec=grid_spec,
        out_shape=jax.ShapeDtypeStruct((B, S, WG_Q), jnp.bfloat16),
        compiler_params=pltpu.CompilerParams(
            dimension_semantics=("arbitrary", "arbitrary"), vmem_limit_bytes=VMEM_LIMIT_BYTES),
        name="win_attn",
    )(slopes2, sink_logit.astype(jnp.float32), q_t, k, v_t)


def _layer_norm(x, g, b):
    mu = jnp.mean(x, axis=-1, keepdims=True)
    xc = x - mu
    var = jnp.mean(xc * xc, axis=-1, keepdims=True)
    return xc * lax.rsqrt(var + LN_EPS) * g + b


def _merge_kernel(x_ref, oa_ref, ob_ref, wg_ref, bg_ref, wa_ref, wb_ref, wo_ref, g_ref, b_ref, y_ref):
    x = x_ref[...]
    gates = jax.nn.sigmoid(_dot(x.astype(jnp.bfloat16), wg_ref[...]) + bg_ref[...])
    merged = (gates[:, :D_MODEL] * _dot(oa_ref[...], wa_ref[...])
              + gates[:, D_MODEL:] * _dot(ob_ref[...], wb_ref[...]))
    mix = _dot(merged.astype(jnp.bfloat16), wo_ref[...])
    y_ref[...] = _layer_norm(DEEPNORM_ALPHA * x + mix, g_ref[...], b_ref[...])


def _merge(x, o_a, o_b, w_gate, b_gate, w_br_a, w_br_b, w_out, ln_g, ln_b):
    B, S, D = x.shape
    tok = lambda b, s: (b, s, 0)
    const = lambda b, s: (0, 0)
    return pl.pallas_call(
        _merge_kernel,
        grid=(B, S // ROW_TILE),
        in_specs=[
            pl.BlockSpec((None, ROW_TILE, D), tok),
            pl.BlockSpec((None, ROW_TILE, DA_WIDTH), tok),
            pl.BlockSpec((None, ROW_TILE, WG_Q), tok),
            pl.BlockSpec((D, 2 * D), const),
            pl.BlockSpec((1, 2 * D), const),
            pl.BlockSpec((DA_WIDTH, D), const),
            pl.BlockSpec((WG_Q, D), const),
            pl.BlockSpec((D, D), const),
            pl.BlockSpec((1, D), const),
            pl.BlockSpec((1, D), const),
        ],
        out_specs=pl.BlockSpec((None, ROW_TILE, D), tok),
        out_shape=jax.ShapeDtypeStruct((B, S, D), jnp.float32),
        compiler_params=pltpu.CompilerParams(
            dimension_semantics=("arbitrary", "arbitrary"), vmem_limit_bytes=VMEM_LIMIT_BYTES),
        name="merge_ln1",
    )(x, o_a, o_b, w_gate, b_gate, w_br_a, w_br_b, w_out, ln_g, ln_b)


def _ffn_kernel(x_ref, w1_ref, b1_ref, w2_ref, b2_ref, g_ref, b_ref, y_ref):
    x = x_ref[...]
    h = jnp.maximum(_dot(x.astype(jnp.bfloat16), w1_ref[...]) + b1_ref[...], 0.0)
    f = _dot((h * h).astype(jnp.bfloat16), w2_ref[...]) + b2_ref[...]
    y_ref[...] = _layer_norm(DEEPNORM_ALPHA * x + f, g_ref[...], b_ref[...])


def _ffn(x, w1, b1, w2, b2, ln_g, ln_b):
    B, S, D = x.shape
    tok = lambda b, s: (b, s, 0)
    const = lambda b, s: (0, 0)
    return pl.pallas_call(
        _ffn_kernel,
        grid=(B, S // ROW_TILE),
        in_specs=[
            pl.BlockSpec((None, ROW_TILE, D), tok),
            pl.BlockSpec((D, D_FF), const),
            pl.BlockSpec((1, D_FF), const),
            pl.BlockSpec((D_FF, D), const),
            pl.BlockSpec((1, D), const),
            pl.BlockSpec((1, D), const),
            pl.BlockSpec((1, D), const),
        ],
        out_specs=pl.BlockSpec((None, ROW_TILE, D), tok),
        out_shape=jax.ShapeDtypeStruct((B, S, D), jnp.float32),
        compiler_params=pltpu.CompilerParams(
            dimension_semantics=("arbitrary", "arbitrary"), vmem_limit_bytes=VMEM_LIMIT_BYTES),
        name="ffn_ln2",
    )(x, w1, b1, w2, b2, ln_g, ln_b)


def _encoder_layer(x, l, w_in, b_gate, lam_q1, lam_k1, lam_q2, lam_k2, subln_g, sink_logit,
                   w_br_a, w_br_b, w_out, ln1_g, ln1_b, w_ff1, b_ff1, w_ff2, b_ff2, ln2_g, ln2_b):
    bf = jnp.bfloat16
    row = lambda a: a.reshape(1, -1)
    w_t = jnp.concatenate([w_in[:, OFF_DA_Q:OFF_DA_K], w_in[:, OFF_DA_V:OFF_WG_Q],
                           w_in[:, OFF_WG_Q:OFF_WG_K], w_in[:, OFF_WG_V:OFF_GATE],
                           w_in[:, OFF_DA_K:OFF_DA_V]], axis=1).T.astype(bf)
    w_ka = w_in[:, OFF_DA_K:OFF_DA_V].astype(bf)
    w_kb = w_in[:, OFF_WG_K:OFF_WG_V].astype(bf)
    w_gate = w_in[:, OFF_GATE:].astype(bf)
    lam_init = 0.8 - 0.6 * math.exp(-0.3 * l)

    qa_t, va_t, qb_t, vb_t, ka_t, k_a, k_b = _project(x, w_t, w_ka, w_kb)
    o_a = _diff_attention(qa_t, ka_t, k_a, va_t, lam_q1, lam_k1, lam_q2, lam_k2, subln_g, lam_init)
    o_b = _window_attention(qb_t, k_b, vb_t, sink_logit)
    x1 = _merge(x, o_a, o_b, w_gate, row(b_gate), w_br_a.astype(bf), w_br_b.astype(bf),
                w_out.astype(bf), row(ln1_g), row(ln1_b))
    return _ffn(x1, w_ff1.astype(bf), row(b_ff1), w_ff2.astype(bf), row(b_ff2), row(ln2_g), row(ln2_b))


def kernel(x_prompt, x_sample, w_in, b_gate, lam_q1, lam_k1, lam_q2, lam_k2, subln_g, sink_logit,
           w_br_a, w_br_b, w_out, ln1_g, ln1_b, w_ff1, b_ff1, w_ff2, b_ff2, ln2_g, ln2_b):
    def run(x):
        for l in range(DEPTH):
            x = _encoder_layer(x, l, w_in[l], b_gate[l], lam_q1[l], lam_k1[l], lam_q2[l], lam_k2[l],
                               subln_g[l], sink_logit[l], w_br_a[l], w_br_b[l], w_out[l],
                               ln1_g[l], ln1_b[l], w_ff1[l], b_ff1[l], w_ff2[l], b_ff2[l],
                               ln2_g[l], ln2_b[l])
        return x

    return (run(x_prompt), run(x_sample))
```

```python
import functools
import math

import jax
import jax.numpy as jnp
import numpy as np
from jax import lax
from jax.experimental import pallas as pl
from jax.experimental.pallas import tpu as pltpu

D_MODEL = 1024
DA_HEADS = 8
DA_HEAD_DIM = 64
DA_WIDTH = DA_HEADS * 2 * DA_HEAD_DIM
WG_HEADS = 16
WG_KV_HEADS = 4
WG_HEAD_DIM = 64
WG_Q = WG_HEADS * WG_HEAD_DIM
WG_KV = WG_KV_HEADS * WG_HEAD_DIM
WINDOW = 128
D_FF = 4 * D_MODEL
DEPTH = 1
DEEPNORM_ALPHA = (2.0 * DEPTH) ** 0.25
LN_EPS = 1e-5
LOG2E = math.log2(math.e)
NEG_BIG = -1e30

OFF_DA_Q = 0
OFF_DA_K = OFF_DA_Q + DA_WIDTH
OFF_DA_V = OFF_DA_K + DA_WIDTH
OFF_WG_Q = OFF_DA_V + DA_WIDTH
OFF_WG_K = OFF_WG_Q + WG_Q
OFF_WG_V = OFF_WG_K + WG_KV
OFF_GATE = OFF_WG_V + WG_KV

LANES = 128
SUBLANES = 8
BF16_ROWS = 16
VMEM_LIMIT_BYTES = 56 * 1024 * 1024

ROW_TILE = 512
DA_TQ = 512
DA_TK = ROW_TILE
WG_TQ = 256
WG_BAND = WG_TQ + 2 * WINDOW
WG_MIN_DENOM = 2.0 ** -60

DA_KCOLS = 2 * DA_HEAD_DIM
AUG_ROWS = BF16_ROWS
SKIP_LOG2 = -150.0


def _alibi_slopes(n_heads):
    return 2.0 ** (-8.0 * np.arange(1, n_heads + 1) / n_heads)


def _nt_dot(a, b):
    return lax.dot_general(a, b, (((1,), (1,)), ((), ())), preferred_element_type=jnp.float32)


def _dot(a, b):
    return jnp.dot(a, b, preferred_element_type=jnp.float32)


def _proj_kernel(x_ref, wt_ref, wka_ref, wkb_ref,
                 qa_ref, va_ref, qb_ref, vb_ref, kat_ref, ka_ref, kb_ref, knsq_ref):
    xb = x_ref[...].astype(jnp.bfloat16)
    da_scale = DA_HEAD_DIM ** -0.5 * LOG2E
    wg_scale = WG_HEAD_DIM ** -0.5 * LOG2E
    r0, r1, r2, r3 = DA_WIDTH, 2 * DA_WIDTH, 2 * DA_WIDTH + WG_Q, 2 * DA_WIDTH + WG_Q + WG_KV
    qa_ref[...] = (_nt_dot(wt_ref[0:r0, :], xb) * da_scale).astype(jnp.bfloat16)
    va_ref[...] = _nt_dot(wt_ref[r0:r1, :], xb).astype(jnp.bfloat16)
    qb_ref[...] = (_nt_dot(wt_ref[r1:r2, :], xb) * wg_scale).astype(jnp.bfloat16)
    kat = _nt_dot(wt_ref[r3:, :], xb)
    kat_ref[...] = kat.astype(jnp.bfloat16)
    nsq = jnp.sum((kat * kat).reshape(2 * DA_HEADS, DA_HEAD_DIM, ROW_TILE), axis=1)
    @pl.when(pl.program_id(1) == 0)
    def _():
        knsq_ref[...] = nsq

    @pl.when(pl.program_id(1) > 0)
    def _():
        knsq_ref[...] = jnp.maximum(knsq_ref[...], nsq)
    vb = _nt_dot(wt_ref[r2:r3, :], xb).astype(jnp.bfloat16)
    for t in range(ROW_TILE // LANES):
        vb_ref[t] = vb[:, t * LANES:(t + 1) * LANES]
    kb_ref[...] = _dot(xb, wkb_ref[...]).astype(jnp.bfloat16)
    pos = pl.program_id(1) * ROW_TILE + lax.broadcasted_iota(jnp.int32, (ROW_TILE, DA_KCOLS), 0)
    lane = lax.broadcasted_iota(jnp.int32, (ROW_TILE, DA_KCOLS), 1)
    pos_hi = ((pos // LANES) * LANES).astype(jnp.float32)
    pos_lo = (pos % LANES).astype(jnp.float32)
    lower = lane < DA_HEAD_DIM

    def position_cols(c0):
        return jnp.where(lane < c0, 0.0,
                         jnp.where(lane < c0 + 3, 1.0,
                                   jnp.where(lane < c0 + 6, pos_hi,
                                             jnp.where(lane < c0 + 9, pos_lo, 0.0))))
    pat0, pat1 = position_cols(DA_HEAD_DIM), position_cols(0)
    for hd in range(DA_HEADS):
        kk = _dot(xb, wka_ref[:, hd * DA_KCOLS:(hd + 1) * DA_KCOLS])
        ka_ref[:, (2 * hd) * DA_KCOLS:(2 * hd + 1) * DA_KCOLS] = jnp.where(lower, kk, pat0).astype(jnp.bfloat16)
        ka_ref[:, (2 * hd + 1) * DA_KCOLS:(2 * hd + 2) * DA_KCOLS] = jnp.where(lower, pat1, kk).astype(jnp.bfloat16)


def _project(x, w_t, w_ka, w_kb):
    B, S, D = x.shape
    ns = S // ROW_TILE
    rows_t = w_t.shape[0]
    ka_cols = 2 * DA_HEADS * DA_KCOLS
    bf = jnp.bfloat16
    return pl.pallas_call(
        _proj_kernel,
        grid=(B, ns),
        in_specs=[
            pl.BlockSpec((None, ROW_TILE, D), lambda b, s: (b, s, 0)),
            pl.BlockSpec((rows_t, D), lambda b, s: (0, 0)),
            pl.BlockSpec((D, DA_WIDTH), lambda b, s: (0, 0)),
            pl.BlockSpec((D, WG_KV), lambda b, s: (0, 0)),
        ],
        out_specs=[
            pl.BlockSpec((None, DA_WIDTH, ROW_TILE), lambda b, s: (b, 0, s)),
            pl.BlockSpec((None, None, DA_WIDTH, ROW_TILE), lambda b, s: (b, s, 0, 0)),
            pl.BlockSpec((None, WG_Q, ROW_TILE), lambda b, s: (b, 0, s)),
            pl.BlockSpec((None, ROW_TILE // LANES, WG_KV, LANES), lambda b, s: (b, s, 0, 0)),
            pl.BlockSpec((None, DA_WIDTH, ROW_TILE), lambda b, s: (b, 0, s)),
            pl.BlockSpec((None, ROW_TILE, ka_cols), lambda b, s: (b, s, 0)),
            pl.BlockSpec((None, ROW_TILE, WG_KV), lambda b, s: (b, s, 0)),
            pl.BlockSpec((None, 2 * DA_HEADS, ROW_TILE), lambda b, s: (b, 0, 0)),
        ],
        out_shape=[
            jax.ShapeDtypeStruct((B, DA_WIDTH, S), bf),
            jax.ShapeDtypeStruct((B, ns, DA_WIDTH, ROW_TILE), bf),
            jax.ShapeDtypeStruct((B, WG_Q, S), bf),
            jax.ShapeDtypeStruct((B, S // LANES, WG_KV, LANES), bf),
            jax.ShapeDtypeStruct((B, DA_WIDTH, S), bf),
            jax.ShapeDtypeStruct((B, S, ka_cols), bf),
            jax.ShapeDtypeStruct((B, S, WG_KV), bf),
            jax.ShapeDtypeStruct((B, 2 * DA_HEADS, ROW_TILE), jnp.float32),
        ],
        compiler_params=pltpu.CompilerParams(
            dimension_semantics=("arbitrary", "arbitrary"), vmem_limit_bytes=VMEM_LIMIT_BYTES),
        name="in_proj",
    )(x, w_t, w_ka, w_kb)


def _split3(x):
    hi = x.astype(jnp.bfloat16).astype(jnp.float32)
    r1 = x - hi
    mid = r1.astype(jnp.bfloat16).astype(jnp.float32)
    return hi, mid, r1 - mid


def _diff_attn_kernel(slope_ref, islope_ref, q_ref, kt_ref, kn_ref, k_ref, v_ref,
                      lq1_ref, lk1_ref, lq2_ref, lk2_ref, g_ref,
                      o_ref, bias_ref, qaug_ref, ea_ref, eb_ref, acc_ref,
                      *, n_k, lam_init):
    tq, tk, dh = DA_TQ, DA_TK, DA_HEAD_DIM
    h = pl.program_id(1)
    qi = pl.program_id(2)
    sig = slope_ref[h]

    def k_tile(kt):
        return k_ref[pl.ds(pl.multiple_of(kt * tk, tk), tk), :]

    def col_sum8(e):
        return jnp.sum(e.reshape(tk // SUBLANES, SUBLANES, e.shape[1]), axis=0)

    def both_maps(m0, m1):
        z = jnp.zeros_like(m0)
        return jnp.concatenate([jnp.concatenate([m0, z], axis=1), jnp.concatenate([z, m1], axis=1)], axis=0)

    @pl.when(qi == 0)
    def _():
        dd = (lax.broadcasted_iota(jnp.int32, (tk, tq), 1) - lax.broadcasted_iota(jnp.int32, (tk, tq), 0))
        b = -sig * jnp.abs(dd).astype(jnp.float32)
        bias_ref[...] = jnp.concatenate([b, b], axis=1)

    q = q_ref[...]
    qc = (q[:dh], q[dh:])
    zpad = jnp.zeros((DA_KCOLS - dh, tq), q.dtype)
    qpad = both_maps(jnp.concatenate([qc[0], zpad], axis=0),
                     jnp.concatenate([zpad, qc[1]], axis=0))

    q_start = qi * tq
    kt_ov = q_start // tk

    qk_self = q.astype(jnp.float32) * kt_ref[...].astype(jnp.float32)
    r = jnp.concatenate([jnp.sum(qk_self[:dh], axis=0, keepdims=True),
                         jnp.sum(qk_self[dh:], axis=0, keepdims=True)], axis=1)

    k_norm = jnp.sqrt(jnp.max(kn_ref[...], axis=1, keepdims=True)) * 1.01
    q_sq = q.astype(jnp.float32) ** 2
    bound = jnp.concatenate([jnp.sqrt(jnp.sum(q_sq[:dh], axis=0, keepdims=True)) * k_norm[0:1],
                             jnp.sqrt(jnp.sum(q_sq[dh:], axis=0, keepdims=True)) * k_norm[1:2]], axis=1)
    gap = jnp.max(bound - r, axis=1, keepdims=True)
    reach = jnp.minimum((gap - SKIP_LOG2) * islope_ref[h], float(2 * n_k * tk))
    qs_f = q_start.astype(jnp.float32)
    lo_f = jnp.floor((qs_f - reach - 1.0) * (1.0 / tk))
    hi_f = jnp.floor((qs_f + float(tq) + reach) * (1.0 / tk))
    lo_t = jnp.minimum(jnp.clip(lo_f, 0.0, float(n_k)).astype(jnp.int32)[0, 0], kt_ov)
    hi_t = jnp.maximum(jnp.clip(hi_f, -1.0, float(n_k - 1)).astype(jnp.int32)[0, 0], kt_ov)
    n_left = kt_ov - lo_t
    n_tot = n_left + hi_t - kt_ov

    ipos = (q_start + lax.broadcasted_iota(jnp.int32, (1, tq), 1)).astype(jnp.float32)
    row = lax.broadcasted_iota(jnp.int32, (AUG_ROWS, tq), 0)
    sig_row = jnp.full((1, tq), sig, jnp.float32)
    zrest = jnp.zeros((DA_KCOLS - dh - AUG_ROWS, tq), q.dtype)
    for side, sgn in enumerate((1.0, -1.0, 0.0)):
        s3 = _split3(sgn * sig_row)
        maps = []
        for c in range(2):
            r3 = _split3(-(r[:, c * tq:(c + 1) * tq] + sgn * sig * ipos))
            slab = jnp.zeros((AUG_ROWS, tq), jnp.float32)
            for i, piece in enumerate(r3 + s3 + s3):
                slab = jnp.where(row == i, piece, slab)
            parts = [qc[c], slab.astype(q.dtype), zrest] if c == 0 else [slab.astype(q.dtype), zrest, qc[c]]
            maps.append(jnp.concatenate(parts, axis=0))
        qaug_ref[side] = both_maps(*maps)

    def tile_of(i):
        i = jnp.clip(i, 0, jnp.maximum(n_tot - 1, 0))
        kt = jnp.where(i < n_left, lo_t + i, kt_ov + 1 + i - n_left)
        return jnp.minimum(kt, n_k - 1), (i >= n_left).astype(jnp.int32)

    def scores(i):
        kt, side = tile_of(i)
        return _dot(k_tile(kt), qaug_ref[side])

    def exp_scores(i, e_ref):
        e = jnp.exp2(scores(i))
        e_ref[...] = e.astype(e_ref.dtype)
        return col_sum8(e)

    def add_av(i, e_ref):
        kt, _ = tile_of(i)
        acc_ref[...] += _dot(v_ref[kt], e_ref[...])

    e = jnp.exp2(_dot(k_tile(kt_ov), qaug_ref[2]) + bias_ref[...])
    pend = exp_scores(0, ea_ref)
    l8 = col_sum8(e)
    acc_ref[...] = _dot(v_ref[kt_ov], e.astype(jnp.bfloat16))

    def pair(p, carry):
        l8, pend = carry
        i = 2 * p
        pend_b = exp_scores(i + 1, eb_ref)
        add_av(i, ea_ref)
        pend_a = exp_scores(i + 2, ea_ref)
        add_av(i + 1, eb_ref)
        return l8 + pend + pend_b, pend_a

    n_pairs = n_tot // 2
    carry = lax.fori_loop(0, n_pairs // 2, lambda t, c: pair(2 * t + 1, pair(2 * t, c)), (l8, pend))
    l8, pend = lax.cond(n_pairs % 2 == 1, lambda c: pair(n_pairs - 1, c), lambda c: c, carry)

    def last_tile():
        add_av(n_tot - 1, ea_ref)
        return l8 + pend

    l8 = lax.cond(n_tot % 2 == 1, last_tile, lambda: l8)
    l = jnp.sum(l8, axis=0, keepdims=True)

    lam = (jnp.exp(jnp.sum(lq1_ref[...] * lk1_ref[...], axis=1, keepdims=True))
           - jnp.exp(jnp.sum(lq2_ref[...] * lk2_ref[...], axis=1, keepdims=True)) + lam_init)

    def finish(l):
        on = acc_ref[...] / l
        o = on[:, :tq] - lam * on[:, tq:]
        o = o * lax.rsqrt(jnp.mean(o * o, axis=0, keepdims=True) + LN_EPS)
        o = o * g_ref[...] * (1.0 - lam_init)
        o_ref[...] = o.T.astype(o_ref.dtype)

    finish(l)

    chk = jnp.sum(acc_ref[...] * 0.0) + jnp.sum(l * 0.0)
    overflowed = jnp.logical_not(chk == 0.0)

    def exact_path():
        dd = (lax.broadcasted_iota(jnp.int32, (tk, tq), 1)
              - lax.broadcasted_iota(jnp.int32, (tk, tq), 0)).astype(jnp.float32)
        acc_ref[...] = jnp.zeros_like(acc_ref)

        def body(kt, carry):
            m, lc = carry
            b = -sig * jnp.abs(dd + (q_start - kt * tk).astype(jnp.float32))
            u = _dot(k_tile(kt), qpad) + jnp.concatenate([b, b], axis=1)
            m_new = jnp.maximum(m, jnp.max(u, axis=0, keepdims=True))
            e = jnp.exp2(u - m_new)
            alpha = jnp.exp2(m - m_new)
            acc_ref[...] = alpha * acc_ref[...] + _dot(v_ref[kt], e.astype(jnp.bfloat16))
            return m_new, alpha * lc + jnp.sum(e, axis=0, keepdims=True)

        init = (jnp.full((1, 2 * tq), NEG_BIG, jnp.float32), jnp.zeros((1, 2 * tq), jnp.float32))
        finish(lax.fori_loop(0, n_k, body, init)[1])

    pl.when(overflowed)(exact_path)


def _diff_attention(q_t, k_t, k_nsq, k, v_t, lam_q1, lam_k1, lam_q2, lam_k2, subln_g, lam_init):
    B, _, S = q_t.shape
    assert DA_TQ == DA_TK, "the diagonal-tile bias assumes square tiles"
    hw = 2 * DA_HEAD_DIM
    n_k = S // DA_TK
    slopes2 = _alibi_slopes(DA_HEADS) * LOG2E
    vec = lambda a: a.reshape(1, DA_HEAD_DIM)
    small = pl.BlockSpec((1, DA_HEAD_DIM), lambda b, h, i, s1, s2: (0, 0))
    grid_spec = pltpu.PrefetchScalarGridSpec(
        num_scalar_prefetch=2,
        grid=(B, DA_HEADS, S // DA_TQ),
        in_specs=[
            pl.BlockSpec((None, hw, DA_TQ), lambda b, h, i, s1, s2: (b, h, i)),
            pl.BlockSpec((None, hw, DA_TQ), lambda b, h, i, s1, s2: (b, h, i)),
            pl.BlockSpec((None, None, 2, ROW_TILE), lambda b, h, i, s1, s2: (b, h, 0, 0)),
            pl.BlockSpec((None, S, 2 * DA_KCOLS), lambda b, h, i, s1, s2: (b, 0, h)),
            pl.BlockSpec((None, n_k, hw, DA_TK), lambda b, h, i, s1, s2: (b, 0, h, 0)),
            small, small, small, small,
            pl.BlockSpec((hw, 1), lambda b, h, i, s1, s2: (0, 0)),
        ],
        out_specs=pl.BlockSpec((None, DA_TQ, hw), lambda b, h, i, s1, s2: (b, i, h)),
        scratch_shapes=[
            pltpu.VMEM((DA_TK, 2 * DA_TQ), jnp.float32),
            pltpu.VMEM((3, 2 * DA_KCOLS, 2 * DA_TQ), jnp.bfloat16),
            pltpu.VMEM((DA_TK, 2 * DA_TQ), jnp.bfloat16),
            pltpu.VMEM((DA_TK, 2 * DA_TQ), jnp.bfloat16),
            pltpu.VMEM((hw, 2 * DA_TQ), jnp.float32),
        ],
    )
    return pl.pallas_call(
        functools.partial(_diff_attn_kernel, n_k=n_k, lam_init=lam_init),
        grid_spec=grid_spec,
        out_shape=jax.ShapeDtypeStruct((B, S, DA_WIDTH), jnp.bfloat16),
        compiler_params=pltpu.CompilerParams(
            dimension_semantics=("arbitrary", "arbitrary", "arbitrary"),
            vmem_limit_bytes=VMEM_LIMIT_BYTES),
        name="diff_attn",
    )(jnp.asarray(slopes2, jnp.float32), jnp.asarray(1.0 / slopes2, jnp.float32),
      q_t, k_t, k_nsq.reshape(B, DA_HEADS, 2, ROW_TILE), k, v_t,
      vec(lam_q1), vec(lam_k1), vec(lam_q2), vec(lam_k2), subln_g.reshape(hw, 1))


def _win_attn_kernel(slope_ref, sink_ref, q_ref, k_ref, v_ref, o_ref, ot_ref, *, seq_len):
    tq, band = WG_TQ, WG_BAND
    n_vt = band // LANES
    rep = WG_HEADS // WG_KV_HEADS
    qi = pl.program_id(1)
    q_start = qi * tq
    tile0 = jnp.clip(q_start // LANES - WINDOW // LANES, 0, seq_len // LANES - n_vt)
    k_start = pl.multiple_of(tile0 * LANES, LANES)

    kband = k_ref[pl.ds(k_start, band), :]
    kpos = k_start + lax.broadcasted_iota(jnp.int32, (band, tq), 0)
    qpos = q_start + lax.broadcasted_iota(jnp.int32, (band, tq), 1)
    dist_i = jnp.abs(qpos - kpos)
    dist = jnp.where(dist_i <= WINDOW, dist_i.astype(jnp.float32), -NEG_BIG)

    def group(g, shift_by_max):
        rows = slice(g * WG_HEAD_DIM, (g + 1) * WG_HEAD_DIM)
        blocks = []
        for r in range(rep):
            hd = g * rep + r
            parts = []
            if g > 0:
                parts.append(jnp.zeros((g * WG_HEAD_DIM, tq), q_ref.dtype))
            parts.append(q_ref[hd * WG_HEAD_DIM:(hd + 1) * WG_HEAD_DIM, :])
            if g < WG_KV_HEADS - 1:
                parts.append(jnp.zeros(((WG_KV_HEADS - 1 - g) * WG_HEAD_DIM, tq), q_ref.dtype))
            blocks.append(jnp.concatenate(parts, axis=0))
        s = _dot(kband, jnp.concatenate(blocks, axis=1))
        es, denoms = [], []
        for r in range(rep):
            hd = g * rep + r
            sr = s[:, r * tq:(r + 1) * tq] - slope_ref[hd] * dist
            sk = jnp.full((1, tq), sink_ref[hd] * LOG2E, jnp.float32)
            if shift_by_max:
                m = jnp.maximum(jnp.max(sr, axis=0, keepdims=True), sk)
                sr, sk = sr - m, sk - m
            e = jnp.exp2(sr)
            denoms.append(jnp.sum(e, axis=0, keepdims=True) + jnp.exp2(sk))
            es.append(e.astype(jnp.bfloat16))
        vg = jnp.concatenate([v_ref[tile0 + t, rows, :] for t in range(n_vt)], axis=1)
        den = jnp.concatenate(denoms, axis=1)
        og = _dot(vg, jnp.concatenate(es, axis=1)) / den
        for r in range(rep):
            hd = g * rep + r
            ot_ref[hd * WG_HEAD_DIM:(hd + 1) * WG_HEAD_DIM, :] = og[:, r * tq:(r + 1) * tq]
        return jnp.sum(og * 0.0) + jnp.sum(jnp.where(den >= WG_MIN_DENOM, 0.0, 1.0))

    bad = group(0, False)
    for g in range(1, WG_KV_HEADS):
        bad = bad + group(g, False)

    @pl.when(jnp.logical_not(bad == 0.0))
    def _():
        for g in range(WG_KV_HEADS):
            group(g, True)

    o_ref[...] = ot_ref[...].T.astype(o_ref.dtype)


def _window_attention(q_t, k, v_t, sink_logit):
    B, _, S = q_t.shape
    slopes2 = jnp.asarray(_alibi_slopes(WG_HEADS) * LOG2E, jnp.float32)
    grid_spec = pltpu.PrefetchScalarGridSpec(
        num_scalar_prefetch=2,
        grid=(B, S // WG_TQ),
        in_specs=[
            pl.BlockSpec((None, WG_Q, WG_TQ), lambda b, i, s1, s2: (b, 0, i)),
            pl.BlockSpec((None, S, WG_KV), lambda b, i, s1, s2: (b, 0, 0)),
            pl.BlockSpec((None, S // LANES, WG_KV, LANES), lambda b, i, s1, s2: (b, 0, 0, 0)),
        ],
        out_specs=pl.BlockSpec((None, WG_TQ, WG_Q), lambda b, i, s1, s2: (b, i, 0)),
        scratch_shapes=[pltpu.VMEM((WG_Q, WG_TQ), jnp.float32)],
    )
    return pl.pallas_call(
        functools.partial(_win_attn_kernel, seq_len=S),
        grid_spec=grid_spec,
        out_shape=jax.ShapeDtypeStruct((B, S, WG_Q), jnp.bfloat16),
        compiler_params=pltpu.CompilerParams(
            dimension_semantics=("arbitrary", "arbitrary"), vmem_limit_bytes=VMEM_LIMIT_BYTES),
        name="win_attn",
    )(slopes2, sink_logit.astype(jnp.float32), q_t, k, v_t)


def _layer_norm(x, g, b):
    mu = jnp.mean(x, axis=-1, keepdims=True)
    xc = x - mu
    var = jnp.mean(xc * xc, axis=-1, keepdims=True)
    return xc * lax.rsqrt(var + LN_EPS) * g + b


def _merge_kernel(x_ref, oa_ref, ob_ref, wg_ref, bg_ref, wa_ref, wb_ref, wo_ref, g_ref, b_ref, y_ref):
    x = x_ref[...]
    gates = jax.nn.sigmoid(_dot(x.astype(jnp.bfloat16), wg_ref[...]) + bg_ref[...])
    merged = (gates[:, :D_MODEL] * _dot(oa_ref[...], wa_ref[...])
              + gates[:, D_MODEL:] * _dot(ob_ref[...], wb_ref[...]))
    mix = _dot(merged.astype(jnp.bfloat16), wo_ref[...])
    y_ref[...] = _layer_norm(DEEPNORM_ALPHA * x + mix, g_ref[...], b_ref[...])


def _merge(x, o_a, o_b, w_gate, b_gate, w_br_a, w_br_b, w_out, ln_g, ln_b):
    B, S, D = x.shape
    tok = lambda b, s: (b, s, 0)
    const = lambda b, s: (0, 0)
    return pl.pallas_call(
        _merge_kernel,
        grid=(B, S // ROW_TILE),
        in_specs=[
            pl.BlockSpec((None, ROW_TILE, D), tok),
            pl.BlockSpec((None, ROW_TILE, DA_WIDTH), tok),
            pl.BlockSpec((None, ROW_TILE, WG_Q), tok),
            pl.BlockSpec((D, 2 * D), const),
            pl.BlockSpec((1, 2 * D), const),
            pl.BlockSpec((DA_WIDTH, D), const),
            pl.BlockSpec((WG_Q, D), const),
            pl.BlockSpec((D, D), const),
            pl.BlockSpec((1, D), const),
            pl.BlockSpec((1, D), const),
        ],
        out_specs=pl.BlockSpec((None, ROW_TILE, D), tok),
        out_shape=jax.ShapeDtypeStruct((B, S, D), jnp.float32),
        compiler_params=pltpu.CompilerParams(
            dimension_semantics=("arbitrary", "arbitrary"), vmem_limit_bytes=VMEM_LIMIT_BYTES),
        name="merge_ln1",
    )(x, o_a, o_b, w_gate, b_gate, w_br_a, w_br_b, w_out, ln_g, ln_b)


def _ffn_kernel(x_ref, w1_ref, b1_ref, w2_ref, b2_ref, g_ref, b_ref, y_ref):
    x = x_ref[...]
    h = jnp.maximum(_dot(x.astype(jnp.bfloat16), w1_ref[...]) + b1_ref[...], 0.0)
    f = _dot((h * h).astype(jnp.bfloat16), w2_ref[...]) + b2_ref[...]
    y_ref[...] = _layer_norm(DEEPNORM_ALPHA * x + f, g_ref[...], b_ref[...])


def _ffn(x, w1, b1, w2, b2, ln_g, ln_b):
    B, S, D = x.shape
    tok = lambda b, s: (b, s, 0)
    const = lambda b, s: (0, 0)
    return pl.pallas_call(
        _ffn_kernel,
        grid=(B, S // ROW_TILE),
        in_specs=[
            pl.BlockSpec((None, ROW_TILE, D), tok),
            pl.BlockSpec((D, D_FF), const),
            pl.BlockSpec((1, D_FF), const),
            pl.BlockSpec((D_FF, D), const),
            pl.BlockSpec((1, D), const),
            pl.BlockSpec((1, D), const),
            pl.BlockSpec((1, D), const),
        ],
        out_specs=pl.BlockSpec((None, ROW_TILE, D), tok),
        out_shape=jax.ShapeDtypeStruct((B, S, D), jnp.float32),
        compiler_params=pltpu.CompilerParams(
            dimension_semantics=("arbitrary", "arbitrary"), vmem_limit_bytes=VMEM_LIMIT_BYTES),
        name="ffn_ln2",
    )(x, w1, b1, w2, b2, ln_g, ln_b)


def _encoder_layer(x, l, w_in, b_gate, lam_q1, lam_k1, lam_q2, lam_k2, subln_g, sink_logit,
                   w_br_a, w_br_b, w_out, ln1_g, ln1_b, w_ff1, b_ff1, w_ff2, b_ff2, ln2_g, ln2_b):
    bf = jnp.bfloat16
    row = lambda a: a.reshape(1, -1)
    w_t = jnp.concatenate([w_in[:, OFF_DA_Q:OFF_DA_K], w_in[:, OFF_DA_V:OFF_WG_Q],
                           w_in[:, OFF_WG_Q:OFF_WG_K], w_in[:, OFF_WG_V:OFF_GATE],
                           w_in[:, OFF_DA_K:OFF_DA_V]], axis=1).T.astype(bf)
    w_ka = w_in[:, OFF_DA_K:OFF_DA_V].astype(bf)
    w_kb = w_in[:, OFF_WG_K:OFF_WG_V].astype(bf)
    w_gate = w_in[:, OFF_GATE:].astype(bf)
    lam_init = 0.8 - 0.6 * math.exp(-0.3 * l)

    qa_t, va_t, qb_t, vb_t, ka_t, k_a, k_b, ka_nsq = _project(x, w_t, w_ka, w_kb)
    o_a = _diff_attention(qa_t, ka_t, ka_nsq, k_a, va_t, lam_q1, lam_k1, lam_q2, lam_k2, subln_g, lam_init)
    o_b = _window_attention(qb_t, k_b, vb_t, sink_logit)
    x1 = _merge(x, o_a, o_b, w_gate, row(b_gate), w_br_a.astype(bf), w_br_b.astype(bf),
                w_out.astype(bf), row(ln1_g), row(ln1_b))
    return _ffn(x1, w_ff1.astype(bf), row(b_ff1), w_ff2.astype(bf), row(b_ff2), row(ln2_g), row(ln2_b))


def kernel(x_prompt, x_sample, w_in, b_gate, lam_q1, lam_k1, lam_q2, lam_k2, subln_g, sink_logit,
           w_br_a, w_br_b, w_out, ln1_g, ln1_b, w_ff1, b_ff1, w_ff2, b_ff2, ln2_g, ln2_b):
    def run(x):
        for l in range(DEPTH):
            x = _encoder_layer(x, l, w_in[l], b_gate[l], lam_q1[l], lam_k1[l], lam_q2[l], lam_k2[l],
                               subln_g[l], sink_logit[l], w_br_a[l], w_br_b[l], w_out[l],
                               ln1_g[l], ln1_b[l], w_ff1[l], b_ff1[l], w_ff2[l], b_ff2[l],
                               ln2_g[l], ln2_b[l])
        return x

    return (run(x_prompt), run(x_sample))
```

```python
import functools
import math

import jax
import jax.numpy as jnp
import numpy as np
from jax import lax
from jax.experimental import pallas as pl
from jax.experimental.pallas import tpu as pltpu

D_MODEL = 1024
DA_HEADS = 8
DA_HEAD_DIM = 64
DA_WIDTH = DA_HEADS * 2 * DA_HEAD_DIM
WG_HEADS = 16
WG_KV_HEADS = 4
WG_HEAD_DIM = 64
WG_Q = WG_HEADS * WG_HEAD_DIM
WG_KV = WG_KV_HEADS * WG_HEAD_DIM
WINDOW = 128
D_FF = 4 * D_MODEL
DEPTH = 1
DEEPNORM_ALPHA = (2.0 * DEPTH) ** 0.25
LN_EPS = 1e-5
LOG2E = math.log2(math.e)
NEG_BIG = -1e30

OFF_DA_Q = 0
OFF_DA_K = OFF_DA_Q + DA_WIDTH
OFF_DA_V = OFF_DA_K + DA_WIDTH
OFF_WG_Q = OFF_DA_V + DA_WIDTH
OFF_WG_K = OFF_WG_Q + WG_Q
OFF_WG_V = OFF_WG_K + WG_KV
OFF_GATE = OFF_WG_V + WG_KV

LANES = 128
SUBLANES = 8
BF16_ROWS = 16
VMEM_LIMIT_BYTES = 56 * 1024 * 1024

ROW_TILE = 512
DA_TQ = 512
DA_TK = ROW_TILE
WG_TQ = 256
WG_BAND = WG_TQ + 2 * WINDOW
WG_MIN_DENOM = 2.0 ** -60

DA_KCOLS = 2 * DA_HEAD_DIM
AUG_ROWS = BF16_ROWS
SKIP_LOG2 = -150.0


def _alibi_slopes(n_heads):
    return 2.0 ** (-8.0 * np.arange(1, n_heads + 1) / n_heads)


def _nt_dot(a, b):
    return lax.dot_general(a, b, (((1,), (1,)), ((), ())), preferred_element_type=jnp.float32)


def _dot(a, b):
    return jnp.dot(a, b, preferred_element_type=jnp.float32)


def _proj_kernel(x_ref, wt_ref, wka_ref, wkb_ref,
                 qa_ref, va_ref, qb_ref, vb_ref, kat_ref, ka_ref, kb_ref, knsq_ref):
    xb = x_ref[...].astype(jnp.bfloat16)
    da_scale = DA_HEAD_DIM ** -0.5 * LOG2E
    wg_scale = WG_HEAD_DIM ** -0.5 * LOG2E
    r0, r1, r2, r3 = DA_WIDTH, 2 * DA_WIDTH, 2 * DA_WIDTH + WG_Q, 2 * DA_WIDTH + WG_Q + WG_KV
    qa_ref[...] = (_nt_dot(wt_ref[0:r0, :], xb) * da_scale).astype(jnp.bfloat16)
    va_ref[...] = _nt_dot(wt_ref[r0:r1, :], xb).astype(jnp.bfloat16)
    qb_ref[...] = (_nt_dot(wt_ref[r1:r2, :], xb) * wg_scale).astype(jnp.bfloat16)
    kat = _nt_dot(wt_ref[r3:, :], xb)
    kat_ref[...] = kat.astype(jnp.bfloat16)
    nsq = jnp.sum((kat * kat).reshape(2 * DA_HEADS, DA_HEAD_DIM, ROW_TILE), axis=1)
    @pl.when(pl.program_id(1) == 0)
    def _():
        knsq_ref[...] = nsq

    @pl.when(pl.program_id(1) > 0)
    def _():
        knsq_ref[...] = jnp.maximum(knsq_ref[...], nsq)
    vb = _nt_dot(wt_ref[r2:r3, :], xb).astype(jnp.bfloat16)
    for t in range(ROW_TILE // LANES):
        vb_ref[t] = vb[:, t * LANES:(t + 1) * LANES]
    kb_ref[...] = _dot(xb, wkb_ref[...]).astype(jnp.bfloat16)
    pos = pl.program_id(1) * ROW_TILE + lax.broadcasted_iota(jnp.int32, (ROW_TILE, DA_KCOLS), 0)
    lane = lax.broadcasted_iota(jnp.int32, (ROW_TILE, DA_KCOLS), 1)
    pos_hi = ((pos // LANES) * LANES).astype(jnp.float32)
    pos_lo = (pos % LANES).astype(jnp.float32)
    lower = lane < DA_HEAD_DIM

    def position_cols(c0):
        return jnp.where(lane < c0, 0.0,
                         jnp.where(lane < c0 + 3, 1.0,
                                   jnp.where(lane < c0 + 6, pos_hi,
                                             jnp.where(lane < c0 + 9, pos_lo, 0.0))))
    pat0, pat1 = position_cols(DA_HEAD_DIM), position_cols(0)
    for hd in range(DA_HEADS):
        kk = _dot(xb, wka_ref[:, hd * DA_KCOLS:(hd + 1) * DA_KCOLS])
        ka_ref[:, (2 * hd) * DA_KCOLS:(2 * hd + 1) * DA_KCOLS] = jnp.where(lower, kk, pat0).astype(jnp.bfloat16)
        ka_ref[:, (2 * hd + 1) * DA_KCOLS:(2 * hd + 2) * DA_KCOLS] = jnp.where(lower, pat1, kk).astype(jnp.bfloat16)


def _project(x, w_t, w_ka, w_kb):
    B, S, D = x.shape
    ns = S // ROW_TILE
    rows_t = w_t.shape[0]
    ka_cols = 2 * DA_HEADS * DA_KCOLS
    bf = jnp.bfloat16
    return pl.pallas_call(
        _proj_kernel,
        grid=(B, ns),
        in_specs=[
            pl.BlockSpec((None, ROW_TILE, D), lambda b, s: (b, s, 0)),
            pl.BlockSpec((rows_t, D), lambda b, s: (0, 0)),
            pl.BlockSpec((D, DA_WIDTH), lambda b, s: (0, 0)),
            pl.BlockSpec((D, WG_KV), lambda b, s: (0, 0)),
        ],
        out_specs=[
            pl.BlockSpec((None, DA_WIDTH, ROW_TILE), lambda b, s: (b, 0, s)),
            pl.BlockSpec((None, None, DA_WIDTH, ROW_TILE), lambda b, s: (b, s, 0, 0)),
            pl.BlockSpec((None, WG_Q, ROW_TILE), lambda b, s: (b, 0, s)),
            pl.BlockSpec((None, ROW_TILE // LANES, WG_KV, LANES), lambda b, s: (b, s, 0, 0)),
            pl.BlockSpec((None, DA_WIDTH, ROW_TILE), lambda b, s: (b, 0, s)),
            pl.BlockSpec((None, ROW_TILE, ka_cols), lambda b, s: (b, s, 0)),
            pl.BlockSpec((None, ROW_TILE, WG_KV), lambda b, s: (b, s, 0)),
            pl.BlockSpec((None, 2 * DA_HEADS, ROW_TILE), lambda b, s: (b, 0, 0)),
        ],
        out_shape=[
            jax.ShapeDtypeStruct((B, DA_WIDTH, S), bf),
            jax.ShapeDtypeStruct((B, ns, DA_WIDTH, ROW_TILE), bf),
            jax.ShapeDtypeStruct((B, WG_Q, S), bf),
            jax.ShapeDtypeStruct((B, S // LANES, WG_KV, LANES), bf),
            jax.ShapeDtypeStruct((B, DA_WIDTH, S), bf),
            jax.ShapeDtypeStruct((B, S, ka_cols), bf),
            jax.ShapeDtypeStruct((B, S, WG_KV), bf),
            jax.ShapeDtypeStruct((B, 2 * DA_HEADS, ROW_TILE), jnp.float32),
        ],
        compiler_params=pltpu.CompilerParams(
            dimension_semantics=("arbitrary", "arbitrary"), vmem_limit_bytes=VMEM_LIMIT_BYTES),
        name="in_proj",
    )(x, w_t, w_ka, w_kb)


def _split3(x):
    hi = x.astype(jnp.bfloat16).astype(jnp.float32)
    r1 = x - hi
    mid = r1.astype(jnp.bfloat16).astype(jnp.float32)
    return hi, mid, r1 - mid


def _diff_attn_kernel(slope_ref, islope_ref, q_ref, kt_ref, kn_ref, k_ref, v_ref,
                      lq1_ref, lk1_ref, lq2_ref, lk2_ref, g_ref,
                      o_ref, bias_ref, qaug_ref, ea_ref, eb_ref, acc_ref,
                      *, n_k, lam_init):
    tq, tk, dh = DA_TQ, DA_TK, DA_HEAD_DIM
    h = pl.program_id(1)
    qi = pl.program_id(2)
    sig = slope_ref[h]

    def k_tile(kt):
        return k_ref[pl.ds(pl.multiple_of(kt * tk, tk), tk), :]

    def col_sum8(e):
        return jnp.sum(e.reshape(tk // SUBLANES, SUBLANES, e.shape[1]), axis=0)

    def both_maps(m0, m1):
        z = jnp.zeros_like(m0)
        return jnp.concatenate([jnp.concatenate([m0, z], axis=1), jnp.concatenate([z, m1], axis=1)], axis=0)

    @pl.when(qi == 0)
    def _():
        dd = (lax.broadcasted_iota(jnp.int32, (tk, tq), 1) - lax.broadcasted_iota(jnp.int32, (tk, tq), 0))
        b = -sig * jnp.abs(dd).astype(jnp.float32)
        bias_ref[...] = jnp.concatenate([b, b], axis=1)

    q = q_ref[...]
    qc = (q[:dh], q[dh:])
    zpad = jnp.zeros((DA_KCOLS - dh, tq), q.dtype)
    qpad = both_maps(jnp.concatenate([qc[0], zpad], axis=0),
                     jnp.concatenate([zpad, qc[1]], axis=0))

    q_start = qi * tq
    kt_ov = q_start // tk

    qk_self = q.astype(jnp.float32) * kt_ref[...].astype(jnp.float32)
    r = jnp.concatenate([jnp.sum(qk_self[:dh], axis=0, keepdims=True),
                         jnp.sum(qk_self[dh:], axis=0, keepdims=True)], axis=1)

    k_norm = jnp.sqrt(jnp.max(kn_ref[...], axis=1, keepdims=True)) * 1.01
    q_sq = q.astype(jnp.float32) ** 2
    bound = jnp.concatenate([jnp.sqrt(jnp.sum(q_sq[:dh], axis=0, keepdims=True)) * k_norm[0:1],
                             jnp.sqrt(jnp.sum(q_sq[dh:], axis=0, keepdims=True)) * k_norm[1:2]], axis=1)
    gap = jnp.max(bound - r, axis=1, keepdims=True)
    reach = jnp.minimum((gap - SKIP_LOG2) * islope_ref[h], float(2 * n_k * tk))
    qs_f = q_start.astype(jnp.float32)
    lo_f = jnp.floor((qs_f - reach - 1.0) * (1.0 / tk))
    hi_f = jnp.floor((qs_f + float(tq) + reach) * (1.0 / tk))
    lo_t = jnp.minimum(jnp.clip(lo_f, 0.0, float(n_k)).astype(jnp.int32)[0, 0], kt_ov)
    hi_t = jnp.maximum(jnp.clip(hi_f, -1.0, float(n_k - 1)).astype(jnp.int32)[0, 0], kt_ov)
    n_left = kt_ov - lo_t
    n_tot = n_left + hi_t - kt_ov

    ipos = (q_start + lax.broadcasted_iota(jnp.int32, (1, tq), 1)).astype(jnp.float32)
    row = lax.broadcasted_iota(jnp.int32, (AUG_ROWS, tq), 0)
    sig_row = jnp.full((1, tq), sig, jnp.float32)
    zrest = jnp.zeros((DA_KCOLS - dh - AUG_ROWS, tq), q.dtype)
    for side, sgn in enumerate((1.0, -1.0, 0.0)):
        s3 = _split3(sgn * sig_row)
        maps = []
        for c in range(2):
            r3 = _split3(-(r[:, c * tq:(c + 1) * tq] + sgn * sig * ipos))
            slab = jnp.zeros((AUG_ROWS, tq), jnp.float32)
            for i, piece in enumerate(r3 + s3 + s3):
                slab = jnp.where(row == i, piece, slab)
            parts = [qc[c], slab.astype(q.dtype), zrest] if c == 0 else [slab.astype(q.dtype), zrest, qc[c]]
            maps.append(jnp.concatenate(parts, axis=0))
        qaug_ref[side] = both_maps(*maps)

    def tile_of(i):
        i = jnp.clip(i, 0, jnp.maximum(n_tot - 1, 0))
        kt = jnp.where(i < n_left, lo_t + i, kt_ov + 1 + i - n_left)
        return jnp.minimum(kt, n_k - 1), (i >= n_left).astype(jnp.int32)

    def scores(i):
        kt, side = tile_of(i)
        return _dot(k_tile(kt), qaug_ref[side])

    def exp_scores(i, e_ref):
        e = jnp.exp2(scores(i))
        e_ref[...] = e.astype(e_ref.dtype)
        return col_sum8(e)

    def add_av(i, e_ref):
        kt, _ = tile_of(i)
        acc_ref[...] += _dot(v_ref[kt], e_ref[...])

    e = jnp.exp2(_dot(k_tile(kt_ov), qaug_ref[2]) + bias_ref[...])
    pend = exp_scores(0, ea_ref)
    l8 = col_sum8(e)
    acc_ref[...] = _dot(v_ref[kt_ov], e.astype(jnp.bfloat16))

    def pair(p, carry):
        l8, pend = carry
        i = 2 * p
        pend_b = exp_scores(i + 1, eb_ref)
        add_av(i, ea_ref)
        pend_a = exp_scores(i + 2, ea_ref)
        add_av(i + 1, eb_ref)
        return l8 + pend + pend_b, pend_a

    ends_even = jnp.logical_and(n_tot > 0, n_tot % 2 == 0)
    n_pairs = n_tot // 2 - ends_even.astype(jnp.int32)
    carry = lax.fori_loop(0, n_pairs // 2, lambda t, c: pair(2 * t + 1, pair(2 * t, c)), (l8, pend))
    l8, pend = lax.cond(n_pairs % 2 == 1, lambda c: pair(n_pairs - 1, c), lambda c: c, carry)

    def last_tile():
        add_av(n_tot - 1, ea_ref)
        return l8 + pend

    def last_two_tiles():
        pend_b = exp_scores(n_tot - 1, eb_ref)
        add_av(n_tot - 2, ea_ref)
        add_av(n_tot - 1, eb_ref)
        return l8 + pend + pend_b

    l8 = lax.cond(n_tot % 2 == 1, last_tile, lambda: l8)
    l8 = lax.cond(ends_even, last_two_tiles, lambda: l8)
    l = jnp.sum(l8, axis=0, keepdims=True)

    lam = (jnp.exp(jnp.sum(lq1_ref[...] * lk1_ref[...], axis=1, keepdims=True))
           - jnp.exp(jnp.sum(lq2_ref[...] * lk2_ref[...], axis=1, keepdims=True)) + lam_init)

    def finish(l):
        on = acc_ref[...] / l
        o = on[:, :tq] - lam * on[:, tq:]
        o = o * lax.rsqrt(jnp.mean(o * o, axis=0, keepdims=True) + LN_EPS)
        o = o * g_ref[...] * (1.0 - lam_init)
        o_ref[...] = o.T.astype(o_ref.dtype)

    finish(l)

    chk = jnp.sum(acc_ref[...] * 0.0) + jnp.sum(l * 0.0)
    overflowed = jnp.logical_not(chk == 0.0)

    def exact_path():
        dd = (lax.broadcasted_iota(jnp.int32, (tk, tq), 1)
              - lax.broadcasted_iota(jnp.int32, (tk, tq), 0)).astype(jnp.float32)
        acc_ref[...] = jnp.zeros_like(acc_ref)

        def body(kt, carry):
            m, lc = carry
            b = -sig * jnp.abs(dd + (q_start - kt * tk).astype(jnp.float32))
            u = _dot(k_tile(kt), qpad) + jnp.concatenate([b, b], axis=1)
            m_new = jnp.maximum(m, jnp.max(u, axis=0, keepdims=True))
            e = jnp.exp2(u - m_new)
            alpha = jnp.exp2(m - m_new)
            acc_ref[...] = alpha * acc_ref[...] + _dot(v_ref[kt], e.astype(jnp.bfloat16))
            return m_new, alpha * lc + jnp.sum(e, axis=0, keepdims=True)

        init = (jnp.full((1, 2 * tq), NEG_BIG, jnp.float32), jnp.zeros((1, 2 * tq), jnp.float32))
        finish(lax.fori_loop(0, n_k, body, init)[1])

    pl.when(overflowed)(exact_path)


def _diff_attention(q_t, k_t, k_nsq, k, v_t, lam_q1, lam_k1, lam_q2, lam_k2, subln_g, lam_init):
    B, _, S = q_t.shape
    assert DA_TQ == DA_TK, "the diagonal-tile bias assumes square tiles"
    hw = 2 * DA_HEAD_DIM
    n_k = S // DA_TK
    slopes2 = _alibi_slopes(DA_HEADS) * LOG2E
    vec = lambda a: a.reshape(1, DA_HEAD_DIM)
    small = pl.BlockSpec((1, DA_HEAD_DIM), lambda b, h, i, s1, s2: (0, 0))
    grid_spec = pltpu.PrefetchScalarGridSpec(
        num_scalar_prefetch=2,
        grid=(B, DA_HEADS, S // DA_TQ),
        in_specs=[
            pl.BlockSpec((None, hw, DA_TQ), lambda b, h, i, s1, s2: (b, h, i)),
            pl.BlockSpec((None, hw, DA_TQ), lambda b, h, i, s1, s2: (b, h, i)),
            pl.BlockSpec((None, None, 2, ROW_TILE), lambda b, h, i, s1, s2: (b, h, 0, 0)),
            pl.BlockSpec((None, S, 2 * DA_KCOLS), lambda b, h, i, s1, s2: (b, 0, h)),
            pl.BlockSpec((None, n_k, hw, DA_TK), lambda b, h, i, s1, s2: (b, 0, h, 0)),
            small, small, small, small,
            pl.BlockSpec((hw, 1), lambda b, h, i, s1, s2: (0, 0)),
        ],
        out_specs=pl.BlockSpec((None, DA_TQ, hw), lambda b, h, i, s1, s2: (b, i, h)),
        scratch_shapes=[
            pltpu.VMEM((DA_TK, 2 * DA_TQ), jnp.float32),
            pltpu.VMEM((3, 2 * DA_KCOLS, 2 * DA_TQ), jnp.bfloat16),
            pltpu.VMEM((DA_TK, 2 * DA_TQ), jnp.bfloat16),
            pltpu.VMEM((DA_TK, 2 * DA_TQ), jnp.bfloat16),
            pltpu.VMEM((hw, 2 * DA_TQ), jnp.float32),
        ],
    )
    return pl.pallas_call(
        functools.partial(_diff_attn_kernel, n_k=n_k, lam_init=lam_init),
        grid_spec=grid_spec,
        out_shape=jax.ShapeDtypeStruct((B, S, DA_WIDTH), jnp.bfloat16),
        compiler_params=pltpu.CompilerParams(
            dimension_semantics=("arbitrary", "arbitrary", "arbitrary"),
            vmem_limit_bytes=VMEM_LIMIT_BYTES),
        name="diff_attn",
    )(jnp.asarray(slopes2, jnp.float32), jnp.asarray(1.0 / slopes2, jnp.float32),
      q_t, k_t, k_nsq.reshape(B, DA_HEADS, 2, ROW_TILE), k, v_t,
      vec(lam_q1), vec(lam_k1), vec(lam_q2), vec(lam_k2), subln_g.reshape(hw, 1))


def _win_attn_kernel(slope_ref, sink_ref, q_ref, k_ref, v_ref, o_ref, ot_ref, *, seq_len):
    tq, band = WG_TQ, WG_BAND
    n_vt = band // LANES
    rep = WG_HEADS // WG_KV_HEADS
    qi = pl.program_id(1)
    q_start = qi * tq
    tile0 = jnp.clip(q_start // LANES - WINDOW // LANES, 0, seq_len // LANES - n_vt)
    k_start = pl.multiple_of(tile0 * LANES, LANES)

    kband = k_ref[pl.ds(k_start, band), :]
    kpos = k_start + lax.broadcasted_iota(jnp.int32, (band, tq), 0)
    qpos = q_start + lax.broadcasted_iota(jnp.int32, (band, tq), 1)
    dist_i = jnp.abs(qpos - kpos)
    dist = jnp.where(dist_i <= WINDOW, dist_i.astype(jnp.float32), -NEG_BIG)

    def group(g, shift_by_max):
        rows = slice(g * WG_HEAD_DIM, (g + 1) * WG_HEAD_DIM)
        blocks = []
        for r in range(rep):
            hd = g * rep + r
            parts = []
            if g > 0:
                parts.append(jnp.zeros((g * WG_HEAD_DIM, tq), q_ref.dtype))
            parts.append(q_ref[hd * WG_HEAD_DIM:(hd + 1) * WG_HEAD_DIM, :])
            if g < WG_KV_HEADS - 1:
                parts.append(jnp.zeros(((WG_KV_HEADS - 1 - g) * WG_HEAD_DIM, tq), q_ref.dtype))
            blocks.append(jnp.concatenate(parts, axis=0))
        s = _dot(kband, jnp.concatenate(blocks, axis=1))
        es, denoms = [], []
        for r in range(rep):
            hd = g * rep + r
            sr = s[:, r * tq:(r + 1) * tq] - slope_ref[hd] * dist
            sk = jnp.full((1, tq), sink_ref[hd] * LOG2E, jnp.float32)
            if shift_by_max:
                m = jnp.maximum(jnp.max(sr, axis=0, keepdims=True), sk)
                sr, sk = sr - m, sk - m
            e = jnp.exp2(sr)
            denoms.append(jnp.sum(e, axis=0, keepdims=True) + jnp.exp2(sk))
            es.append(e.astype(jnp.bfloat16))
        vg = jnp.concatenate([v_ref[tile0 + t, rows, :] for t in range(n_vt)], axis=1)
        den = jnp.concatenate(denoms, axis=1)
        og = _dot(vg, jnp.concatenate(es, axis=1)) / den
        for r in range(rep):
            hd = g * rep + r
            ot_ref[hd * WG_HEAD_DIM:(hd + 1) * WG_HEAD_DIM, :] = og[:, r * tq:(r + 1) * tq]
        return jnp.sum(og * 0.0) + jnp.sum(jnp.where(den >= WG_MIN_DENOM, 0.0, 1.0))

    bad = group(0, False)
    for g in range(1, WG_KV_HEADS):
        bad = bad + group(g, False)

    @pl.when(jnp.logical_not(bad == 0.0))
    def _():
        for g in range(WG_KV_HEADS):
            group(g, True)

    o_ref[...] = ot_ref[...].T.astype(o_ref.dtype)


def _window_attention(q_t, k, v_t, sink_logit):
    B, _, S = q_t.shape
    slopes2 = jnp.asarray(_alibi_slopes(WG_HEADS) * LOG2E, jnp.float32)
    grid_spec = pltpu.PrefetchScalarGridSpec(
        num_scalar_prefetch=2,
        grid=(B, S // WG_TQ),
        in_specs=[
            pl.BlockSpec((None, WG_Q, WG_TQ), lambda b, i, s1, s2: (b, 0, i)),
            pl.BlockSpec((None, S, WG_KV), lambda b, i, s1, s2: (b, 0, 0)),
            pl.BlockSpec((None, S // LANES, WG_KV, LANES), lambda b, i, s1, s2: (b, 0, 0, 0)),
        ],
        out_specs=pl.BlockSpec((None, WG_TQ, WG_Q), lambda b, i, s1, s2: (b, i, 0)),
        scratch_shapes=[pltpu.VMEM((WG_Q, WG_TQ), jnp.float32)],
    )
    return pl.pallas_call(
        functools.partial(_win_attn_kernel, seq_len=S),
        grid_spec=grid_spec,
        out_shape=jax.ShapeDtypeStruct((B, S, WG_Q), jnp.bfloat16),
        compiler_params=pltpu.CompilerParams(
            dimension_semantics=("arbitrary", "arbitrary"), vmem_limit_bytes=VMEM_LIMIT_BYTES),
        name="win_attn",
    )(slopes2, sink_logit.astype(jnp.float32), q_t, k, v_t)


def _layer_norm(x, g, b):
    mu = jnp.mean(x, axis=-1, keepdims=True)
    xc = x - mu
    var = jnp.mean(xc * xc, axis=-1, keepdims=True)
    return xc * lax.rsqrt(var + LN_EPS) * g + b


def _merge_kernel(x_ref, oa_ref, ob_ref, wg_ref, bg_ref, wa_ref, wb_ref, wo_ref, g_ref, b_ref, y_ref):
    x = x_ref[...]
    gates = jax.nn.sigmoid(_dot(x.astype(jnp.bfloat16), wg_ref[...]) + bg_ref[...])
    merged = (gates[:, :D_MODEL] * _dot(oa_ref[...], wa_ref[...])
              + gates[:, D_MODEL:] * _dot(ob_ref[...], wb_ref[...]))
    mix = _dot(merged.astype(jnp.bfloat16), wo_ref[...])
    y_ref[...] = _layer_norm(DEEPNORM_ALPHA * x + mix, g_ref[...], b_ref[...])


def _merge(x, o_a, o_b, w_gate, b_gate, w_br_a, w_br_b, w_out, ln_g, ln_b):
    B, S, D = x.shape
    tok = lambda b, s: (b, s, 0)
    const = lambda b, s: (0, 0)
    return pl.pallas_call(
        _merge_kernel,
        grid=(B, S // ROW_TILE),
        in_specs=[
            pl.BlockSpec((None, ROW_TILE, D), tok),
            pl.BlockSpec((None, ROW_TILE, DA_WIDTH), tok),
            pl.BlockSpec((None, ROW_TILE, WG_Q), tok),
            pl.BlockSpec((D, 2 * D), const),
            pl.BlockSpec((1, 2 * D), const),
            pl.BlockSpec((DA_WIDTH, D), const),
            pl.BlockSpec((WG_Q, D), const),
            pl.BlockSpec((D, D), const),
            pl.BlockSpec((1, D), const),
            pl.BlockSpec((1, D), const),
        ],
        out_specs=pl.BlockSpec((None, ROW_TILE, D), tok),
        out_shape=jax.ShapeDtypeStruct((B, S, D), jnp.float32),
        compiler_params=pltpu.CompilerParams(
            dimension_semantics=("arbitrary", "arbitrary"), vmem_limit_bytes=VMEM_LIMIT_BYTES),
        name="merge_ln1",
    )(x, o_a, o_b, w_gate, b_gate, w_br_a, w_br_b, w_out, ln_g, ln_b)


def _ffn_kernel(x_ref, w1_ref, b1_ref, w2_ref, b2_ref, g_ref, b_ref, y_ref):
    x = x_ref[...]
    h = jnp.maximum(_dot(x.astype(jnp.bfloat16), w1_ref[...]) + b1_ref[...], 0.0)
    f = _dot((h * h).astype(jnp.bfloat16), w2_ref[...]) + b2_ref[...]
    y_ref[...] = _layer_norm(DEEPNORM_ALPHA * x + f, g_ref[...], b_ref[...])


def _ffn(x, w1, b1, w2, b2, ln_g, ln_b):
    B, S, D = x.shape
    tok = lambda b, s: (b, s, 0)
    const = lambda b, s: (0, 0)
    return pl.pallas_call(
        _ffn_kernel,
        grid=(B, S // ROW_TILE),
        in_specs=[
            pl.BlockSpec((None, ROW_TILE, D), tok),
            pl.BlockSpec((D, D_FF), const),
            pl.BlockSpec((1, D_FF), const),
            pl.BlockSpec((D_FF, D), const),
            pl.BlockSpec((1, D), const),
            pl.BlockSpec((1, D), const),
            pl.BlockSpec((1, D), const),
        ],
        out_specs=pl.BlockSpec((None, ROW_TILE, D), tok),
        out_shape=jax.ShapeDtypeStruct((B, S, D), jnp.float32),
        compiler_params=pltpu.CompilerParams(
            dimension_semantics=("arbitrary", "arbitrary"), vmem_limit_bytes=VMEM_LIMIT_BYTES),
        name="ffn_ln2",
    )(x, w1, b1, w2, b2, ln_g, ln_b)


def _encoder_layer(x, l, w_in, b_gate, lam_q1, lam_k1, lam_q2, lam_k2, subln_g, sink_logit,
                   w_br_a, w_br_b, w_out, ln1_g, ln1_b, w_ff1, b_ff1, w_ff2, b_ff2, ln2_g, ln2_b):
    bf = jnp.bfloat16
    row = lambda a: a.reshape(1, -1)
    w_t = jnp.concatenate([w_in[:, OFF_DA_Q:OFF_DA_K], w_in[:, OFF_DA_V:OFF_WG_Q],
                           w_in[:, OFF_WG_Q:OFF_WG_K], w_in[:, OFF_WG_V:OFF_GATE],
                           w_in[:, OFF_DA_K:OFF_DA_V]], axis=1).T.astype(bf)
    w_ka = w_in[:, OFF_DA_K:OFF_DA_V].astype(bf)
    w_kb = w_in[:, OFF_WG_K:OFF_WG_V].astype(bf)
    w_gate = w_in[:, OFF_GATE:].astype(bf)
    lam_init = 0.8 - 0.6 * math.exp(-0.3 * l)

    qa_t, va_t, qb_t, vb_t, ka_t, k_a, k_b, ka_nsq = _project(x, w_t, w_ka, w_kb)
    o_a = _diff_attention(qa_t, ka_t, ka_nsq, k_a, va_t, lam_q1, lam_k1, lam_q2, lam_k2, subln_g, lam_init)
    o_b = _window_attention(qb_t, k_b, vb_t, sink_logit)
    x1 = _merge(x, o_a, o_b, w_gate, row(b_gate), w_br_a.astype(bf), w_br_b.astype(bf),
                w_out.astype(bf), row(ln1_g), row(ln1_b))
    return _ffn(x1, w_ff1.astype(bf), row(b_ff1), w_ff2.astype(bf), row(b_ff2), row(ln2_g), row(ln2_b))


def kernel(x_prompt, x_sample, w_in, b_gate, lam_q1, lam_k1, lam_q2, lam_k2, subln_g, sink_logit,
           w_br_a, w_br_b, w_out, ln1_g, ln1_b, w_ff1, b_ff1, w_ff2, b_ff2, ln2_g, ln2_b):
    def run(x):
        for l in range(DEPTH):
            x = _encoder_layer(x, l, w_in[l], b_gate[l], lam_q1[l], lam_k1[l], lam_q2[l], lam_k2[l],
                               subln_g[l], sink_logit[l], w_br_a[l], w_br_b[l], w_out[l],
                               ln1_g[l], ln1_b[l], w_ff1[l], b_ff1[l], w_ff2[l], b_ff2[l],
                               ln2_g[l], ln2_b[l])
        return x

    return (run(x_prompt), run(x_sample))
```

```python
import functools
import math

import jax
import jax.numpy as jnp
import numpy as np
from jax import lax
from jax.experimental import pallas as pl
from jax.experimental.pallas import tpu as pltpu

D_MODEL = 1024
DA_HEADS = 8
DA_HEAD_DIM = 64
DA_WIDTH = DA_HEADS * 2 * DA_HEAD_DIM
WG_HEADS = 16
WG_KV_HEADS = 4
WG_HEAD_DIM = 64
WG_Q = WG_HEADS * WG_HEAD_DIM
WG_KV = WG_KV_HEADS * WG_HEAD_DIM
WINDOW = 128
D_FF = 4 * D_MODEL
DEPTH = 1
DEEPNORM_ALPHA = (2.0 * DEPTH) ** 0.25
LN_EPS = 1e-5
LOG2E = math.log2(math.e)
NEG_BIG = -1e30

OFF_DA_Q = 0
OFF_DA_K = OFF_DA_Q + DA_WIDTH
OFF_DA_V = OFF_DA_K + DA_WIDTH
OFF_WG_Q = OFF_DA_V + DA_WIDTH
OFF_WG_K = OFF_WG_Q + WG_Q
OFF_WG_V = OFF_WG_K + WG_KV
OFF_GATE = OFF_WG_V + WG_KV

LANES = 128
SUBLANES = 8
BF16_ROWS = 16
VMEM_LIMIT_BYTES = 56 * 1024 * 1024

ROW_TILE = 512
DA_TQ = 512
DA_TK = ROW_TILE
WG_TQ = 256
WG_BAND = WG_TQ + 2 * WINDOW
WG_MIN_DENOM = 2.0 ** -60

DA_KCOLS = 2 * DA_HEAD_DIM
AUG_ROWS = BF16_ROWS
SKIP_LOG2 = -150.0


def _alibi_slopes(n_heads):
    return 2.0 ** (-8.0 * np.arange(1, n_heads + 1) / n_heads)


def _nt_dot(a, b):
    return lax.dot_general(a, b, (((1,), (1,)), ((), ())), preferred_element_type=jnp.float32)


def _dot(a, b):
    return jnp.dot(a, b, preferred_element_type=jnp.float32)


def _proj_kernel(x_ref, wt_ref, wka_ref, wkb_ref,
                 qa_ref, va_ref, qb_ref, vb_ref, kat_ref, ka_ref, kb_ref, knsq_ref):
    xb = x_ref[...].astype(jnp.bfloat16)
    da_scale = DA_HEAD_DIM ** -0.5 * LOG2E
    wg_scale = WG_HEAD_DIM ** -0.5 * LOG2E
    r0, r1, r2, r3 = DA_WIDTH, 2 * DA_WIDTH, 2 * DA_WIDTH + WG_Q, 2 * DA_WIDTH + WG_Q + WG_KV
    qa_ref[...] = (_nt_dot(wt_ref[0:r0, :], xb) * da_scale).astype(jnp.bfloat16)
    va_ref[...] = _nt_dot(wt_ref[r0:r1, :], xb).astype(jnp.bfloat16)
    qb_ref[...] = (_nt_dot(wt_ref[r1:r2, :], xb) * wg_scale).astype(jnp.bfloat16)
    kat = _nt_dot(wt_ref[r3:, :], xb)
    kat_ref[...] = kat.astype(jnp.bfloat16)
    nsq = jnp.sum((kat * kat).reshape(2 * DA_HEADS, DA_HEAD_DIM, ROW_TILE), axis=1)
    @pl.when(pl.program_id(1) == 0)
    def _():
        knsq_ref[...] = nsq

    @pl.when(pl.program_id(1) > 0)
    def _():
        knsq_ref[...] = jnp.maximum(knsq_ref[...], nsq)
    vb = _nt_dot(wt_ref[r2:r3, :], xb).astype(jnp.bfloat16)
    for t in range(ROW_TILE // LANES):
        vb_ref[t] = vb[:, t * LANES:(t + 1) * LANES]
    kb_ref[...] = _dot(xb, wkb_ref[...]).astype(jnp.bfloat16)
    pos = pl.program_id(1) * ROW_TILE + lax.broadcasted_iota(jnp.int32, (ROW_TILE, DA_KCOLS), 0)
    lane = lax.broadcasted_iota(jnp.int32, (ROW_TILE, DA_KCOLS), 1)
    pos_hi = ((pos // LANES) * LANES).astype(jnp.float32)
    pos_lo = (pos % LANES).astype(jnp.float32)
    lower = lane < DA_HEAD_DIM

    def position_cols(c0):
        return jnp.where(lane < c0, 0.0,
                         jnp.where(lane < c0 + 3, 1.0,
                                   jnp.where(lane < c0 + 6, pos_hi,
                                             jnp.where(lane < c0 + 9, pos_lo, 0.0))))
    pat0, pat1 = position_cols(DA_HEAD_DIM), position_cols(0)
    for hd in range(DA_HEADS):
        kk = _dot(xb, wka_ref[:, hd * DA_KCOLS:(hd + 1) * DA_KCOLS])
        ka_ref[:, (2 * hd) * DA_KCOLS:(2 * hd + 1) * DA_KCOLS] = jnp.where(lower, kk, pat0).astype(jnp.bfloat16)
        ka_ref[:, (2 * hd + 1) * DA_KCOLS:(2 * hd + 2) * DA_KCOLS] = jnp.where(lower, pat1, kk).astype(jnp.bfloat16)


def _project(x, w_t, w_ka, w_kb):
    B, S, D = x.shape
    ns = S // ROW_TILE
    rows_t = w_t.shape[0]
    ka_cols = 2 * DA_HEADS * DA_KCOLS
    bf = jnp.bfloat16
    return pl.pallas_call(
        _proj_kernel,
        grid=(B, ns),
        in_specs=[
            pl.BlockSpec((None, ROW_TILE, D), lambda b, s: (b, s, 0)),
            pl.BlockSpec((rows_t, D), lambda b, s: (0, 0)),
            pl.BlockSpec((D, DA_WIDTH), lambda b, s: (0, 0)),
            pl.BlockSpec((D, WG_KV), lambda b, s: (0, 0)),
        ],
        out_specs=[
            pl.BlockSpec((None, DA_WIDTH, ROW_TILE), lambda b, s: (b, 0, s)),
            pl.BlockSpec((None, None, DA_WIDTH, ROW_TILE), lambda b, s: (b, s, 0, 0)),
            pl.BlockSpec((None, WG_Q, ROW_TILE), lambda b, s: (b, 0, s)),
            pl.BlockSpec((None, ROW_TILE // LANES, WG_KV, LANES), lambda b, s: (b, s, 0, 0)),
            pl.BlockSpec((None, DA_WIDTH, ROW_TILE), lambda b, s: (b, 0, s)),
            pl.BlockSpec((None, ROW_TILE, ka_cols), lambda b, s: (b, s, 0)),
            pl.BlockSpec((None, ROW_TILE, WG_KV), lambda b, s: (b, s, 0)),
            pl.BlockSpec((None, 2 * DA_HEADS, ROW_TILE), lambda b, s: (b, 0, 0)),
        ],
        out_shape=[
            jax.ShapeDtypeStruct((B, DA_WIDTH, S), bf),
            jax.ShapeDtypeStruct((B, ns, DA_WIDTH, ROW_TILE), bf),
            jax.ShapeDtypeStruct((B, WG_Q, S), bf),
            jax.ShapeDtypeStruct((B, S // LANES, WG_KV, LANES), bf),
            jax.ShapeDtypeStruct((B, DA_WIDTH, S), bf),
            jax.ShapeDtypeStruct((B, S, ka_cols), bf),
            jax.ShapeDtypeStruct((B, S, WG_KV), bf),
            jax.ShapeDtypeStruct((B, 2 * DA_HEADS, ROW_TILE), jnp.float32),
        ],
        compiler_params=pltpu.CompilerParams(
            dimension_semantics=("arbitrary", "arbitrary"), vmem_limit_bytes=VMEM_LIMIT_BYTES),
        name="in_proj",
    )(x, w_t, w_ka, w_kb)


def _split3(x):
    hi = x.astype(jnp.bfloat16).astype(jnp.float32)
    r1 = x - hi
    mid = r1.astype(jnp.bfloat16).astype(jnp.float32)
    return hi, mid, r1 - mid


def _diff_attn_kernel(slope_ref, islope_ref, q_ref, kt_ref, kn_ref, k_ref, v_ref,
                      lq1_ref, lk1_ref, lq2_ref, lk2_ref, g_ref,
                      o_ref, bias_ref, qaug_ref, ea_ref, eb_ref, acc_ref,
                      *, n_k, lam_init):
    tq, tk, dh = DA_TQ, DA_TK, DA_HEAD_DIM
    h = pl.program_id(1)
    qi = pl.program_id(2)
    sig = slope_ref[h]

    def k_tile(kt):
        return k_ref[pl.ds(pl.multiple_of(kt * tk, tk), tk), :]

    def col_sum8(e):
        return jnp.sum(e.reshape(tk // SUBLANES, SUBLANES, e.shape[1]), axis=0)

    def both_maps(m0, m1):
        z = jnp.zeros_like(m0)
        return jnp.concatenate([jnp.concatenate([m0, z], axis=1), jnp.concatenate([z, m1], axis=1)], axis=0)

    @pl.when(qi == 0)
    def _():
        dd = (lax.broadcasted_iota(jnp.int32, (tk, tq), 1) - lax.broadcasted_iota(jnp.int32, (tk, tq), 0))
        b = -sig * jnp.abs(dd).astype(jnp.float32)
        bias_ref[...] = jnp.concatenate([b, b], axis=1)

    q = q_ref[...]
    qc = (q[:dh], q[dh:])
    zpad = jnp.zeros((DA_KCOLS - dh, tq), q.dtype)
    qpad = both_maps(jnp.concatenate([qc[0], zpad], axis=0),
                     jnp.concatenate([zpad, qc[1]], axis=0))

    q_start = qi * tq
    kt_ov = q_start // tk

    qk_self = q.astype(jnp.float32) * kt_ref[...].astype(jnp.float32)
    r = jnp.concatenate([jnp.sum(qk_self[:dh], axis=0, keepdims=True),
                         jnp.sum(qk_self[dh:], axis=0, keepdims=True)], axis=1)

    k_norm = jnp.sqrt(jnp.max(kn_ref[...], axis=1, keepdims=True)) * 1.01
    q_sq = q.astype(jnp.float32) ** 2
    bound = jnp.concatenate([jnp.sqrt(jnp.sum(q_sq[:dh], axis=0, keepdims=True)) * k_norm[0:1],
                             jnp.sqrt(jnp.sum(q_sq[dh:], axis=0, keepdims=True)) * k_norm[1:2]], axis=1)
    gap = jnp.max(bound - r, axis=1, keepdims=True)
    reach = jnp.minimum((gap - SKIP_LOG2) * islope_ref[h], float(2 * n_k * tk))
    qs_f = q_start.astype(jnp.float32)
    lo_f = jnp.floor((qs_f - reach - 1.0) * (1.0 / tk))
    hi_f = jnp.floor((qs_f + float(tq) + reach) * (1.0 / tk))
    lo_t = jnp.minimum(jnp.clip(lo_f, 0.0, float(n_k)).astype(jnp.int32)[0, 0], kt_ov)
    hi_t = jnp.maximum(jnp.clip(hi_f, -1.0, float(n_k - 1)).astype(jnp.int32)[0, 0], kt_ov)
    n_left = kt_ov - lo_t
    n_tot = n_left + hi_t - kt_ov

    ipos = (q_start + lax.broadcasted_iota(jnp.int32, (1, tq), 1)).astype(jnp.float32)
    row = lax.broadcasted_iota(jnp.int32, (AUG_ROWS, tq), 0)
    sig_row = jnp.full((1, tq), sig, jnp.float32)
    zrest = jnp.zeros((DA_KCOLS - dh - AUG_ROWS, tq), q.dtype)
    for side, sgn in enumerate((1.0, -1.0, 0.0)):
        s3 = _split3(sgn * sig_row)
        maps = []
        for c in range(2):
            r3 = _split3(-(r[:, c * tq:(c + 1) * tq] + sgn * sig * ipos))
            slab = jnp.zeros((AUG_ROWS, tq), jnp.float32)
            for i, piece in enumerate(r3 + s3 + s3):
                slab = jnp.where(row == i, piece, slab)
            parts = [qc[c], slab.astype(q.dtype), zrest] if c == 0 else [slab.astype(q.dtype), zrest, qc[c]]
            maps.append(jnp.concatenate(parts, axis=0))
        qaug_ref[side] = both_maps(*maps)

    def tile_of(i):
        i = jnp.clip(i, 0, jnp.maximum(n_tot - 1, 0))
        kt = jnp.where(i < n_left, lo_t + i, kt_ov + 1 + i - n_left)
        return jnp.minimum(kt, n_k - 1), (i >= n_left).astype(jnp.int32)

    def scores(i):
        kt, side = tile_of(i)
        return _dot(k_tile(kt), qaug_ref[side])

    def exp_scores(i, e_ref):
        e = jnp.exp2(scores(i))
        e_ref[...] = e.astype(e_ref.dtype)
        return col_sum8(e)

    def add_av(i, e_ref):
        kt, _ = tile_of(i)
        acc_ref[...] += _dot(v_ref[kt], e_ref[...])

    e = jnp.exp2(_dot(k_tile(kt_ov), qaug_ref[2]) + bias_ref[...])
    pend = exp_scores(0, ea_ref)
    l8 = col_sum8(e)
    acc_ref[...] = _dot(v_ref[kt_ov], e.astype(jnp.bfloat16))

    def pair(p, carry):
        l8, pend = carry
        i = 2 * p
        pend_b = exp_scores(i + 1, eb_ref)
        add_av(i, ea_ref)
        pend_a = exp_scores(i + 2, ea_ref)
        add_av(i + 1, eb_ref)
        return l8 + pend + pend_b, pend_a

    ends_even = jnp.logical_and(n_tot > 0, n_tot % 2 == 0)
    n_pairs = n_tot // 2 - ends_even.astype(jnp.int32)
    def pairs(first, count, c):
        for j in range(count):
            c = pair(first + j, c)
        return c

    carry = lax.fori_loop(0, n_pairs // 4, lambda t, c: pairs(4 * t, 4, c), (l8, pend))
    done = 4 * (n_pairs // 4)
    carry = lax.cond(n_pairs % 4 >= 2, lambda c: pairs(done, 2, c), lambda c: c, carry)
    done = done + 2 * ((n_pairs % 4) // 2)
    l8, pend = lax.cond(n_pairs % 2 == 1, lambda c: pair(done, c), lambda c: c, carry)

    def last_tile():
        add_av(n_tot - 1, ea_ref)
        return l8 + pend

    def last_two_tiles():
        pend_b = exp_scores(n_tot - 1, eb_ref)
        add_av(n_tot - 2, ea_ref)
        add_av(n_tot - 1, eb_ref)
        return l8 + pend + pend_b

    l8 = lax.cond(n_tot % 2 == 1, last_tile, lambda: l8)
    l8 = lax.cond(ends_even, last_two_tiles, lambda: l8)
    l = jnp.sum(l8, axis=0, keepdims=True)

    lam = (jnp.exp(jnp.sum(lq1_ref[...] * lk1_ref[...], axis=1, keepdims=True))
           - jnp.exp(jnp.sum(lq2_ref[...] * lk2_ref[...], axis=1, keepdims=True)) + lam_init)

    def finish(l):
        on = acc_ref[...] / l
        o = on[:, :tq] - lam * on[:, tq:]
        o = o * lax.rsqrt(jnp.mean(o * o, axis=0, keepdims=True) + LN_EPS)
        o = o * g_ref[...] * (1.0 - lam_init)
        o_ref[...] = o.T.astype(o_ref.dtype)

    finish(l)

    chk = jnp.sum(acc_ref[...] * 0.0) + jnp.sum(l * 0.0)
    overflowed = jnp.logical_not(chk == 0.0)

    def exact_path():
        dd = (lax.broadcasted_iota(jnp.int32, (tk, tq), 1)
              - lax.broadcasted_iota(jnp.int32, (tk, tq), 0)).astype(jnp.float32)
        acc_ref[...] = jnp.zeros_like(acc_ref)

        def body(kt, carry):
            m, lc = carry
            b = -sig * jnp.abs(dd + (q_start - kt * tk).astype(jnp.float32))
            u = _dot(k_tile(kt), qpad) + jnp.concatenate([b, b], axis=1)
            m_new = jnp.maximum(m, jnp.max(u, axis=0, keepdims=True))
            e = jnp.exp2(u - m_new)
            alpha = jnp.exp2(m - m_new)
            acc_ref[...] = alpha * acc_ref[...] + _dot(v_ref[kt], e.astype(jnp.bfloat16))
            return m_new, alpha * lc + jnp.sum(e, axis=0, keepdims=True)

        init = (jnp.full((1, 2 * tq), NEG_BIG, jnp.float32), jnp.zeros((1, 2 * tq), jnp.float32))
        finish(lax.fori_loop(0, n_k, body, init)[1])

    pl.when(overflowed)(exact_path)


def _diff_attention(q_t, k_t, k_nsq, k, v_t, lam_q1, lam_k1, lam_q2, lam_k2, subln_g, lam_init):
    B, _, S = q_t.shape
    assert DA_TQ == DA_TK, "the diagonal-tile bias assumes square tiles"
    hw = 2 * DA_HEAD_DIM
    n_k = S // DA_TK
    slopes2 = _alibi_slopes(DA_HEADS) * LOG2E
    vec = lambda a: a.reshape(1, DA_HEAD_DIM)
    small = pl.BlockSpec((1, DA_HEAD_DIM), lambda b, h, i, s1, s2: (0, 0))
    grid_spec = pltpu.PrefetchScalarGridSpec(
        num_scalar_prefetch=2,
        grid=(B, DA_HEADS, S // DA_TQ),
        in_specs=[
            pl.BlockSpec((None, hw, DA_TQ), lambda b, h, i, s1, s2: (b, h, i)),
            pl.BlockSpec((None, hw, DA_TQ), lambda b, h, i, s1, s2: (b, h, i)),
            pl.BlockSpec((None, None, 2, ROW_TILE), lambda b, h, i, s1, s2: (b, h, 0, 0)),
            pl.BlockSpec((None, S, 2 * DA_KCOLS), lambda b, h, i, s1, s2: (b, 0, h)),
            pl.BlockSpec((None, n_k, hw, DA_TK), lambda b, h, i, s1, s2: (b, 0, h, 0)),
            small, small, small, small,
            pl.BlockSpec((hw, 1), lambda b, h, i, s1, s2: (0, 0)),
        ],
        out_specs=pl.BlockSpec((None, DA_TQ, hw), lambda b, h, i, s1, s2: (b, i, h)),
        scratch_shapes=[
            pltpu.VMEM((DA_TK, 2 * DA_TQ), jnp.float32),
            pltpu.VMEM((3, 2 * DA_KCOLS, 2 * DA_TQ), jnp.bfloat16),
            pltpu.VMEM((DA_TK, 2 * DA_TQ), jnp.bfloat16),
            pltpu.VMEM((DA_TK, 2 * DA_TQ), jnp.bfloat16),
            pltpu.VMEM((hw, 2 * DA_TQ), jnp.float32),
        ],
    )
    return pl.pallas_call(
        functools.partial(_diff_attn_kernel, n_k=n_k, lam_init=lam_init),
        grid_spec=grid_spec,
        out_shape=jax.ShapeDtypeStruct((B, S, DA_WIDTH), jnp.bfloat16),
        compiler_params=pltpu.CompilerParams(
            dimension_semantics=("arbitrary", "arbitrary", "arbitrary"),
            vmem_limit_bytes=VMEM_LIMIT_BYTES),
        name="diff_attn",
    )(jnp.asarray(slopes2, jnp.float32), jnp.asarray(1.0 / slopes2, jnp.float32),
      q_t, k_t, k_nsq.reshape(B, DA_HEADS, 2, ROW_TILE), k, v_t,
      vec(lam_q1), vec(lam_k1), vec(lam_q2), vec(lam_k2), subln_g.reshape(hw, 1))


def _win_attn_kernel(slope_ref, sink_ref, q_ref, k_ref, v_ref, o_ref, ot_ref, *, seq_len):
    tq, band = WG_TQ, WG_BAND
    n_vt = band // LANES
    rep = WG_HEADS // WG_KV_HEADS
    qi = pl.program_id(1)
    q_start = qi * tq
    tile0 = jnp.clip(q_start // LANES - WINDOW // LANES, 0, seq_len // LANES - n_vt)
    k_start = pl.multiple_of(tile0 * LANES, LANES)

    kband = k_ref[pl.ds(k_start, band), :]
    kpos = k_start + lax.broadcasted_iota(jnp.int32, (band, tq), 0)
    qpos = q_start + lax.broadcasted_iota(jnp.int32, (band, tq), 1)
    dist_i = jnp.abs(qpos - kpos)
    dist = jnp.where(dist_i <= WINDOW, dist_i.astype(jnp.float32), -NEG_BIG)

    def group(g, shift_by_max):
        rows = slice(g * WG_HEAD_DIM, (g + 1) * WG_HEAD_DIM)
        blocks = []
        for r in range(rep):
            hd = g * rep + r
            parts = []
            if g > 0:
                parts.append(jnp.zeros((g * WG_HEAD_DIM, tq), q_ref.dtype))
            parts.append(q_ref[hd * WG_HEAD_DIM:(hd + 1) * WG_HEAD_DIM, :])
            if g < WG_KV_HEADS - 1:
                parts.append(jnp.zeros(((WG_KV_HEADS - 1 - g) * WG_HEAD_DIM, tq), q_ref.dtype))
            blocks.append(jnp.concatenate(parts, axis=0))
        s = _dot(kband, jnp.concatenate(blocks, axis=1))
        es, denoms = [], []
        for r in range(rep):
            hd = g * rep + r
            sr = s[:, r * tq:(r + 1) * tq] - slope_ref[hd] * dist
            sk = jnp.full((1, tq), sink_ref[hd] * LOG2E, jnp.float32)
            if shift_by_max:
                m = jnp.maximum(jnp.max(sr, axis=0, keepdims=True), sk)
                sr, sk = sr - m, sk - m
            e = jnp.exp2(sr)
            denoms.append(jnp.sum(e, axis=0, keepdims=True) + jnp.exp2(sk))
            es.append(e.astype(jnp.bfloat16))
        vg = jnp.concatenate([v_ref[tile0 + t, rows, :] for t in range(n_vt)], axis=1)
        den = jnp.concatenate(denoms, axis=1)
        og = _dot(vg, jnp.concatenate(es, axis=1)) / den
        for r in range(rep):
            hd = g * rep + r
            ot_ref[hd * WG_HEAD_DIM:(hd + 1) * WG_HEAD_DIM, :] = og[:, r * tq:(r + 1) * tq]
        return jnp.sum(og * 0.0) + jnp.sum(jnp.where(den >= WG_MIN_DENOM, 0.0, 1.0))

    bad = group(0, False)
    for g in range(1, WG_KV_HEADS):
        bad = bad + group(g, False)

    @pl.when(jnp.logical_not(bad == 0.0))
    def _():
        for g in range(WG_KV_HEADS):
            group(g, True)

    o_ref[...] = ot_ref[...].T.astype(o_ref.dtype)


def _window_attention(q_t, k, v_t, sink_logit):
    B, _, S = q_t.shape
    slopes2 = jnp.asarray(_alibi_slopes(WG_HEADS) * LOG2E, jnp.float32)
    grid_spec = pltpu.PrefetchScalarGridSpec(
        num_scalar_prefetch=2,
        grid=(B, S // WG_TQ),
        in_specs=[
            pl.BlockSpec((None, WG_Q, WG_TQ), lambda b, i, s1, s2: (b, 0, i)),
            pl.BlockSpec((None, S, WG_KV), lambda b, i, s1, s2: (b, 0, 0)),
            pl.BlockSpec((None, S // LANES, WG_KV, LANES), lambda b, i, s1, s2: (b, 0, 0, 0)),
        ],
        out_specs=pl.BlockSpec((None, WG_TQ, WG_Q), lambda b, i, s1, s2: (b, i, 0)),
        scratch_shapes=[pltpu.VMEM((WG_Q, WG_TQ), jnp.float32)],
    )
    return pl.pallas_call(
        functools.partial(_win_attn_kernel, seq_len=S),
        grid_spec=grid_spec,
        out_shape=jax.ShapeDtypeStruct((B, S, WG_Q), jnp.bfloat16),
        compiler_params=pltpu.CompilerParams(
            dimension_semantics=("arbitrary", "arbitrary"), vmem_limit_bytes=VMEM_LIMIT_BYTES),
        name="win_attn",
    )(slopes2, sink_logit.astype(jnp.float32), q_t, k, v_t)


def _layer_norm(x, g, b):
    mu = jnp.mean(x, axis=-1, keepdims=True)
    xc = x - mu
    var = jnp.mean(xc * xc, axis=-1, keepdims=True)
    return xc * lax.rsqrt(var + LN_EPS) * g + b


def _merge_kernel(x_ref, oa_ref, ob_ref, wg_ref, bg_ref, wa_ref, wb_ref, wo_ref, g_ref, b_ref, y_ref):
    x = x_ref[...]
    gates = jax.nn.sigmoid(_dot(x.astype(jnp.bfloat16), wg_ref[...]) + bg_ref[...])
    merged = (gates[:, :D_MODEL] * _dot(oa_ref[...], wa_ref[...])
              + gates[:, D_MODEL:] * _dot(ob_ref[...], wb_ref[...]))
    mix = _dot(merged.astype(jnp.bfloat16), wo_ref[...])
    y_ref[...] = _layer_norm(DEEPNORM_ALPHA * x + mix, g_ref[...], b_ref[...])


def _merge(x, o_a, o_b, w_gate, b_gate, w_br_a, w_br_b, w_out, ln_g, ln_b):
    B, S, D = x.shape
    tok = lambda b, s: (b, s, 0)
    const = lambda b, s: (0, 0)
    return pl.pallas_call(
        _merge_kernel,
        grid=(B, S // ROW_TILE),
        in_specs=[
            pl.BlockSpec((None, ROW_TILE, D), tok),
            pl.BlockSpec((None, ROW_TILE, DA_WIDTH), tok),
            pl.BlockSpec((None, ROW_TILE, WG_Q), tok),
            pl.BlockSpec((D, 2 * D), const),
            pl.BlockSpec((1, 2 * D), const),
            pl.BlockSpec((DA_WIDTH, D), const),
            pl.BlockSpec((WG_Q, D), const),
            pl.BlockSpec((D, D), const),
            pl.BlockSpec((1, D), const),
            pl.BlockSpec((1, D), const),
        ],
        out_specs=pl.BlockSpec((None, ROW_TILE, D), tok),
        out_shape=jax.ShapeDtypeStruct((B, S, D), jnp.float32),
        compiler_params=pltpu.CompilerParams(
            dimension_semantics=("arbitrary", "arbitrary"), vmem_limit_bytes=VMEM_LIMIT_BYTES),
        name="merge_ln1",
    )(x, o_a, o_b, w_gate, b_gate, w_br_a, w_br_b, w_out, ln_g, ln_b)


def _ffn_kernel(x_ref, w1_ref, b1_ref, w2_ref, b2_ref, g_ref, b_ref, y_ref):
    x = x_ref[...]
    h = jnp.maximum(_dot(x.astype(jnp.bfloat16), w1_ref[...]) + b1_ref[...], 0.0)
    f = _dot((h * h).astype(jnp.bfloat16), w2_ref[...]) + b2_ref[...]
    y_ref[...] = _layer_norm(DEEPNORM_ALPHA * x + f, g_ref[...], b_ref[...])


def _ffn(x, w1, b1, w2, b2, ln_g, ln_b):
    B, S, D = x.shape
    tok = lambda b, s: (b, s, 0)
    const = lambda b, s: (0, 0)
    return pl.pallas_call(
        _ffn_kernel,
        grid=(B, S // ROW_TILE),
        in_specs=[
            pl.BlockSpec((None, ROW_TILE, D), tok),
            pl.BlockSpec((D, D_FF), const),
            pl.BlockSpec((1, D_FF), const),
            pl.BlockSpec((D_FF, D), const),
            pl.BlockSpec((1, D), const),
            pl.BlockSpec((1, D), const),
            pl.BlockSpec((1, D), const),
        ],
        out_specs=pl.BlockSpec((None, ROW_TILE, D), tok),
        out_shape=jax.ShapeDtypeStruct((B, S, D), jnp.float32),
        compiler_params=pltpu.CompilerParams(
            dimension_semantics=("arbitrary", "arbitrary"), vmem_limit_bytes=VMEM_LIMIT_BYTES),
        name="ffn_ln2",
    )(x, w1, b1, w2, b2, ln_g, ln_b)


def _encoder_layer(x, l, w_in, b_gate, lam_q1, lam_k1, lam_q2, lam_k2, subln_g, sink_logit,
                   w_br_a, w_br_b, w_out, ln1_g, ln1_b, w_ff1, b_ff1, w_ff2, b_ff2, ln2_g, ln2_b):
    bf = jnp.bfloat16
    row = lambda a: a.reshape(1, -1)
    w_t = jnp.concatenate([w_in[:, OFF_DA_Q:OFF_DA_K], w_in[:, OFF_DA_V:OFF_WG_Q],
                           w_in[:, OFF_WG_Q:OFF_WG_K], w_in[:, OFF_WG_V:OFF_GATE],
                           w_in[:, OFF_DA_K:OFF_DA_V]], axis=1).T.astype(bf)
    w_ka = w_in[:, OFF_DA_K:OFF_DA_V].astype(bf)
    w_kb = w_in[:, OFF_WG_K:OFF_WG_V].astype(bf)
    w_gate = w_in[:, OFF_GATE:].astype(bf)
    lam_init = 0.8 - 0.6 * math.exp(-0.3 * l)

    qa_t, va_t, qb_t, vb_t, ka_t, k_a, k_b, ka_nsq = _project(x, w_t, w_ka, w_kb)
    o_a = _diff_attention(qa_t, ka_t, ka_nsq, k_a, va_t, lam_q1, lam_k1, lam_q2, lam_k2, subln_g, lam_init)
    o_b = _window_attention(qb_t, k_b, vb_t, sink_logit)
    x1 = _merge(x, o_a, o_b, w_gate, row(b_gate), w_br_a.astype(bf), w_br_b.astype(bf),
                w_out.astype(bf), row(ln1_g), row(ln1_b))
    return _ffn(x1, w_ff1.astype(bf), row(b_ff1), w_ff2.astype(bf), row(b_ff2), row(ln2_g), row(ln2_b))


def kernel(x_prompt, x_sample, w_in, b_gate, lam_q1, lam_k1, lam_q2, lam_k2, subln_g, sink_logit,
           w_br_a, w_br_b, w_out, ln1_g, ln1_b, w_ff1, b_ff1, w_ff2, b_ff2, ln2_g, ln2_b):
    def run(x):
        for l in range(DEPTH):
            x = _encoder_layer(x, l, w_in[l], b_gate[l], lam_q1[l], lam_k1[l], lam_q2[l], lam_k2[l],
                               subln_g[l], sink_logit[l], w_br_a[l], w_br_b[l], w_out[l],
                               ln1_g[l], ln1_b[l], w_ff1[l], b_ff1[l], w_ff2[l], b_ff2[l],
                               ln2_g[l], ln2_b[l])
        return x

    return (run(x_prompt), run(x_sample))
```

```python
import functools
import math

import jax
import jax.numpy as jnp
import numpy as np
from jax import lax
from jax.experimental import pallas as pl
from jax.experimental.pallas import tpu as pltpu

D_MODEL = 1024
DA_HEADS = 8
DA_HEAD_DIM = 64
DA_WIDTH = DA_HEADS * 2 * DA_HEAD_DIM
WG_HEADS = 16
WG_KV_HEADS = 4
WG_HEAD_DIM = 64
WG_Q = WG_HEADS * WG_HEAD_DIM
WG_KV = WG_KV_HEADS * WG_HEAD_DIM
WINDOW = 128
D_FF = 4 * D_MODEL
DEPTH = 1
DEEPNORM_ALPHA = (2.0 * DEPTH) ** 0.25
LN_EPS = 1e-5
LOG2E = math.log2(math.e)
NEG_BIG = -1e30

OFF_DA_Q = 0
OFF_DA_K = OFF_DA_Q + DA_WIDTH
OFF_DA_V = OFF_DA_K + DA_WIDTH
OFF_WG_Q = OFF_DA_V + DA_WIDTH
OFF_WG_K = OFF_WG_Q + WG_Q
OFF_WG_V = OFF_WG_K + WG_KV
OFF_GATE = OFF_WG_V + WG_KV

LANES = 128
SUBLANES = 8
BF16_ROWS = 16
VMEM_LIMIT_BYTES = 56 * 1024 * 1024

ROW_TILE = 512
DA_TQ = 512
DA_TK = ROW_TILE
WG_TQ = 256
WG_BAND = WG_TQ + 2 * WINDOW
WG_MIN_DENOM = 2.0 ** -60

DA_KCOLS = 2 * DA_HEAD_DIM
AUG_ROWS = BF16_ROWS
SKIP_LOG2 = -150.0


def _alibi_slopes(n_heads):
    return 2.0 ** (-8.0 * np.arange(1, n_heads + 1) / n_heads)


def _nt_dot(a, b):
    return lax.dot_general(a, b, (((1,), (1,)), ((), ())), preferred_element_type=jnp.float32)


def _dot(a, b):
    return jnp.dot(a, b, preferred_element_type=jnp.float32)


def _proj_kernel(x_ref, wt_ref, wka_ref, wkb_ref,
                 qa_ref, va_ref, qb_ref, vb_ref, kat_ref, ka_ref, kb_ref, knsq_ref):
    xb = x_ref[...].astype(jnp.bfloat16)
    da_scale = DA_HEAD_DIM ** -0.5 * LOG2E
    wg_scale = WG_HEAD_DIM ** -0.5 * LOG2E
    r0, r1, r2, r3 = DA_WIDTH, 2 * DA_WIDTH, 2 * DA_WIDTH + WG_Q, 2 * DA_WIDTH + WG_Q + WG_KV
    qa_ref[...] = (_nt_dot(wt_ref[0:r0, :], xb) * da_scale).astype(jnp.bfloat16)
    va_ref[...] = _nt_dot(wt_ref[r0:r1, :], xb).astype(jnp.bfloat16)
    qb_ref[...] = (_nt_dot(wt_ref[r1:r2, :], xb) * wg_scale).astype(jnp.bfloat16)
    kat = _nt_dot(wt_ref[r3:, :], xb)
    kat_ref[...] = kat.astype(jnp.bfloat16)
    nsq = jnp.sum((kat * kat).reshape(2 * DA_HEADS, DA_HEAD_DIM, ROW_TILE), axis=1)
    vb = _nt_dot(wt_ref[r2:r3, :], xb).astype(jnp.bfloat16)
    for t in range(ROW_TILE // LANES):
        vb_ref[t] = vb[:, t * LANES:(t + 1) * LANES]
    kb_ref[...] = _dot(xb, wkb_ref[...]).astype(jnp.bfloat16)
    pos = pl.program_id(1) * ROW_TILE + lax.broadcasted_iota(jnp.int32, (ROW_TILE, DA_KCOLS), 0)
    lane = lax.broadcasted_iota(jnp.int32, (ROW_TILE, DA_KCOLS), 1)
    pos_hi = ((pos // LANES) * LANES).astype(jnp.float32)
    pos_lo = (pos % LANES).astype(jnp.float32)
    lower = lane < DA_HEAD_DIM

    def position_cols(c0):
        return jnp.where(lane < c0, 0.0,
                         jnp.where(lane < c0 + 3, 1.0,
                                   jnp.where(lane < c0 + 6, pos_hi,
                                             jnp.where(lane < c0 + 9, pos_lo, 0.0))))
    pat0, pat1 = position_cols(DA_HEAD_DIM), position_cols(0)
    for hd in range(DA_HEADS):
        kk = _dot(xb, wka_ref[:, hd * DA_KCOLS:(hd + 1) * DA_KCOLS])
        ka_ref[:, (2 * hd) * DA_KCOLS:(2 * hd + 1) * DA_KCOLS] = jnp.where(lower, kk, pat0).astype(jnp.bfloat16)
        ka_ref[:, (2 * hd + 1) * DA_KCOLS:(2 * hd + 2) * DA_KCOLS] = jnp.where(lower, pat1, kk).astype(jnp.bfloat16)

    @pl.when(pl.program_id(1) == 0)
    def _():
        knsq_ref[...] = nsq

    @pl.when(pl.program_id(1) > 0)
    def _():
        knsq_ref[...] = jnp.maximum(knsq_ref[...], nsq)


def _project(x, w_t, w_ka, w_kb):
    B, S, D = x.shape
    ns = S // ROW_TILE
    rows_t = w_t.shape[0]
    ka_cols = 2 * DA_HEADS * DA_KCOLS
    bf = jnp.bfloat16
    return pl.pallas_call(
        _proj_kernel,
        grid=(B, ns),
        in_specs=[
            pl.BlockSpec((None, ROW_TILE, D), lambda b, s: (b, s, 0)),
            pl.BlockSpec((rows_t, D), lambda b, s: (0, 0)),
            pl.BlockSpec((D, DA_WIDTH), lambda b, s: (0, 0)),
            pl.BlockSpec((D, WG_KV), lambda b, s: (0, 0)),
        ],
        out_specs=[
            pl.BlockSpec((None, DA_WIDTH, ROW_TILE), lambda b, s: (b, 0, s)),
            pl.BlockSpec((None, None, DA_WIDTH, ROW_TILE), lambda b, s: (b, s, 0, 0)),
            pl.BlockSpec((None, WG_Q, ROW_TILE), lambda b, s: (b, 0, s)),
            pl.BlockSpec((None, ROW_TILE // LANES, WG_KV, LANES), lambda b, s: (b, s, 0, 0)),
            pl.BlockSpec((None, DA_WIDTH, ROW_TILE), lambda b, s: (b, 0, s)),
            pl.BlockSpec((None, ROW_TILE, ka_cols), lambda b, s: (b, s, 0)),
            pl.BlockSpec((None, ROW_TILE, WG_KV), lambda b, s: (b, s, 0)),
            pl.BlockSpec((None, 2 * DA_HEADS, ROW_TILE), lambda b, s: (b, 0, 0)),
        ],
        out_shape=[
            jax.ShapeDtypeStruct((B, DA_WIDTH, S), bf),
            jax.ShapeDtypeStruct((B, ns, DA_WIDTH, ROW_TILE), bf),
            jax.ShapeDtypeStruct((B, WG_Q, S), bf),
            jax.ShapeDtypeStruct((B, S // LANES, WG_KV, LANES), bf),
            jax.ShapeDtypeStruct((B, DA_WIDTH, S), bf),
            jax.ShapeDtypeStruct((B, S, ka_cols), bf),
            jax.ShapeDtypeStruct((B, S, WG_KV), bf),
            jax.ShapeDtypeStruct((B, 2 * DA_HEADS, ROW_TILE), jnp.float32),
        ],
        compiler_params=pltpu.CompilerParams(
            dimension_semantics=("arbitrary", "arbitrary"), vmem_limit_bytes=VMEM_LIMIT_BYTES),
        name="in_proj",
    )(x, w_t, w_ka, w_kb)


def _split3(x):
    hi = x.astype(jnp.bfloat16).astype(jnp.float32)
    r1 = x - hi
    mid = r1.astype(jnp.bfloat16).astype(jnp.float32)
    return hi, mid, r1 - mid


def _diff_attn_kernel(slope_ref, islope_ref, q_ref, kt_ref, kn_ref, k_ref, v_ref,
                      lq1_ref, lk1_ref, lq2_ref, lk2_ref, g_ref,
                      o_ref, bias_ref, qaug_ref, ea_ref, eb_ref, acc_ref,
                      *, n_k, lam_init):
    tq, tk, dh = DA_TQ, DA_TK, DA_HEAD_DIM
    h = pl.program_id(1)
    qi = pl.program_id(2)
    sig = slope_ref[h]

    def k_tile(kt):
        return k_ref[pl.ds(pl.multiple_of(kt * tk, tk), tk), :]

    def col_sum8(e):
        return jnp.sum(e.reshape(tk // SUBLANES, SUBLANES, e.shape[1]), axis=0)

    def both_maps(m0, m1):
        z = jnp.zeros_like(m0)
        return jnp.concatenate([jnp.concatenate([m0, z], axis=1), jnp.concatenate([z, m1], axis=1)], axis=0)

    @pl.when(qi == 0)
    def _():
        dd = (lax.broadcasted_iota(jnp.int32, (tk, tq), 1) - lax.broadcasted_iota(jnp.int32, (tk, tq), 0))
        b = -sig * jnp.abs(dd).astype(jnp.float32)
        bias_ref[...] = jnp.concatenate([b, b], axis=1)

    q = q_ref[...]
    qc = (q[:dh], q[dh:])
    zpad = jnp.zeros((DA_KCOLS - dh, tq), q.dtype)
    qpad = both_maps(jnp.concatenate([qc[0], zpad], axis=0),
                     jnp.concatenate([zpad, qc[1]], axis=0))

    q_start = qi * tq
    kt_ov = q_start // tk

    qk_self = q.astype(jnp.float32) * kt_ref[...].astype(jnp.float32)
    r = jnp.concatenate([jnp.sum(qk_self[:dh], axis=0, keepdims=True),
                         jnp.sum(qk_self[dh:], axis=0, keepdims=True)], axis=1)

    k_norm = jnp.sqrt(jnp.max(kn_ref[...], axis=1, keepdims=True)) * 1.01
    q_sq = q.astype(jnp.float32) ** 2
    bound = jnp.concatenate([jnp.sqrt(jnp.sum(q_sq[:dh], axis=0, keepdims=True)) * k_norm[0:1],
                             jnp.sqrt(jnp.sum(q_sq[dh:], axis=0, keepdims=True)) * k_norm[1:2]], axis=1)
    gap = jnp.max(bound - r, axis=1, keepdims=True)
    reach = jnp.minimum((gap - SKIP_LOG2) * islope_ref[h], float(2 * n_k * tk))
    qs_f = q_start.astype(jnp.float32)
    lo_f = jnp.floor((qs_f - reach - 1.0) * (1.0 / tk))
    hi_f = jnp.floor((qs_f + float(tq) + reach) * (1.0 / tk))
    lo_t = jnp.minimum(jnp.clip(lo_f, 0.0, float(n_k)).astype(jnp.int32)[0, 0], kt_ov)
    hi_t = jnp.maximum(jnp.clip(hi_f, -1.0, float(n_k - 1)).astype(jnp.int32)[0, 0], kt_ov)
    n_left = kt_ov - lo_t
    n_tot = n_left + hi_t - kt_ov

    ipos = (q_start + lax.broadcasted_iota(jnp.int32, (1, tq), 1)).astype(jnp.float32)
    row = lax.broadcasted_iota(jnp.int32, (AUG_ROWS, tq), 0)
    sig_row = jnp.full((1, tq), sig, jnp.float32)
    zrest = jnp.zeros((DA_KCOLS - dh - AUG_ROWS, tq), q.dtype)
    for side, sgn in enumerate((1.0, -1.0, 0.0)):
        s3 = _split3(sgn * sig_row)
        maps = []
        for c in range(2):
            r3 = _split3(-(r[:, c * tq:(c + 1) * tq] + sgn * sig * ipos))
            slab = jnp.zeros((AUG_ROWS, tq), jnp.float32)
            for i, piece in enumerate(r3 + s3 + s3):
                slab = jnp.where(row == i, piece, slab)
            parts = [qc[c], slab.astype(q.dtype), zrest] if c == 0 else [slab.astype(q.dtype), zrest, qc[c]]
            maps.append(jnp.concatenate(parts, axis=0))
        qaug_ref[side] = both_maps(*maps)

    def tile_of(i):
        i = jnp.clip(i, 0, jnp.maximum(n_tot - 1, 0))
        kt = jnp.where(i < n_left, lo_t + i, kt_ov + 1 + i - n_left)
        return jnp.minimum(kt, n_k - 1), (i >= n_left).astype(jnp.int32)

    def scores(i):
        kt, side = tile_of(i)
        return _dot(k_tile(kt), qaug_ref[side])

    def exp_scores(i, e_ref):
        e = jnp.exp2(scores(i))
        e_ref[...] = e.astype(e_ref.dtype)
        return col_sum8(e)

    def add_av(i, e_ref):
        kt, _ = tile_of(i)
        acc_ref[...] += _dot(v_ref[kt], e_ref[...])

    e = jnp.exp2(_dot(k_tile(kt_ov), qaug_ref[2]) + bias_ref[...])
    pend = exp_scores(0, ea_ref)
    l8 = col_sum8(e)
    acc_ref[...] = _dot(v_ref[kt_ov], e.astype(jnp.bfloat16))

    def pair(p, carry):
        l8, pend = carry
        i = 2 * p
        pend_b = exp_scores(i + 1, eb_ref)
        add_av(i, ea_ref)
        pend_a = exp_scores(i + 2, ea_ref)
        add_av(i + 1, eb_ref)
        return l8 + pend + pend_b, pend_a

    ends_even = jnp.logical_and(n_tot > 0, n_tot % 2 == 0)
    n_pairs = n_tot // 2 - ends_even.astype(jnp.int32)

    def pairs(first, count, c):
        for j in range(count):
            c = pair(first + j, c)
        return c

    def last_tile(c):
        add_av(n_tot - 1, ea_ref)
        return c[0] + c[1]

    def last_two_tiles(c):
        pend_b = exp_scores(n_tot - 1, eb_ref)
        add_av(n_tot - 2, ea_ref)
        add_av(n_tot - 1, eb_ref)
        return c[0] + c[1] + pend_b

    carry = lax.fori_loop(0, n_pairs // 4, lambda t, c: pairs(4 * t, 4, c), (l8, pend))
    done = 4 * (n_pairs // 4)

    def seven_left(c):
        return last_tile(pairs(done, 3, c))

    def other_left(c):
        c = lax.cond(n_pairs % 4 >= 2, lambda c: pairs(done, 2, c), lambda c: c, c)
        c = lax.cond(n_pairs % 2 == 1, lambda c: pair(done + 2 * ((n_pairs % 4) // 2), c), lambda c: c, c)
        l8 = lax.cond(n_tot % 2 == 1, last_tile, lambda c: c[0], c)
        return lax.cond(ends_even, last_two_tiles, lambda c: c[0], (l8, c[1]))

    l8 = lax.cond(jnp.logical_and(n_pairs % 4 == 3, n_tot % 2 == 1), seven_left, other_left, carry)
    l = jnp.sum(l8, axis=0, keepdims=True)

    lam = (jnp.exp(jnp.sum(lq1_ref[...] * lk1_ref[...], axis=1, keepdims=True))
           - jnp.exp(jnp.sum(lq2_ref[...] * lk2_ref[...], axis=1, keepdims=True)) + lam_init)

    def finish(l):
        on = acc_ref[...] / l
        o = on[:, :tq] - lam * on[:, tq:]
        o = o * lax.rsqrt(jnp.mean(o * o, axis=0, keepdims=True) + LN_EPS)
        o = o * g_ref[...] * (1.0 - lam_init)
        o_ref[...] = o.T.astype(o_ref.dtype)

    finish(l)

    chk = jnp.sum(acc_ref[...] * 0.0) + jnp.sum(l * 0.0)
    overflowed = jnp.logical_not(chk == 0.0)

    def exact_path():
        dd = (lax.broadcasted_iota(jnp.int32, (tk, tq), 1)
              - lax.broadcasted_iota(jnp.int32, (tk, tq), 0)).astype(jnp.float32)
        acc_ref[...] = jnp.zeros_like(acc_ref)

        def body(kt, carry):
            m, lc = carry
            b = -sig * jnp.abs(dd + (q_start - kt * tk).astype(jnp.float32))
            u = _dot(k_tile(kt), qpad) + jnp.concatenate([b, b], axis=1)
            m_new = jnp.maximum(m, jnp.max(u, axis=0, keepdims=True))
            e = jnp.exp2(u - m_new)
            alpha = jnp.exp2(m - m_new)
            acc_ref[...] = alpha * acc_ref[...] + _dot(v_ref[kt], e.astype(jnp.bfloat16))
            return m_new, alpha * lc + jnp.sum(e, axis=0, keepdims=True)

        init = (jnp.full((1, 2 * tq), NEG_BIG, jnp.float32), jnp.zeros((1, 2 * tq), jnp.float32))
        finish(lax.fori_loop(0, n_k, body, init)[1])

    pl.when(overflowed)(exact_path)


def _diff_attention(q_t, k_t, k_nsq, k, v_t, lam_q1, lam_k1, lam_q2, lam_k2, subln_g, lam_init):
    B, _, S = q_t.shape
    assert DA_TQ == DA_TK, "the diagonal-tile bias assumes square tiles"
    hw = 2 * DA_HEAD_DIM
    n_k = S // DA_TK
    slopes2 = _alibi_slopes(DA_HEADS) * LOG2E
    vec = lambda a: a.reshape(1, DA_HEAD_DIM)
    small = pl.BlockSpec((1, DA_HEAD_DIM), lambda b, h, i, s1, s2: (0, 0))
    grid_spec = pltpu.PrefetchScalarGridSpec(
        num_scalar_prefetch=2,
        grid=(B, DA_HEADS, S // DA_TQ),
        in_specs=[
            pl.BlockSpec((None, hw, DA_TQ), lambda b, h, i, s1, s2: (b, h, i)),
            pl.BlockSpec((None, hw, DA_TQ), lambda b, h, i, s1, s2: (b, h, i)),
            pl.BlockSpec((None, None, 2, ROW_TILE), lambda b, h, i, s1, s2: (b, h, 0, 0)),
            pl.BlockSpec((None, S, 2 * DA_KCOLS), lambda b, h, i, s1, s2: (b, 0, h)),
            pl.BlockSpec((None, n_k, hw, DA_TK), lambda b, h, i, s1, s2: (b, 0, h, 0)),
            small, small, small, small,
            pl.BlockSpec((hw, 1), lambda b, h, i, s1, s2: (0, 0)),
        ],
        out_specs=pl.BlockSpec((None, DA_TQ, hw), lambda b, h, i, s1, s2: (b, i, h)),
        scratch_shapes=[
            pltpu.VMEM((DA_TK, 2 * DA_TQ), jnp.float32),
            pltpu.VMEM((3, 2 * DA_KCOLS, 2 * DA_TQ), jnp.bfloat16),
            pltpu.VMEM((DA_TK, 2 * DA_TQ), jnp.bfloat16),
            pltpu.VMEM((DA_TK, 2 * DA_TQ), jnp.bfloat16),
            pltpu.VMEM((hw, 2 * DA_TQ), jnp.float32),
        ],
    )
    return pl.pallas_call(
        functools.partial(_diff_attn_kernel, n_k=n_k, lam_init=lam_init),
        grid_spec=grid_spec,
        out_shape=jax.ShapeDtypeStruct((B, S, DA_WIDTH), jnp.bfloat16),
        compiler_params=pltpu.CompilerParams(
            dimension_semantics=("arbitrary", "arbitrary", "arbitrary"),
            vmem_limit_bytes=VMEM_LIMIT_BYTES),
        name="diff_attn",
    )(jnp.asarray(slopes2, jnp.float32), jnp.asarray(1.0 / slopes2, jnp.float32),
      q_t, k_t, k_nsq.reshape(B, DA_HEADS, 2, ROW_TILE), k, v_t,
      vec(lam_q1), vec(lam_k1), vec(lam_q2), vec(lam_k2), subln_g.reshape(hw, 1))


def _win_attn_kernel(slope_ref, sink_ref, q_ref, k_ref, v_ref, o_ref, ot_ref, *, seq_len):
    tq, band = WG_TQ, WG_BAND
    n_vt = band // LANES
    rep = WG_HEADS // WG_KV_HEADS
    qi = pl.program_id(1)
    q_start = qi * tq
    tile0 = jnp.clip(q_start // LANES - WINDOW // LANES, 0, seq_len // LANES - n_vt)
    k_start = pl.multiple_of(tile0 * LANES, LANES)

    kband = k_ref[pl.ds(k_start, band), :]
    kpos = k_start + lax.broadcasted_iota(jnp.int32, (band, tq), 0)
    qpos = q_start + lax.broadcasted_iota(jnp.int32, (band, tq), 1)
    dist_i = jnp.abs(qpos - kpos)
    dist = jnp.where(dist_i <= WINDOW, dist_i.astype(jnp.float32), -NEG_BIG)

    def group(g, shift_by_max):
        rows = slice(g * WG_HEAD_DIM, (g + 1) * WG_HEAD_DIM)
        blocks = []
        for r in range(rep):
            hd = g * rep + r
            parts = []
            if g > 0:
                parts.append(jnp.zeros((g * WG_HEAD_DIM, tq), q_ref.dtype))
            parts.append(q_ref[hd * WG_HEAD_DIM:(hd + 1) * WG_HEAD_DIM, :])
            if g < WG_KV_HEADS - 1:
                parts.append(jnp.zeros(((WG_KV_HEADS - 1 - g) * WG_HEAD_DIM, tq), q_ref.dtype))
            blocks.append(jnp.concatenate(parts, axis=0))
        s = _dot(kband, jnp.concatenate(blocks, axis=1))
        es, denoms = [], []
        for r in range(rep):
            hd = g * rep + r
            sr = s[:, r * tq:(r + 1) * tq] - slope_ref[hd] * dist
            sk = jnp.full((1, tq), sink_ref[hd] * LOG2E, jnp.float32)
            if shift_by_max:
                m = jnp.maximum(jnp.max(sr, axis=0, keepdims=True), sk)
                sr, sk = sr - m, sk - m
            e = jnp.exp2(sr)
            denoms.append(jnp.sum(e, axis=0, keepdims=True) + jnp.exp2(sk))
            es.append(e.astype(jnp.bfloat16))
        vg = jnp.concatenate([v_ref[tile0 + t, rows, :] for t in range(n_vt)], axis=1)
        den = jnp.concatenate(denoms, axis=1)
        og = _dot(vg, jnp.concatenate(es, axis=1)) / den
        for r in range(rep):
            hd = g * rep + r
            ot_ref[hd * WG_HEAD_DIM:(hd + 1) * WG_HEAD_DIM, :] = og[:, r * tq:(r + 1) * tq]
        return jnp.sum(og * 0.0) + jnp.sum(jnp.where(den >= WG_MIN_DENOM, 0.0, 1.0))

    bad = group(0, False)
    for g in range(1, WG_KV_HEADS):
        bad = bad + group(g, False)

    @pl.when(jnp.logical_not(bad == 0.0))
    def _():
        for g in range(WG_KV_HEADS):
            group(g, True)

    o_ref[...] = ot_ref[...].T.astype(o_ref.dtype)


def _window_attention(q_t, k, v_t, sink_logit):
    B, _, S = q_t.shape
    slopes2 = jnp.asarray(_alibi_slopes(WG_HEADS) * LOG2E, jnp.float32)
    grid_spec = pltpu.PrefetchScalarGridSpec(
        num_scalar_prefetch=2,
        grid=(B, S // WG_TQ),
        in_specs=[
            pl.BlockSpec((None, WG_Q, WG_TQ), lambda b, i, s1, s2: (b, 0, i)),
            pl.BlockSpec((None, S, WG_KV), lambda b, i, s1, s2: (b, 0, 0)),
            pl.BlockSpec((None, S // LANES, WG_KV, LANES), lambda b, i, s1, s2: (b, 0, 0, 0)),
        ],
        out_specs=pl.BlockSpec((None, WG_TQ, WG_Q), lambda b, i, s1, s2: (b, i, 0)),
        scratch_shapes=[pltpu.VMEM((WG_Q, WG_TQ), jnp.float32)],
    )
    return pl.pallas_call(
        functools.partial(_win_attn_kernel, seq_len=S),
        grid_spec=grid_spec,
        out_shape=jax.ShapeDtypeStruct((B, S, WG_Q), jnp.bfloat16),
        compiler_params=pltpu.CompilerParams(
            dimension_semantics=("arbitrary", "arbitrary"), vmem_limit_bytes=VMEM_LIMIT_BYTES),
        name="win_attn",
    )(slopes2, sink_logit.astype(jnp.float32), q_t, k, v_t)


def _layer_norm(x, g, b):
    mu = jnp.mean(x, axis=-1, keepdims=True)
    xc = x - mu
    var = jnp.mean(xc * xc, axis=-1, keepdims=True)
    return xc * lax.rsqrt(var + LN_EPS) * g + b


def _merge_kernel(x_ref, oa_ref, ob_ref, wg_ref, bg_ref, wa_ref, wb_ref, wo_ref, g_ref, b_ref, y_ref):
    x = x_ref[...]
    gates = jax.nn.sigmoid(_dot(x.astype(jnp.bfloat16), wg_ref[...]) + bg_ref[...])
    merged = (gates[:, :D_MODEL] * _dot(oa_ref[...], wa_ref[...])
              + gates[:, D_MODEL:] * _dot(ob_ref[...], wb_ref[...]))
    mix = _dot(merged.astype(jnp.bfloat16), wo_ref[...])
    y_ref[...] = _layer_norm(DEEPNORM_ALPHA * x + mix, g_ref[...], b_ref[...])


def _merge(x, o_a, o_b, w_gate, b_gate, w_br_a, w_br_b, w_out, ln_g, ln_b):
    B, S, D = x.shape
    tok = lambda b, s: (b, s, 0)
    const = lambda b, s: (0, 0)
    return pl.pallas_call(
        _merge_kernel,
        grid=(B, S // ROW_TILE),
        in_specs=[
            pl.BlockSpec((None, ROW_TILE, D), tok),
            pl.BlockSpec((None, ROW_TILE, DA_WIDTH), tok),
            pl.BlockSpec((None, ROW_TILE, WG_Q), tok),
            pl.BlockSpec((D, 2 * D), const),
            pl.BlockSpec((1, 2 * D), const),
            pl.BlockSpec((DA_WIDTH, D), const),
            pl.BlockSpec((WG_Q, D), const),
            pl.BlockSpec((D, D), const),
            pl.BlockSpec((1, D), const),
            pl.BlockSpec((1, D), const),
        ],
        out_specs=pl.BlockSpec((None, ROW_TILE, D), tok),
        out_shape=jax.ShapeDtypeStruct((B, S, D), jnp.float32),
        compiler_params=pltpu.CompilerParams(
            dimension_semantics=("arbitrary", "arbitrary"), vmem_limit_bytes=VMEM_LIMIT_BYTES),
        name="merge_ln1",
    )(x, o_a, o_b, w_gate, b_gate, w_br_a, w_br_b, w_out, ln_g, ln_b)


def _ffn_kernel(x_ref, w1_ref, b1_ref, w2_ref, b2_ref, g_ref, b_ref, y_ref):
    x = x_ref[...]
    h = jnp.maximum(_dot(x.astype(jnp.bfloat16), w1_ref[...]) + b1_ref[...], 0.0)
    f = _dot((h * h).astype(jnp.bfloat16), w2_ref[...]) + b2_ref[...]
    y_ref[...] = _layer_norm(DEEPNORM_ALPHA * x + f, g_ref[...], b_ref[...])


def _ffn(x, w1, b1, w2, b2, ln_g, ln_b):
    B, S, D = x.shape
    tok = lambda b, s: (b, s, 0)
    const = lambda b, s: (0, 0)
    return pl.pallas_call(
        _ffn_kernel,
        grid=(B, S // ROW_TILE),
        in_specs=[
            pl.BlockSpec((None, ROW_TILE, D), tok),
            pl.BlockSpec((D, D_FF), const),
            pl.BlockSpec((1, D_FF), const),
            pl.BlockSpec((D_FF, D), const),
            pl.BlockSpec((1, D), const),
            pl.BlockSpec((1, D), const),
            pl.BlockSpec((1, D), const),
        ],
        out_specs=pl.BlockSpec((None, ROW_TILE, D), tok),
        out_shape=jax.ShapeDtypeStruct((B, S, D), jnp.float32),
        compiler_params=pltpu.CompilerParams(
            dimension_semantics=("arbitrary", "arbitrary"), vmem_limit_bytes=VMEM_LIMIT_BYTES),
        name="ffn_ln2",
    )(x, w1, b1, w2, b2, ln_g, ln_b)


def _encoder_layer(x, l, w_in, b_gate, lam_q1, lam_k1, lam_q2, lam_k2, subln_g, sink_logit,
                   w_br_a, w_br_b, w_out, ln1_g, ln1_b, w_ff1, b_ff1, w_ff2, b_ff2, ln2_g, ln2_b):
    bf = jnp.bfloat16
    row = lambda a: a.reshape(1, -1)
    w_t = jnp.concatenate([w_in[:, OFF_DA_Q:OFF_DA_K], w_in[:, OFF_DA_V:OFF_WG_Q],
                           w_in[:, OFF_WG_Q:OFF_WG_K], w_in[:, OFF_WG_V:OFF_GATE],
                           w_in[:, OFF_DA_K:OFF_DA_V]], axis=1).T.astype(bf)
    w_ka = w_in[:, OFF_DA_K:OFF_DA_V].astype(bf)
    w_kb = w_in[:, OFF_WG_K:OFF_WG_V].astype(bf)
    w_gate = w_in[:, OFF_GATE:].astype(bf)
    lam_init = 0.8 - 0.6 * math.exp(-0.3 * l)

    qa_t, va_t, qb_t, vb_t, ka_t, k_a, k_b, ka_nsq = _project(x, w_t, w_ka, w_kb)
    o_a = _diff_attention(qa_t, ka_t, ka_nsq, k_a, va_t, lam_q1, lam_k1, lam_q2, lam_k2, subln_g, lam_init)
    o_b = _window_attention(qb_t, k_b, vb_t, sink_logit)
    x1 = _merge(x, o_a, o_b, w_gate, row(b_gate), w_br_a.astype(bf), w_br_b.astype(bf),
                w_out.astype(bf), row(ln1_g), row(ln1_b))
    return _ffn(x1, w_ff1.astype(bf), row(b_ff1), w_ff2.astype(bf), row(b_ff2), row(ln2_g), row(ln2_b))


def kernel(x_prompt, x_sample, w_in, b_gate, lam_q1, lam_k1, lam_q2, lam_k2, subln_g, sink_logit,
           w_br_a, w_br_b, w_out, ln1_g, ln1_b, w_ff1, b_ff1, w_ff2, b_ff2, ln2_g, ln2_b):
    def run(x):
        for l in range(DEPTH):
            x = _encoder_layer(x, l, w_in[l], b_gate[l], lam_q1[l], lam_k1[l], lam_q2[l], lam_k2[l],
                               subln_g[l], sink_logit[l], w_br_a[l], w_br_b[l], w_out[l],
                               ln1_g[l], ln1_b[l], w_ff1[l], b_ff1[l], w_ff2[l], b_ff2[l],
                               ln2_g[l], ln2_b[l])
        return x

    return (run(x_prompt), run(x_sample))
```

```python
import functools
import math

import jax
import jax.numpy as jnp
import numpy as np
from jax import lax
from jax.experimental import pallas as pl
from jax.experimental.pallas import tpu as pltpu

D_MODEL = 1024
DA_HEADS = 8
DA_HEAD_DIM = 64
DA_WIDTH = DA_HEADS * 2 * DA_HEAD_DIM
WG_HEADS = 16
WG_KV_HEADS = 4
WG_HEAD_DIM = 64
WG_Q = WG_HEADS * WG_HEAD_DIM
WG_KV = WG_KV_HEADS * WG_HEAD_DIM
WINDOW = 128
D_FF = 4 * D_MODEL
DEPTH = 1
DEEPNORM_ALPHA = (2.0 * DEPTH) ** 0.25
LN_EPS = 1e-5
LOG2E = math.log2(math.e)
NEG_BIG = -1e30

OFF_DA_Q = 0
OFF_DA_K = OFF_DA_Q + DA_WIDTH
OFF_DA_V = OFF_DA_K + DA_WIDTH
OFF_WG_Q = OFF_DA_V + DA_WIDTH
OFF_WG_K = OFF_WG_Q + WG_Q
OFF_WG_V = OFF_WG_K + WG_KV
OFF_GATE = OFF_WG_V + WG_KV

LANES = 128
SUBLANES = 8
BF16_ROWS = 16
VMEM_LIMIT_BYTES = 56 * 1024 * 1024

ROW_TILE = 512
DA_TQ = 512
DA_TK = ROW_TILE
WG_TQ = 256
WG_BAND = WG_TQ + 2 * WINDOW
WG_MIN_DENOM = 2.0 ** -60

DA_KCOLS = 2 * DA_HEAD_DIM
AUG_ROWS = BF16_ROWS
SKIP_LOG2 = -150.0


def _alibi_slopes(n_heads):
    return 2.0 ** (-8.0 * np.arange(1, n_heads + 1) / n_heads)


def _nt_dot(a, b):
    return lax.dot_general(a, b, (((1,), (1,)), ((), ())), preferred_element_type=jnp.float32)


def _dot(a, b):
    return jnp.dot(a, b, preferred_element_type=jnp.float32)


def _proj_kernel(x_ref, wt_ref, wka_ref, wkb_ref,
                 qa_ref, va_ref, qb_ref, vb_ref, kat_ref, ka_ref, kb_ref, knsq_ref):
    xb = x_ref[...].astype(jnp.bfloat16)
    da_scale = DA_HEAD_DIM ** -0.5 * LOG2E
    wg_scale = WG_HEAD_DIM ** -0.5 * LOG2E
    r0, r1, r2, r3 = DA_WIDTH, 2 * DA_WIDTH, 2 * DA_WIDTH + WG_Q, 2 * DA_WIDTH + WG_Q + WG_KV
    qa_ref[...] = (_nt_dot(wt_ref[0:r0, :], xb) * da_scale).astype(jnp.bfloat16)
    va_ref[...] = _nt_dot(wt_ref[r0:r1, :], xb).astype(jnp.bfloat16)
    qb_ref[...] = (_nt_dot(wt_ref[r1:r2, :], xb) * wg_scale).astype(jnp.bfloat16)
    kat = _nt_dot(wt_ref[r3:, :], xb)
    kat_ref[...] = kat.astype(jnp.bfloat16)
    nsq = jnp.sum((kat * kat).reshape(2 * DA_HEADS, DA_HEAD_DIM, ROW_TILE), axis=1)
    vb = _nt_dot(wt_ref[r2:r3, :], xb).astype(jnp.bfloat16)
    for t in range(ROW_TILE // LANES):
        vb_ref[t] = vb[:, t * LANES:(t + 1) * LANES]
    kb_ref[...] = _dot(xb, wkb_ref[...]).astype(jnp.bfloat16)
    pos = pl.program_id(1) * ROW_TILE + lax.broadcasted_iota(jnp.int32, (ROW_TILE, DA_KCOLS), 0)
    lane = lax.broadcasted_iota(jnp.int32, (ROW_TILE, DA_KCOLS), 1)
    pos_hi = ((pos // LANES) * LANES).astype(jnp.float32)
    pos_lo = (pos % LANES).astype(jnp.float32)
    lower = lane < DA_HEAD_DIM

    def position_cols(c0):
        return jnp.where(lane < c0, 0.0,
                         jnp.where(lane < c0 + 3, 1.0,
                                   jnp.where(lane < c0 + 6, pos_hi,
                                             jnp.where(lane < c0 + 9, pos_lo, 0.0))))
    pat0, pat1 = position_cols(DA_HEAD_DIM), position_cols(0)
    for hd in range(DA_HEADS):
        kk = _dot(xb, wka_ref[:, hd * DA_KCOLS:(hd + 1) * DA_KCOLS])
        ka_ref[:, (2 * hd) * DA_KCOLS:(2 * hd + 1) * DA_KCOLS] = jnp.where(lower, kk, pat0).astype(jnp.bfloat16)
        ka_ref[:, (2 * hd + 1) * DA_KCOLS:(2 * hd + 2) * DA_KCOLS] = jnp.where(lower, pat1, kk).astype(jnp.bfloat16)

    @pl.when(pl.program_id(1) == 0)
    def _():
        knsq_ref[...] = nsq

    @pl.when(pl.program_id(1) > 0)
    def _():
        knsq_ref[...] = jnp.maximum(knsq_ref[...], nsq)


def _project(x, w_t, w_ka, w_kb):
    B, S, D = x.shape
    ns = S // ROW_TILE
    rows_t = w_t.shape[0]
    ka_cols = 2 * DA_HEADS * DA_KCOLS
    bf = jnp.bfloat16
    return pl.pallas_call(
        _proj_kernel,
        grid=(B, ns),
        in_specs=[
            pl.BlockSpec((None, ROW_TILE, D), lambda b, s: (b, s, 0)),
            pl.BlockSpec((rows_t, D), lambda b, s: (0, 0)),
            pl.BlockSpec((D, DA_WIDTH), lambda b, s: (0, 0)),
            pl.BlockSpec((D, WG_KV), lambda b, s: (0, 0)),
        ],
        out_specs=[
            pl.BlockSpec((None, DA_WIDTH, ROW_TILE), lambda b, s: (b, 0, s)),
            pl.BlockSpec((None, None, DA_WIDTH, ROW_TILE), lambda b, s: (b, s, 0, 0)),
            pl.BlockSpec((None, WG_Q, ROW_TILE), lambda b, s: (b, 0, s)),
            pl.BlockSpec((None, ROW_TILE // LANES, WG_KV, LANES), lambda b, s: (b, s, 0, 0)),
            pl.BlockSpec((None, DA_WIDTH, ROW_TILE), lambda b, s: (b, 0, s)),
            pl.BlockSpec((None, ROW_TILE, ka_cols), lambda b, s: (b, s, 0)),
            pl.BlockSpec((None, ROW_TILE, WG_KV), lambda b, s: (b, s, 0)),
            pl.BlockSpec((None, 2 * DA_HEADS, ROW_TILE), lambda b, s: (b, 0, 0)),
        ],
        out_shape=[
            jax.ShapeDtypeStruct((B, DA_WIDTH, S), bf),
            jax.ShapeDtypeStruct((B, ns, DA_WIDTH, ROW_TILE), bf),
            jax.ShapeDtypeStruct((B, WG_Q, S), bf),
            jax.ShapeDtypeStruct((B, S // LANES, WG_KV, LANES), bf),
            jax.ShapeDtypeStruct((B, DA_WIDTH, S), bf),
            jax.ShapeDtypeStruct((B, S, ka_cols), bf),
            jax.ShapeDtypeStruct((B, S, WG_KV), bf),
            jax.ShapeDtypeStruct((B, 2 * DA_HEADS, ROW_TILE), jnp.float32),
        ],
        compiler_params=pltpu.CompilerParams(
            dimension_semantics=("arbitrary", "arbitrary"), vmem_limit_bytes=VMEM_LIMIT_BYTES),
        name="in_proj",
    )(x, w_t, w_ka, w_kb)


def _split3(x):
    hi = x.astype(jnp.bfloat16).astype(jnp.float32)
    r1 = x - hi
    mid = r1.astype(jnp.bfloat16).astype(jnp.float32)
    return hi, mid, r1 - mid


def _diff_attn_kernel(slope_ref, islope_ref, q_ref, kt_ref, kn_ref, k_ref, v_ref,
                      lq1_ref, lk1_ref, lq2_ref, lk2_ref, g_ref,
                      o_ref, bias_ref, qaug_ref, ea_ref, eb_ref, acc_ref,
                      *, n_k, lam_init):
    tq, tk, dh = DA_TQ, DA_TK, DA_HEAD_DIM
    h = pl.program_id(1)
    qi = pl.program_id(2)
    sig = slope_ref[h]

    def k_tile(kt):
        return k_ref[pl.ds(pl.multiple_of(kt * tk, tk), tk), :]

    def col_sum8(e):
        return jnp.sum(e.reshape(tk // SUBLANES, SUBLANES, e.shape[1]), axis=0)

    def both_maps(m0, m1):
        z = jnp.zeros_like(m0)
        return jnp.concatenate([jnp.concatenate([m0, z], axis=1), jnp.concatenate([z, m1], axis=1)], axis=0)

    @pl.when(qi == 0)
    def _():
        dd = (lax.broadcasted_iota(jnp.int32, (tk, tq), 1) - lax.broadcasted_iota(jnp.int32, (tk, tq), 0))
        b = -sig * jnp.abs(dd).astype(jnp.float32)
        bias_ref[...] = jnp.concatenate([b, b], axis=1)

    q = q_ref[...]
    qc = (q[:dh], q[dh:])
    zpad = jnp.zeros((DA_KCOLS - dh, tq), q.dtype)
    qpad = both_maps(jnp.concatenate([qc[0], zpad], axis=0),
                     jnp.concatenate([zpad, qc[1]], axis=0))

    q_start = qi * tq
    kt_ov = q_start // tk

    qk_self = q.astype(jnp.float32) * kt_ref[...].astype(jnp.float32)
    r = jnp.concatenate([jnp.sum(qk_self[:dh], axis=0, keepdims=True),
                         jnp.sum(qk_self[dh:], axis=0, keepdims=True)], axis=1)

    k_norm = jnp.sqrt(jnp.max(kn_ref[...], axis=1, keepdims=True)) * 1.01
    q_sq = q.astype(jnp.float32) ** 2
    bound = jnp.concatenate([jnp.sqrt(jnp.sum(q_sq[:dh], axis=0, keepdims=True)) * k_norm[0:1],
                             jnp.sqrt(jnp.sum(q_sq[dh:], axis=0, keepdims=True)) * k_norm[1:2]], axis=1)
    gap = jnp.max(bound - r, axis=1, keepdims=True)
    reach = jnp.minimum((gap - SKIP_LOG2) * islope_ref[h], float(2 * n_k * tk))
    qs_f = q_start.astype(jnp.float32)
    lo_f = jnp.floor((qs_f - reach - 1.0) * (1.0 / tk))
    hi_f = jnp.floor((qs_f + float(tq) + reach) * (1.0 / tk))
    lo_t = jnp.minimum(jnp.clip(lo_f, 0.0, float(n_k)).astype(jnp.int32)[0, 0], kt_ov)
    hi_t = jnp.maximum(jnp.clip(hi_f, -1.0, float(n_k - 1)).astype(jnp.int32)[0, 0], kt_ov)
    n_left = kt_ov - lo_t
    n_tot = n_left + hi_t - kt_ov

    ipos = (q_start + lax.broadcasted_iota(jnp.int32, (1, tq), 1)).astype(jnp.float32)
    row = lax.broadcasted_iota(jnp.int32, (AUG_ROWS, tq), 0)
    sig_row = jnp.full((1, tq), sig, jnp.float32)
    zrest = jnp.zeros((DA_KCOLS - dh - AUG_ROWS, tq), q.dtype)
    for side, sgn in enumerate((1.0, -1.0, 0.0)):
        s3 = _split3(sgn * sig_row)
        maps = []
        for c in range(2):
            r3 = _split3(-(r[:, c * tq:(c + 1) * tq] + sgn * sig * ipos))
            slab = jnp.zeros((AUG_ROWS, tq), jnp.float32)
            for i, piece in enumerate(r3 + s3 + s3):
                slab = jnp.where(row == i, piece, slab)
            parts = [qc[c], slab.astype(q.dtype), zrest] if c == 0 else [slab.astype(q.dtype), zrest, qc[c]]
            maps.append(jnp.concatenate(parts, axis=0))
        qaug_ref[side] = both_maps(*maps)

    def tile_of(i):
        i = jnp.clip(i, 0, jnp.maximum(n_tot - 1, 0))
        kt = jnp.where(i < n_left, lo_t + i, kt_ov + 1 + i - n_left)
        return jnp.minimum(kt, n_k - 1), (i >= n_left).astype(jnp.int32)

    def scores(i):
        kt, side = tile_of(i)
        return _dot(k_tile(kt), qaug_ref[side])

    def exp_scores(i, e_ref):
        e = jnp.exp2(scores(i))
        e_ref[...] = e.astype(e_ref.dtype)
        return col_sum8(e)

    def add_av(i, e_ref):
        kt, _ = tile_of(i)
        acc_ref[...] += _dot(v_ref[kt], e_ref[...])

    e = jnp.exp2(_dot(k_tile(kt_ov), qaug_ref[2]) + bias_ref[...])
    pend = exp_scores(0, ea_ref)
    l8 = col_sum8(e)
    acc_ref[...] = _dot(v_ref[kt_ov], e.astype(jnp.bfloat16))

    def pair(p, carry):
        l8, pend = carry
        i = 2 * p
        pend_b = exp_scores(i + 1, eb_ref)
        add_av(i, ea_ref)
        pend_a = exp_scores(i + 2, ea_ref)
        add_av(i + 1, eb_ref)
        return l8 + pend + pend_b, pend_a

    ends_even = jnp.logical_and(n_tot > 0, n_tot % 2 == 0)
    n_pairs = n_tot // 2 - ends_even.astype(jnp.int32)

    def pairs(first, count, c):
        for j in range(count):
            c = pair(first + j, c)
        return c

    def last_tile(c):
        add_av(n_tot - 1, ea_ref)
        return c[0] + c[1]

    def last_two_tiles(c):
        pend_b = exp_scores(n_tot - 1, eb_ref)
        add_av(n_tot - 2, ea_ref)
        add_av(n_tot - 1, eb_ref)
        return c[0] + c[1] + pend_b

    carry = lax.fori_loop(0, n_pairs // 4, lambda t, c: pairs(4 * t, 4, c), (l8, pend))
    done = 4 * (n_pairs // 4)

    def seven_left(c):
        return last_tile(pairs(done, 3, c))

    def other_left(c):
        c = lax.cond(n_pairs % 4 >= 2, lambda c: pairs(done, 2, c), lambda c: c, c)
        c = lax.cond(n_pairs % 2 == 1, lambda c: pair(done + 2 * ((n_pairs % 4) // 2), c), lambda c: c, c)
        l8 = lax.cond(n_tot % 2 == 1, last_tile, lambda c: c[0], c)
        return lax.cond(ends_even, last_two_tiles, lambda c: c[0], (l8, c[1]))

    l8 = lax.cond(jnp.logical_and(n_pairs % 4 == 3, n_tot % 2 == 1), seven_left, other_left, carry)
    l = jnp.sum(l8, axis=0, keepdims=True)

    lam = (jnp.exp(jnp.sum(lq1_ref[...] * lk1_ref[...], axis=1, keepdims=True))
           - jnp.exp(jnp.sum(lq2_ref[...] * lk2_ref[...], axis=1, keepdims=True)) + lam_init)

    def finish(l):
        on = acc_ref[...] / l
        o = on[:, :tq] - lam * on[:, tq:]
        o = o * lax.rsqrt(jnp.mean(o * o, axis=0, keepdims=True) + LN_EPS)
        o = o * g_ref[...] * (1.0 - lam_init)
        o_ref[...] = o.T.astype(o_ref.dtype)

    finish(l)

    chk = jnp.sum(acc_ref[...] * 0.0) + jnp.sum(l * 0.0)
    overflowed = jnp.logical_not(chk == 0.0)

    def exact_path():
        dd = (lax.broadcasted_iota(jnp.int32, (tk, tq), 1)
              - lax.broadcasted_iota(jnp.int32, (tk, tq), 0)).astype(jnp.float32)
        acc_ref[...] = jnp.zeros_like(acc_ref)

        def body(kt, carry):
            m, lc = carry
            b = -sig * jnp.abs(dd + (q_start - kt * tk).astype(jnp.float32))
            u = _dot(k_tile(kt), qpad) + jnp.concatenate([b, b], axis=1)
            m_new = jnp.maximum(m, jnp.max(u, axis=0, keepdims=True))
            e = jnp.exp2(u - m_new)
            alpha = jnp.exp2(m - m_new)
            acc_ref[...] = alpha * acc_ref[...] + _dot(v_ref[kt], e.astype(jnp.bfloat16))
            return m_new, alpha * lc + jnp.sum(e, axis=0, keepdims=True)

        init = (jnp.full((1, 2 * tq), NEG_BIG, jnp.float32), jnp.zeros((1, 2 * tq), jnp.float32))
        finish(lax.fori_loop(0, n_k, body, init)[1])

    pl.when(overflowed)(exact_path)


def _diff_attention(q_t, k_t, k_nsq, k, v_t, lam_q1, lam_k1, lam_q2, lam_k2, subln_g, lam_init):
    B, _, S = q_t.shape
    assert DA_TQ == DA_TK, "the diagonal-tile bias assumes square tiles"
    hw = 2 * DA_HEAD_DIM
    n_k = S // DA_TK
    slopes2 = _alibi_slopes(DA_HEADS) * LOG2E
    vec = lambda a: a.reshape(1, DA_HEAD_DIM)
    small = pl.BlockSpec((1, DA_HEAD_DIM), lambda b, h, i, s1, s2: (0, 0))
    grid_spec = pltpu.PrefetchScalarGridSpec(
        num_scalar_prefetch=2,
        grid=(B, DA_HEADS, S // DA_TQ),
        in_specs=[
            pl.BlockSpec((None, hw, DA_TQ), lambda b, h, i, s1, s2: (b, h, i)),
            pl.BlockSpec((None, hw, DA_TQ), lambda b, h, i, s1, s2: (b, h, i)),
            pl.BlockSpec((None, None, 2, ROW_TILE), lambda b, h, i, s1, s2: (b, h, 0, 0)),
            pl.BlockSpec((None, S, 2 * DA_KCOLS), lambda b, h, i, s1, s2: (b, 0, h)),
            pl.BlockSpec((None, n_k, hw, DA_TK), lambda b, h, i, s1, s2: (b, 0, h, 0)),
            small, small, small, small,
            pl.BlockSpec((hw, 1), lambda b, h, i, s1, s2: (0, 0)),
        ],
        out_specs=pl.BlockSpec((None, DA_TQ, hw), lambda b, h, i, s1, s2: (b, i, h)),
        scratch_shapes=[
            pltpu.VMEM((DA_TK, 2 * DA_TQ), jnp.float32),
            pltpu.VMEM((3, 2 * DA_KCOLS, 2 * DA_TQ), jnp.bfloat16),
            pltpu.VMEM((DA_TK, 2 * DA_TQ), jnp.bfloat16),
            pltpu.VMEM((DA_TK, 2 * DA_TQ), jnp.bfloat16),
            pltpu.VMEM((hw, 2 * DA_TQ), jnp.float32),
        ],
    )
    return pl.pallas_call(
        functools.partial(_diff_attn_kernel, n_k=n_k, lam_init=lam_init),
        grid_spec=grid_spec,
        out_shape=jax.ShapeDtypeStruct((B, S, DA_WIDTH), jnp.bfloat16),
        compiler_params=pltpu.CompilerParams(
            dimension_semantics=("arbitrary", "arbitrary", "arbitrary"),
            vmem_limit_bytes=VMEM_LIMIT_BYTES),
        name="diff_attn",
    )(jnp.asarray(slopes2, jnp.float32), jnp.asarray(1.0 / slopes2, jnp.float32),
      q_t, k_t, k_nsq.reshape(B, DA_HEADS, 2, ROW_TILE), k, v_t,
      vec(lam_q1), vec(lam_k1), vec(lam_q2), vec(lam_k2), subln_g.reshape(hw, 1))


def _win_attn_kernel(slope_ref, sink_ref, q_ref, k_ref, v_ref, o_ref, ot_ref, *, seq_len):
    tq, band = WG_TQ, WG_BAND
    n_vt = band // LANES
    rep = WG_HEADS // WG_KV_HEADS
    qi = pl.program_id(1)
    q_start = qi * tq
    tile0 = jnp.clip(q_start // LANES - WINDOW // LANES, 0, seq_len // LANES - n_vt)
    k_start = pl.multiple_of(tile0 * LANES, LANES)

    kband = k_ref[pl.ds(k_start, band), :]
    kpos = k_start + lax.broadcasted_iota(jnp.int32, (band, tq), 0)
    qpos = q_start + lax.broadcasted_iota(jnp.int32, (band, tq), 1)
    dist_i = jnp.abs(qpos - kpos)
    dist = jnp.where(dist_i <= WINDOW, dist_i.astype(jnp.float32), -NEG_BIG)

    def group(g, shift_by_max):
        rows = slice(g * WG_HEAD_DIM, (g + 1) * WG_HEAD_DIM)
        blocks = []
        for r in range(rep):
            hd = g * rep + r
            parts = []
            if g > 0:
                parts.append(jnp.zeros((g * WG_HEAD_DIM, tq), q_ref.dtype))
            parts.append(q_ref[hd * WG_HEAD_DIM:(hd + 1) * WG_HEAD_DIM, :])
            if g < WG_KV_HEADS - 1:
                parts.append(jnp.zeros(((WG_KV_HEADS - 1 - g) * WG_HEAD_DIM, tq), q_ref.dtype))
            blocks.append(jnp.concatenate(parts, axis=0))
        s = _dot(kband, jnp.concatenate(blocks, axis=1))
        es, denoms = [], []
        for r in range(rep):
            hd = g * rep + r
            sr = s[:, r * tq:(r + 1) * tq] - slope_ref[hd] * dist
            sk = jnp.full((1, tq), sink_ref[hd] * LOG2E, jnp.float32)
            if shift_by_max:
                m = jnp.maximum(jnp.max(sr, axis=0, keepdims=True), sk)
                sr, sk = sr - m, sk - m
            e = jnp.exp2(sr)
            denoms.append(jnp.sum(e, axis=0, keepdims=True) + jnp.exp2(sk))
            es.append(e.astype(jnp.bfloat16))
        vg = jnp.concatenate([v_ref[tile0 + t, rows, :] for t in range(n_vt)], axis=1)
        den = jnp.concatenate(denoms, axis=1)
        og = _dot(vg, jnp.concatenate(es, axis=1)) / den
        for r in range(rep):
            hd = g * rep + r
            ot_ref[hd * WG_HEAD_DIM:(hd + 1) * WG_HEAD_DIM, :] = og[:, r * tq:(r + 1) * tq]
        return jnp.sum(og * 0.0) + jnp.sum(jnp.where(den >= WG_MIN_DENOM, 0.0, 1.0))

    bad = group(0, False)
    for g in range(1, WG_KV_HEADS):
        bad = bad + group(g, False)
    o_ref[...] = ot_ref[...].T.astype(o_ref.dtype)

    @pl.when(jnp.logical_not(bad == 0.0))
    def _():
        for g in range(WG_KV_HEADS):
            group(g, True)
        o_ref[...] = ot_ref[...].T.astype(o_ref.dtype)


def _window_attention(q_t, k, v_t, sink_logit):
    B, _, S = q_t.shape
    slopes2 = jnp.asarray(_alibi_slopes(WG_HEADS) * LOG2E, jnp.float32)
    grid_spec = pltpu.PrefetchScalarGridSpec(
        num_scalar_prefetch=2,
        grid=(B, S // WG_TQ),
        in_specs=[
            pl.BlockSpec((None, WG_Q, WG_TQ), lambda b, i, s1, s2: (b, 0, i)),
            pl.BlockSpec((None, S, WG_KV), lambda b, i, s1, s2: (b, 0, 0)),
            pl.BlockSpec((None, S // LANES, WG_KV, LANES), lambda b, i, s1, s2: (b, 0, 0, 0)),
        ],
        out_specs=pl.BlockSpec((None, WG_TQ, WG_Q), lambda b, i, s1, s2: (b, i, 0)),
        scratch_shapes=[pltpu.VMEM((WG_Q, WG_TQ), jnp.float32)],
    )
    return pl.pallas_call(
        functools.partial(_win_attn_kernel, seq_len=S),
        grid_spec=grid_spec,
        out_shape=jax.ShapeDtypeStruct((B, S, WG_Q), jnp.bfloat16),
        compiler_params=pltpu.CompilerParams(
            dimension_semantics=("arbitrary", "arbitrary"), vmem_limit_bytes=VMEM_LIMIT_BYTES),
        name="win_attn",
    )(slopes2, sink_logit.astype(jnp.float32), q_t, k, v_t)


def _layer_norm(x, g, b):
    mu = jnp.mean(x, axis=-1, keepdims=True)
    xc = x - mu
    var = jnp.mean(xc * xc, axis=-1, keepdims=True)
    return xc * lax.rsqrt(var + LN_EPS) * g + b


def _merge_kernel(x_ref, oa_ref, ob_ref, wg_ref, bg_ref, wa_ref, wb_ref, wo_ref, g_ref, b_ref, y_ref):
    x = x_ref[...]
    gates = jax.nn.sigmoid(_dot(x.astype(jnp.bfloat16), wg_ref[...]) + bg_ref[...])
    merged = (gates[:, :D_MODEL] * _dot(oa_ref[...], wa_ref[...])
              + gates[:, D_MODEL:] * _dot(ob_ref[...], wb_ref[...]))
    mix = _dot(merged.astype(jnp.bfloat16), wo_ref[...])
    y_ref[...] = _layer_norm(DEEPNORM_ALPHA * x + mix, g_ref[...], b_ref[...])


def _merge(x, o_a, o_b, w_gate, b_gate, w_br_a, w_br_b, w_out, ln_g, ln_b):
    B, S, D = x.shape
    tok = lambda b, s: (b, s, 0)
    const = lambda b, s: (0, 0)
    return pl.pallas_call(
        _merge_kernel,
        grid=(B, S // ROW_TILE),
        in_specs=[
            pl.BlockSpec((None, ROW_TILE, D), tok),
            pl.BlockSpec((None, ROW_TILE, DA_WIDTH), tok),
            pl.BlockSpec((None, ROW_TILE, WG_Q), tok),
            pl.BlockSpec((D, 2 * D), const),
            pl.BlockSpec((1, 2 * D), const),
            pl.BlockSpec((DA_WIDTH, D), const),
            pl.BlockSpec((WG_Q, D), const),
            pl.BlockSpec((D, D), const),
            pl.BlockSpec((1, D), const),
            pl.BlockSpec((1, D), const),
        ],
        out_specs=pl.BlockSpec((None, ROW_TILE, D), tok),
        out_shape=jax.ShapeDtypeStruct((B, S, D), jnp.float32),
        compiler_params=pltpu.CompilerParams(
            dimension_semantics=("arbitrary", "arbitrary"), vmem_limit_bytes=VMEM_LIMIT_BYTES),
        name="merge_ln1",
    )(x, o_a, o_b, w_gate, b_gate, w_br_a, w_br_b, w_out, ln_g, ln_b)


def _ffn_kernel(x_ref, w1_ref, b1_ref, w2_ref, b2_ref, g_ref, b_ref, y_ref):
    x = x_ref[...]
    h = jnp.maximum(_dot(x.astype(jnp.bfloat16), w1_ref[...]) + b1_ref[...], 0.0)
    f = _dot((h * h).astype(jnp.bfloat16), w2_ref[...]) + b2_ref[...]
    y_ref[...] = _layer_norm(DEEPNORM_ALPHA * x + f, g_ref[...], b_ref[...])


def _ffn(x, w1, b1, w2, b2, ln_g, ln_b):
    B, S, D = x.shape
    tok = lambda b, s: (b, s, 0)
    const = lambda b, s: (0, 0)
    return pl.pallas_call(
        _ffn_kernel,
        grid=(B, S // ROW_TILE),
        in_specs=[
            pl.BlockSpec((None, ROW_TILE, D), tok),
            pl.BlockSpec((D, D_FF), const),
            pl.BlockSpec((1, D_FF), const),
            pl.BlockSpec((D_FF, D), const),
            pl.BlockSpec((1, D), const),
            pl.BlockSpec((1, D), const),
            pl.BlockSpec((1, D), const),
        ],
        out_specs=pl.BlockSpec((None, ROW_TILE, D), tok),
        out_shape=jax.ShapeDtypeStruct((B, S, D), jnp.float32),
        compiler_params=pltpu.CompilerParams(
            dimension_semantics=("arbitrary", "arbitrary"), vmem_limit_bytes=VMEM_LIMIT_BYTES),
        name="ffn_ln2",
    )(x, w1, b1, w2, b2, ln_g, ln_b)


def _encoder_layer(x, l, w_in, b_gate, lam_q1, lam_k1, lam_q2, lam_k2, subln_g, sink_logit,
                   w_br_a, w_br_b, w_out, ln1_g, ln1_b, w_ff1, b_ff1, w_ff2, b_ff2, ln2_g, ln2_b):
    bf = jnp.bfloat16
    row = lambda a: a.reshape(1, -1)
    w_t = jnp.concatenate([w_in[:, OFF_DA_Q:OFF_DA_K], w_in[:, OFF_DA_V:OFF_WG_Q],
                           w_in[:, OFF_WG_Q:OFF_WG_K], w_in[:, OFF_WG_V:OFF_GATE],
                           w_in[:, OFF_DA_K:OFF_DA_V]], axis=1).T.astype(bf)
    w_ka = w_in[:, OFF_DA_K:OFF_DA_V].astype(bf)
    w_kb = w_in[:, OFF_WG_K:OFF_WG_V].astype(bf)
    w_gate = w_in[:, OFF_GATE:].astype(bf)
    lam_init = 0.8 - 0.6 * math.exp(-0.3 * l)

    qa_t, va_t, qb_t, vb_t, ka_t, k_a, k_b, ka_nsq = _project(x, w_t, w_ka, w_kb)
    o_a = _diff_attention(qa_t, ka_t, ka_nsq, k_a, va_t, lam_q1, lam_k1, lam_q2, lam_k2, subln_g, lam_init)
    o_b = _window_attention(qb_t, k_b, vb_t, sink_logit)
    x1 = _merge(x, o_a, o_b, w_gate, row(b_gate), w_br_a.astype(bf), w_br_b.astype(bf),
                w_out.astype(bf), row(ln1_g), row(ln1_b))
    return _ffn(x1, w_ff1.astype(bf), row(b_ff1), w_ff2.astype(bf), row(b_ff2), row(ln2_g), row(ln2_b))


def kernel(x_prompt, x_sample, w_in, b_gate, lam_q1, lam_k1, lam_q2, lam_k2, subln_g, sink_logit,
           w_br_a, w_br_b, w_out, ln1_g, ln1_b, w_ff1, b_ff1, w_ff2, b_ff2, ln2_g, ln2_b):
    def run(x):
        for l in range(DEPTH):
            x = _encoder_layer(x, l, w_in[l], b_gate[l], lam_q1[l], lam_k1[l], lam_q2[l], lam_k2[l],
                               subln_g[l], sink_logit[l], w_br_a[l], w_br_b[l], w_out[l],
                               ln1_g[l], ln1_b[l], w_ff1[l], b_ff1[l], w_ff2[l], b_ff2[l],
                               ln2_g[l], ln2_b[l])
        return x

    return (run(x_prompt), run(x_sample))
```

```python
import functools
import math

import jax
import jax.numpy as jnp
import numpy as np
from jax import lax
from jax.experimental import pallas as pl
from jax.experimental.pallas import tpu as pltpu

D_MODEL = 1024
DA_HEADS = 8
DA_HEAD_DIM = 64
DA_WIDTH = DA_HEADS * 2 * DA_HEAD_DIM
WG_HEADS = 16
WG_KV_HEADS = 4
WG_HEAD_DIM = 64
WG_Q = WG_HEADS * WG_HEAD_DIM
WG_KV = WG_KV_HEADS * WG_HEAD_DIM
WINDOW = 128
D_FF = 4 * D_MODEL
DEPTH = 1
DEEPNORM_ALPHA = (2.0 * DEPTH) ** 0.25
LN_EPS = 1e-5
LOG2E = math.log2(math.e)
NEG_BIG = -1e30

OFF_DA_Q = 0
OFF_DA_K = OFF_DA_Q + DA_WIDTH
OFF_DA_V = OFF_DA_K + DA_WIDTH
OFF_WG_Q = OFF_DA_V + DA_WIDTH
OFF_WG_K = OFF_WG_Q + WG_Q
OFF_WG_V = OFF_WG_K + WG_KV
OFF_GATE = OFF_WG_V + WG_KV

LANES = 128
SUBLANES = 8
BF16_ROWS = 16
VMEM_LIMIT_BYTES = 56 * 1024 * 1024

ROW_TILE = 512
DA_TQ = 512
DA_TK = ROW_TILE
DA_TILES_PER_STEP = 2
WG_TQ = 256
WG_BAND = WG_TQ + 2 * WINDOW
WG_MIN_DENOM = 2.0 ** -60

DA_KCOLS = 2 * DA_HEAD_DIM
AUG_ROWS = BF16_ROWS
SKIP_LOG2 = -150.0


def _alibi_slopes(n_heads):
    return 2.0 ** (-8.0 * np.arange(1, n_heads + 1) / n_heads)


def _nt_dot(a, b):
    return lax.dot_general(a, b, (((1,), (1,)), ((), ())), preferred_element_type=jnp.float32)


def _dot(a, b):
    return jnp.dot(a, b, preferred_element_type=jnp.float32)


def _proj_kernel(x_ref, wt_ref, wka_ref, wkb_ref,
                 qa_ref, va_ref, qb_ref, vb_ref, kat_ref, ka_ref, kb_ref, knsq_ref):
    xb = x_ref[...].astype(jnp.bfloat16)
    da_scale = DA_HEAD_DIM ** -0.5 * LOG2E
    wg_scale = WG_HEAD_DIM ** -0.5 * LOG2E
    r0, r1, r2, r3 = DA_WIDTH, 2 * DA_WIDTH, 2 * DA_WIDTH + WG_Q, 2 * DA_WIDTH + WG_Q + WG_KV
    qa_ref[...] = (_nt_dot(wt_ref[0:r0, :], xb) * da_scale).astype(jnp.bfloat16)
    va_ref[...] = _nt_dot(wt_ref[r0:r1, :], xb).astype(jnp.bfloat16)
    qb_ref[...] = (_nt_dot(wt_ref[r1:r2, :], xb) * wg_scale).astype(jnp.bfloat16)
    kat = _nt_dot(wt_ref[r3:, :], xb)
    kat_ref[...] = kat.astype(jnp.bfloat16)
    nsq = jnp.sum((kat * kat).reshape(2 * DA_HEADS, DA_HEAD_DIM, ROW_TILE), axis=1)
    vb = _nt_dot(wt_ref[r2:r3, :], xb).astype(jnp.bfloat16)
    for t in range(ROW_TILE // LANES):
        vb_ref[t] = vb[:, t * LANES:(t + 1) * LANES]
    kb_ref[...] = _dot(xb, wkb_ref[...]).astype(jnp.bfloat16)
    pos = pl.program_id(1) * ROW_TILE + lax.broadcasted_iota(jnp.int32, (ROW_TILE, DA_KCOLS), 0)
    lane = lax.broadcasted_iota(jnp.int32, (ROW_TILE, DA_KCOLS), 1)
    pos_hi = ((pos // LANES) * LANES).astype(jnp.float32)
    pos_lo = (pos % LANES).astype(jnp.float32)
    lower = lane < DA_HEAD_DIM

    def position_cols(c0):
        return jnp.where(lane < c0, 0.0,
                         jnp.where(lane < c0 + 3, 1.0,
                                   jnp.where(lane < c0 + 6, pos_hi,
                                             jnp.where(lane < c0 + 9, pos_lo, 0.0))))
    pat0, pat1 = position_cols(DA_HEAD_DIM), position_cols(0)
    for hd in range(DA_HEADS):
        kk = _dot(xb, wka_ref[:, hd * DA_KCOLS:(hd + 1) * DA_KCOLS])
        ka_ref[:, (2 * hd) * DA_KCOLS:(2 * hd + 1) * DA_KCOLS] = jnp.where(lower, kk, pat0).astype(jnp.bfloat16)
        ka_ref[:, (2 * hd + 1) * DA_KCOLS:(2 * hd + 2) * DA_KCOLS] = jnp.where(lower, pat1, kk).astype(jnp.bfloat16)

    @pl.when(pl.program_id(1) == 0)
    def _():
        knsq_ref[...] = nsq

    @pl.when(pl.program_id(1) > 0)
    def _():
        knsq_ref[...] = jnp.maximum(knsq_ref[...], nsq)


def _project(x, w_t, w_ka, w_kb):
    B, S, D = x.shape
    ns = S // ROW_TILE
    rows_t = w_t.shape[0]
    ka_cols = 2 * DA_HEADS * DA_KCOLS
    bf = jnp.bfloat16
    return pl.pallas_call(
        _proj_kernel,
        grid=(B, ns),
        in_specs=[
            pl.BlockSpec((None, ROW_TILE, D), lambda b, s: (b, s, 0)),
            pl.BlockSpec((rows_t, D), lambda b, s: (0, 0)),
            pl.BlockSpec((D, DA_WIDTH), lambda b, s: (0, 0)),
            pl.BlockSpec((D, WG_KV), lambda b, s: (0, 0)),
        ],
        out_specs=[
            pl.BlockSpec((None, DA_WIDTH, ROW_TILE), lambda b, s: (b, 0, s)),
            pl.BlockSpec((None, None, DA_WIDTH, ROW_TILE), lambda b, s: (b, s, 0, 0)),
            pl.BlockSpec((None, WG_Q, ROW_TILE), lambda b, s: (b, 0, s)),
            pl.BlockSpec((None, ROW_TILE // LANES, WG_KV, LANES), lambda b, s: (b, s, 0, 0)),
            pl.BlockSpec((None, DA_WIDTH, ROW_TILE), lambda b, s: (b, 0, s)),
            pl.BlockSpec((None, ROW_TILE, ka_cols), lambda b, s: (b, s, 0)),
            pl.BlockSpec((None, ROW_TILE, WG_KV), lambda b, s: (b, s, 0)),
            pl.BlockSpec((None, 2 * DA_HEADS, ROW_TILE), lambda b, s: (b, 0, 0)),
        ],
        out_shape=[
            jax.ShapeDtypeStruct((B, DA_WIDTH, S), bf),
            jax.ShapeDtypeStruct((B, ns, DA_WIDTH, ROW_TILE), bf),
            jax.ShapeDtypeStruct((B, WG_Q, S), bf),
            jax.ShapeDtypeStruct((B, S // LANES, WG_KV, LANES), bf),
            jax.ShapeDtypeStruct((B, DA_WIDTH, S), bf),
            jax.ShapeDtypeStruct((B, S, ka_cols), bf),
            jax.ShapeDtypeStruct((B, S, WG_KV), bf),
            jax.ShapeDtypeStruct((B, 2 * DA_HEADS, ROW_TILE), jnp.float32),
        ],
        compiler_params=pltpu.CompilerParams(
            dimension_semantics=("arbitrary", "arbitrary"), vmem_limit_bytes=VMEM_LIMIT_BYTES),
        name="in_proj",
    )(x, w_t, w_ka, w_kb)


def _split3(x):
    hi = x.astype(jnp.bfloat16).astype(jnp.float32)
    r1 = x - hi
    mid = r1.astype(jnp.bfloat16).astype(jnp.float32)
    return hi, mid, r1 - mid


def _diff_attn_kernel(*refs, n_k, lam_init):
    for sub in range(DA_TILES_PER_STEP):
        _diff_attn_tile(sub, *refs, n_k=n_k, lam_init=lam_init)


def _diff_attn_tile(sub, slope_ref, islope_ref, q_ref, kt_ref, kn_ref, k_ref, v_ref,
                    lq1_ref, lk1_ref, lq2_ref, lk2_ref, g_ref,
                    o_ref, bias_ref, qaug_ref, ea_ref, eb_ref, acc_ref,
                    *, n_k, lam_init):
    tq, tk, dh = DA_TQ, DA_TK, DA_HEAD_DIM
    cols = slice(sub * tq, (sub + 1) * tq)
    h = pl.program_id(1)
    qi = DA_TILES_PER_STEP * pl.program_id(2) + sub
    sig = slope_ref[h]

    def k_tile(kt):
        return k_ref[pl.ds(pl.multiple_of(kt * tk, tk), tk), :]

    def col_sum8(e):
        return jnp.sum(e.reshape(tk // SUBLANES, SUBLANES, e.shape[1]), axis=0)

    def both_maps(m0, m1):
        z = jnp.zeros_like(m0)
        return jnp.concatenate([jnp.concatenate([m0, z], axis=1), jnp.concatenate([z, m1], axis=1)], axis=0)

    @pl.when(qi == 0)
    def _():
        dd = (lax.broadcasted_iota(jnp.int32, (tk, tq), 1) - lax.broadcasted_iota(jnp.int32, (tk, tq), 0))
        b = -sig * jnp.abs(dd).astype(jnp.float32)
        bias_ref[...] = jnp.concatenate([b, b], axis=1)

    q = q_ref[:, cols]
    qc = (q[:dh], q[dh:])
    zpad = jnp.zeros((DA_KCOLS - dh, tq), q.dtype)
    qpad = both_maps(jnp.concatenate([qc[0], zpad], axis=0),
                     jnp.concatenate([zpad, qc[1]], axis=0))

    q_start = qi * tq
    kt_ov = q_start // tk

    qk_self = q.astype(jnp.float32) * kt_ref[:, cols].astype(jnp.float32)
    r = jnp.concatenate([jnp.sum(qk_self[:dh], axis=0, keepdims=True),
                         jnp.sum(qk_self[dh:], axis=0, keepdims=True)], axis=1)

    k_norm = jnp.sqrt(jnp.max(kn_ref[...], axis=1, keepdims=True)) * 1.01
    q_sq = q.astype(jnp.float32) ** 2
    bound = jnp.concatenate([jnp.sqrt(jnp.sum(q_sq[:dh], axis=0, keepdims=True)) * k_norm[0:1],
                             jnp.sqrt(jnp.sum(q_sq[dh:], axis=0, keepdims=True)) * k_norm[1:2]], axis=1)
    gap = jnp.max(bound - r, axis=1, keepdims=True)
    reach = jnp.minimum((gap - SKIP_LOG2) * islope_ref[h], float(2 * n_k * tk))
    qs_f = q_start.astype(jnp.float32)
    lo_f = jnp.floor((qs_f - reach - 1.0) * (1.0 / tk))
    hi_f = jnp.floor((qs_f + float(tq) + reach) * (1.0 / tk))
    lo_t = jnp.minimum(jnp.clip(lo_f, 0.0, float(n_k)).astype(jnp.int32)[0, 0], kt_ov)
    hi_t = jnp.maximum(jnp.clip(hi_f, -1.0, float(n_k - 1)).astype(jnp.int32)[0, 0], kt_ov)
    n_left = kt_ov - lo_t
    n_tot = n_left + hi_t - kt_ov

    ipos = (q_start + lax.broadcasted_iota(jnp.int32, (1, tq), 1)).astype(jnp.float32)
    row = lax.broadcasted_iota(jnp.int32, (AUG_ROWS, tq), 0)
    sig_row = jnp.full((1, tq), sig, jnp.float32)
    zrest = jnp.zeros((DA_KCOLS - dh - AUG_ROWS, tq), q.dtype)
    for side, sgn in enumerate((1.0, -1.0, 0.0)):
        s3 = _split3(sgn * sig_row)
        maps = []
        for c in range(2):
            r3 = _split3(-(r[:, c * tq:(c + 1) * tq] + sgn * sig * ipos))
            slab = jnp.zeros((AUG_ROWS, tq), jnp.float32)
            for i, piece in enumerate(r3 + s3 + s3):
                slab = jnp.where(row == i, piece, slab)
            parts = [qc[c], slab.astype(q.dtype), zrest] if c == 0 else [slab.astype(q.dtype), zrest, qc[c]]
            maps.append(jnp.concatenate(parts, axis=0))
        qaug_ref[side] = both_maps(*maps)

    def tile_of(i):
        i = jnp.clip(i, 0, jnp.maximum(n_tot - 1, 0))
        kt = jnp.where(i < n_left, lo_t + i, kt_ov + 1 + i - n_left)
        return jnp.minimum(kt, n_k - 1), (i >= n_left).astype(jnp.int32)

    def scores(i):
        kt, side = tile_of(i)
        return _dot(k_tile(kt), qaug_ref[side])

    def exp_scores(i, e_ref):
        e = jnp.exp2(scores(i))
        e_ref[...] = e.astype(e_ref.dtype)
        return col_sum8(e)

    def add_av(i, e_ref):
        kt, _ = tile_of(i)
        acc_ref[...] += _dot(v_ref[kt], e_ref[...])

    e = jnp.exp2(_dot(k_tile(kt_ov), qaug_ref[2]) + bias_ref[...])
    pend = exp_scores(0, ea_ref)
    l8 = col_sum8(e)
    acc_ref[...] = _dot(v_ref[kt_ov], e.astype(jnp.bfloat16))

    def pair(p, carry):
        l8, pend = carry
        i = 2 * p
        pend_b = exp_scores(i + 1, eb_ref)
        add_av(i, ea_ref)
        pend_a = exp_scores(i + 2, ea_ref)
        add_av(i + 1, eb_ref)
        return l8 + pend + pend_b, pend_a

    ends_even = jnp.logical_and(n_tot > 0, n_tot % 2 == 0)
    n_pairs = n_tot // 2 - ends_even.astype(jnp.int32)

    def pairs(first, count, c):
        for j in range(count):
            c = pair(first + j, c)
        return c

    def last_tile(c):
        add_av(n_tot - 1, ea_ref)
        return c[0] + c[1]

    def last_two_tiles(c):
        pend_b = exp_scores(n_tot - 1, eb_ref)
        add_av(n_tot - 2, ea_ref)
        add_av(n_tot - 1, eb_ref)
        return c[0] + c[1] + pend_b

    carry = lax.fori_loop(0, n_pairs // 4, lambda t, c: pairs(4 * t, 4, c), (l8, pend))
    done = 4 * (n_pairs // 4)

    def seven_left(c):
        return last_tile(pairs(done, 3, c))

    def other_left(c):
        c = lax.cond(n_pairs % 4 >= 2, lambda c: pairs(done, 2, c), lambda c: c, c)
        c = lax.cond(n_pairs % 2 == 1, lambda c: pair(done + 2 * ((n_pairs % 4) // 2), c), lambda c: c, c)
        l8 = lax.cond(n_tot % 2 == 1, last_tile, lambda c: c[0], c)
        return lax.cond(ends_even, last_two_tiles, lambda c: c[0], (l8, c[1]))

    l8 = lax.cond(jnp.logical_and(n_pairs % 4 == 3, n_tot % 2 == 1), seven_left, other_left, carry)
    l = jnp.sum(l8, axis=0, keepdims=True)

    lam = (jnp.exp(jnp.sum(lq1_ref[...] * lk1_ref[...], axis=1, keepdims=True))
           - jnp.exp(jnp.sum(lq2_ref[...] * lk2_ref[...], axis=1, keepdims=True)) + lam_init)

    def finish(l):
        on = acc_ref[...] / l
        o = on[:, :tq] - lam * on[:, tq:]
        o = o * lax.rsqrt(jnp.mean(o * o, axis=0, keepdims=True) + LN_EPS)
        o = o * g_ref[...] * (1.0 - lam_init)
        o_ref[cols, :] = o.T.astype(o_ref.dtype)

    finish(l)

    chk = jnp.sum(acc_ref[...] * 0.0) + jnp.sum(l * 0.0)
    overflowed = jnp.logical_not(chk == 0.0)

    def exact_path():
        dd = (lax.broadcasted_iota(jnp.int32, (tk, tq), 1)
              - lax.broadcasted_iota(jnp.int32, (tk, tq), 0)).astype(jnp.float32)
        acc_ref[...] = jnp.zeros_like(acc_ref)

        def body(kt, carry):
            m, lc = carry
            b = -sig * jnp.abs(dd + (q_start - kt * tk).astype(jnp.float32))
            u = _dot(k_tile(kt), qpad) + jnp.concatenate([b, b], axis=1)
            m_new = jnp.maximum(m, jnp.max(u, axis=0, keepdims=True))
            e = jnp.exp2(u - m_new)
            alpha = jnp.exp2(m - m_new)
            acc_ref[...] = alpha * acc_ref[...] + _dot(v_ref[kt], e.astype(jnp.bfloat16))
            return m_new, alpha * lc + jnp.sum(e, axis=0, keepdims=True)

        init = (jnp.full((1, 2 * tq), NEG_BIG, jnp.float32), jnp.zeros((1, 2 * tq), jnp.float32))
        finish(lax.fori_loop(0, n_k, body, init)[1])

    pl.when(overflowed)(exact_path)


def _diff_attention(q_t, k_t, k_nsq, k, v_t, lam_q1, lam_k1, lam_q2, lam_k2, subln_g, lam_init):
    B, _, S = q_t.shape
    assert DA_TQ == DA_TK, "the diagonal-tile bias assumes square tiles"
    hw = 2 * DA_HEAD_DIM
    n_k = S // DA_TK
    slopes2 = _alibi_slopes(DA_HEADS) * LOG2E
    vec = lambda a: a.reshape(1, DA_HEAD_DIM)
    small = pl.BlockSpec((1, DA_HEAD_DIM), lambda b, h, i, s1, s2: (0, 0))
    q_step = DA_TILES_PER_STEP * DA_TQ
    grid_spec = pltpu.PrefetchScalarGridSpec(
        num_scalar_prefetch=2,
        grid=(B, DA_HEADS, S // q_step),
        in_specs=[
            pl.BlockSpec((None, hw, q_step), lambda b, h, i, s1, s2: (b, h, i)),
            pl.BlockSpec((None, hw, q_step), lambda b, h, i, s1, s2: (b, h, i)),
            pl.BlockSpec((None, None, 2, ROW_TILE), lambda b, h, i, s1, s2: (b, h, 0, 0)),
            pl.BlockSpec((None, S, 2 * DA_KCOLS), lambda b, h, i, s1, s2: (b, 0, h)),
            pl.BlockSpec((None, n_k, hw, DA_TK), lambda b, h, i, s1, s2: (b, 0, h, 0)),
            small, small, small, small,
            pl.BlockSpec((hw, 1), lambda b, h, i, s1, s2: (0, 0)),
        ],
        out_specs=pl.BlockSpec((None, q_step, hw), lambda b, h, i, s1, s2: (b, i, h)),
        scratch_shapes=[
            pltpu.VMEM((DA_TK, 2 * DA_TQ), jnp.float32),
            pltpu.VMEM((3, 2 * DA_KCOLS, 2 * DA_TQ), jnp.bfloat16),
            pltpu.VMEM((DA_TK, 2 * DA_TQ), jnp.bfloat16),
            pltpu.VMEM((DA_TK, 2 * DA_TQ), jnp.bfloat16),
            pltpu.VMEM((hw, 2 * DA_TQ), jnp.float32),
        ],
    )
    return pl.pallas_call(
        functools.partial(_diff_attn_kernel, n_k=n_k, lam_init=lam_init),
        grid_spec=grid_spec,
        out_shape=jax.ShapeDtypeStruct((B, S, DA_WIDTH), jnp.bfloat16),
        compiler_params=pltpu.CompilerParams(
            dimension_semantics=("arbitrary", "arbitrary", "arbitrary"),
            vmem_limit_bytes=VMEM_LIMIT_BYTES),
        name="diff_attn",
    )(jnp.asarray(slopes2, jnp.float32), jnp.asarray(1.0 / slopes2, jnp.float32),
      q_t, k_t, k_nsq.reshape(B, DA_HEADS, 2, ROW_TILE), k, v_t,
      vec(lam_q1), vec(lam_k1), vec(lam_q2), vec(lam_k2), subln_g.reshape(hw, 1))


def _win_attn_kernel(slope_ref, sink_ref, q_ref, k_ref, v_ref, o_ref, ot_ref, *, seq_len):
    tq, band = WG_TQ, WG_BAND
    n_vt = band // LANES
    rep = WG_HEADS // WG_KV_HEADS
    qi = pl.program_id(1)
    q_start = qi * tq
    tile0 = jnp.clip(q_start // LANES - WINDOW // LANES, 0, seq_len // LANES - n_vt)
    k_start = pl.multiple_of(tile0 * LANES, LANES)

    kband = k_ref[pl.ds(k_start, band), :]
    kpos = k_start + lax.broadcasted_iota(jnp.int32, (band, tq), 0)
    qpos = q_start + lax.broadcasted_iota(jnp.int32, (band, tq), 1)
    dist_i = jnp.abs(qpos - kpos)
    dist = jnp.where(dist_i <= WINDOW, dist_i.astype(jnp.float32), -NEG_BIG)

    def group(g, shift_by_max):
        rows = slice(g * WG_HEAD_DIM, (g + 1) * WG_HEAD_DIM)
        blocks = []
        for r in range(rep):
            hd = g * rep + r
            parts = []
            if g > 0:
                parts.append(jnp.zeros((g * WG_HEAD_DIM, tq), q_ref.dtype))
            parts.append(q_ref[hd * WG_HEAD_DIM:(hd + 1) * WG_HEAD_DIM, :])
            if g < WG_KV_HEADS - 1:
                parts.append(jnp.zeros(((WG_KV_HEADS - 1 - g) * WG_HEAD_DIM, tq), q_ref.dtype))
            blocks.append(jnp.concatenate(parts, axis=0))
        s = _dot(kband, jnp.concatenate(blocks, axis=1))
        es, denoms = [], []
        for r in range(rep):
            hd = g * rep + r
            sr = s[:, r * tq:(r + 1) * tq] - slope_ref[hd] * dist
            sk = jnp.full((1, tq), sink_ref[hd] * LOG2E, jnp.float32)
            if shift_by_max:
                m = jnp.maximum(jnp.max(sr, axis=0, keepdims=True), sk)
                sr, sk = sr - m, sk - m
            e = jnp.exp2(sr)
            denoms.append(jnp.sum(e, axis=0, keepdims=True) + jnp.exp2(sk))
            es.append(e.astype(jnp.bfloat16))
        vg = jnp.concatenate([v_ref[tile0 + t, rows, :] for t in range(n_vt)], axis=1)
        den = jnp.concatenate(denoms, axis=1)
        og = _dot(vg, jnp.concatenate(es, axis=1)) / den
        for r in range(rep):
            hd = g * rep + r
            ot_ref[hd * WG_HEAD_DIM:(hd + 1) * WG_HEAD_DIM, :] = og[:, r * tq:(r + 1) * tq]
        return jnp.sum(og * 0.0) + jnp.sum(jnp.where(den >= WG_MIN_DENOM, 0.0, 1.0))

    bad = group(0, False)
    for g in range(1, WG_KV_HEADS):
        bad = bad + group(g, False)
    o_ref[...] = ot_ref[...].T.astype(o_ref.dtype)

    @pl.when(jnp.logical_not(bad == 0.0))
    def _():
        for g in range(WG_KV_HEADS):
            group(g, True)
        o_ref[...] = ot_ref[...].T.astype(o_ref.dtype)


def _window_attention(q_t, k, v_t, sink_logit):
    B, _, S = q_t.shape
    slopes2 = jnp.asarray(_alibi_slopes(WG_HEADS) * LOG2E, jnp.float32)
    grid_spec = pltpu.PrefetchScalarGridSpec(
        num_scalar_prefetch=2,
        grid=(B, S // WG_TQ),
        in_specs=[
            pl.BlockSpec((None, WG_Q, WG_TQ), lambda b, i, s1, s2: (b, 0, i)),
            pl.BlockSpec((None, S, WG_KV), lambda b, i, s1, s2: (b, 0, 0)),
            pl.BlockSpec((None, S // LANES, WG_KV, LANES), lambda b, i, s1, s2: (b, 0, 0, 0)),
        ],
        out_specs=pl.BlockSpec((None, WG_TQ, WG_Q), lambda b, i, s1, s2: (b, i, 0)),
        scratch_shapes=[pltpu.VMEM((WG_Q, WG_TQ), jnp.float32)],
    )
    return pl.pallas_call(
        functools.partial(_win_attn_kernel, seq_len=S),
        grid_spec=grid_spec,
        out_shape=jax.ShapeDtypeStruct((B, S, WG_Q), jnp.bfloat16),
        compiler_params=pltpu.CompilerParams(
            dimension_semantics=("arbitrary", "arbitrary"), vmem_limit_bytes=VMEM_LIMIT_BYTES),
        name="win_attn",
    )(slopes2, sink_logit.astype(jnp.float32), q_t, k, v_t)


def _layer_norm(x, g, b):
    mu = jnp.mean(x, axis=-1, keepdims=True)
    xc = x - mu
    var = jnp.mean(xc * xc, axis=-1, keepdims=True)
    return xc * lax.rsqrt(var + LN_EPS) * g + b


def _merge_kernel(x_ref, oa_ref, ob_ref, wg_ref, bg_ref, wa_ref, wb_ref, wo_ref, g_ref, b_ref, y_ref):
    x = x_ref[...]
    gates = jax.nn.sigmoid(_dot(x.astype(jnp.bfloat16), wg_ref[...]) + bg_ref[...])
    merged = (gates[:, :D_MODEL] * _dot(oa_ref[...], wa_ref[...])
              + gates[:, D_MODEL:] * _dot(ob_ref[...], wb_ref[...]))
    mix = _dot(merged.astype(jnp.bfloat16), wo_ref[...])
    y_ref[...] = _layer_norm(DEEPNORM_ALPHA * x + mix, g_ref[...], b_ref[...])


def _merge(x, o_a, o_b, w_gate, b_gate, w_br_a, w_br_b, w_out, ln_g, ln_b):
    B, S, D = x.shape
    tok = lambda b, s: (b, s, 0)
    const = lambda b, s: (0, 0)
    return pl.pallas_call(
        _merge_kernel,
        grid=(B, S // ROW_TILE),
        in_specs=[
            pl.BlockSpec((None, ROW_TILE, D), tok),
            pl.BlockSpec((None, ROW_TILE, DA_WIDTH), tok),
            pl.BlockSpec((None, ROW_TILE, WG_Q), tok),
            pl.BlockSpec((D, 2 * D), const),
            pl.BlockSpec((1, 2 * D), const),
            pl.BlockSpec((DA_WIDTH, D), const),
            pl.BlockSpec((WG_Q, D), const),
            pl.BlockSpec((D, D), const),
            pl.BlockSpec((1, D), const),
            pl.BlockSpec((1, D), const),
        ],
        out_specs=pl.BlockSpec((None, ROW_TILE, D), tok),
        out_shape=jax.ShapeDtypeStruct((B, S, D), jnp.float32),
        compiler_params=pltpu.CompilerParams(
            dimension_semantics=("arbitrary", "arbitrary"), vmem_limit_bytes=VMEM_LIMIT_BYTES),
        name="merge_ln1",
    )(x, o_a, o_b, w_gate, b_gate, w_br_a, w_br_b, w_out, ln_g, ln_b)


def _ffn_kernel(x_ref, w1_ref, b1_ref, w2_ref, b2_ref, g_ref, b_ref, y_ref):
    x = x_ref[...]
    h = jnp.maximum(_dot(x.astype(jnp.bfloat16), w1_ref[...]) + b1_ref[...], 0.0)
    f = _dot((h * h).astype(jnp.bfloat16), w2_ref[...]) + b2_ref[...]
    y_ref[...] = _layer_norm(DEEPNORM_ALPHA * x + f, g_ref[...], b_ref[...])


def _ffn(x, w1, b1, w2, b2, ln_g, ln_b):
    B, S, D = x.shape
    tok = lambda b, s: (b, s, 0)
    const = lambda b, s: (0, 0)
    return pl.pallas_call(
        _ffn_kernel,
        grid=(B, S // ROW_TILE),
        in_specs=[
            pl.BlockSpec((None, ROW_TILE, D), tok),
            pl.BlockSpec((D, D_FF), const),
            pl.BlockSpec((1, D_FF), const),
            pl.BlockSpec((D_FF, D), const),
            pl.BlockSpec((1, D), const),
            pl.BlockSpec((1, D), const),
            pl.BlockSpec((1, D), const),
        ],
        out_specs=pl.BlockSpec((None, ROW_TILE, D), tok),
        out_shape=jax.ShapeDtypeStruct((B, S, D), jnp.float32),
        compiler_params=pltpu.CompilerParams(
            dimension_semantics=("arbitrary", "arbitrary"), vmem_limit_bytes=VMEM_LIMIT_BYTES),
        name="ffn_ln2",
    )(x, w1, b1, w2, b2, ln_g, ln_b)


def _encoder_layer(x, l, w_in, b_gate, lam_q1, lam_k1, lam_q2, lam_k2, subln_g, sink_logit,
                   w_br_a, w_br_b, w_out, ln1_g, ln1_b, w_ff1, b_ff1, w_ff2, b_ff2, ln2_g, ln2_b):
    bf = jnp.bfloat16
    row = lambda a: a.reshape(1, -1)
    w_t = jnp.concatenate([w_in[:, OFF_DA_Q:OFF_DA_K], w_in[:, OFF_DA_V:OFF_WG_Q],
                           w_in[:, OFF_WG_Q:OFF_WG_K], w_in[:, OFF_WG_V:OFF_GATE],
                           w_in[:, OFF_DA_K:OFF_DA_V]], axis=1).T.astype(bf)
    w_ka = w_in[:, OFF_DA_K:OFF_DA_V].astype(bf)
    w_kb = w_in[:, OFF_WG_K:OFF_WG_V].astype(bf)
    w_gate = w_in[:, OFF_GATE:].astype(bf)
    lam_init = 0.8 - 0.6 * math.exp(-0.3 * l)

    qa_t, va_t, qb_t, vb_t, ka_t, k_a, k_b, ka_nsq = _project(x, w_t, w_ka, w_kb)
    o_a = _diff_attention(qa_t, ka_t, ka_nsq, k_a, va_t, lam_q1, lam_k1, lam_q2, lam_k2, subln_g, lam_init)
    o_b = _window_attention(qb_t, k_b, vb_t, sink_logit)
    x1 = _merge(x, o_a, o_b, w_gate, row(b_gate), w_br_a.astype(bf), w_br_b.astype(bf),
                w_out.astype(bf), row(ln1_g), row(ln1_b))
    return _ffn(x1, w_ff1.astype(bf), row(b_ff1), w_ff2.astype(bf), row(b_ff2), row(ln2_g), row(ln2_b))


def kernel(x_prompt, x_sample, w_in, b_gate, lam_q1, lam_k1, lam_q2, lam_k2, subln_g, sink_logit,
           w_br_a, w_br_b, w_out, ln1_g, ln1_b, w_ff1, b_ff1, w_ff2, b_ff2, ln2_g, ln2_b):
    def run(x):
        for l in range(DEPTH):
            x = _encoder_layer(x, l, w_in[l], b_gate[l], lam_q1[l], lam_k1[l], lam_q2[l], lam_k2[l],
                               subln_g[l], sink_logit[l], w_br_a[l], w_br_b[l], w_out[l],
                               ln1_g[l], ln1_b[l], w_ff1[l], b_ff1[l], w_ff2[l], b_ff2[l],
                               ln2_g[l], ln2_b[l])
        return x

    return (run(x_prompt), run(x_sample))
```

```python
import functools
import math

import jax
import jax.numpy as jnp
import numpy as np
from jax import lax
from jax.experimental import pallas as pl
from jax.experimental.pallas import tpu as pltpu

D_MODEL = 1024
DA_HEADS = 8
DA_HEAD_DIM = 64
DA_WIDTH = DA_HEADS * 2 * DA_HEAD_DIM
WG_HEADS = 16
WG_KV_HEADS = 4
WG_HEAD_DIM = 64
WG_Q = WG_HEADS * WG_HEAD_DIM
WG_KV = WG_KV_HEADS * WG_HEAD_DIM
WINDOW = 128
D_FF = 4 * D_MODEL
DEPTH = 1
DEEPNORM_ALPHA = (2.0 * DEPTH) ** 0.25
LN_EPS = 1e-5
LOG2E = math.log2(math.e)
NEG_BIG = -1e30

OFF_DA_Q = 0
OFF_DA_K = OFF_DA_Q + DA_WIDTH
OFF_DA_V = OFF_DA_K + DA_WIDTH
OFF_WG_Q = OFF_DA_V + DA_WIDTH
OFF_WG_K = OFF_WG_Q + WG_Q
OFF_WG_V = OFF_WG_K + WG_KV
OFF_GATE = OFF_WG_V + WG_KV

LANES = 128
SUBLANES = 8
BF16_ROWS = 16
VMEM_LIMIT_BYTES = 56 * 1024 * 1024

ROW_TILE = 512
DA_TQ = 512
DA_TK = ROW_TILE
WG_TQ = 256
WG_BAND = WG_TQ + 2 * WINDOW
WG_MIN_DENOM = 2.0 ** -60

DA_KCOLS = 2 * DA_HEAD_DIM
AUG_ROWS = BF16_ROWS
SKIP_LOG2 = -150.0


def _alibi_slopes(n_heads):
    return 2.0 ** (-8.0 * np.arange(1, n_heads + 1) / n_heads)


def _nt_dot(a, b):
    return lax.dot_general(a, b, (((1,), (1,)), ((), ())), preferred_element_type=jnp.float32)


def _dot(a, b):
    return jnp.dot(a, b, preferred_element_type=jnp.float32)


def _proj_kernel(x_ref, wt_ref, wka_ref, wkb_ref,
                 qa_ref, va_ref, qb_ref, vb_ref, kat_ref, ka_ref, kb_ref, knsq_ref):
    xb = x_ref[...].astype(jnp.bfloat16)
    da_scale = DA_HEAD_DIM ** -0.5 * LOG2E
    wg_scale = WG_HEAD_DIM ** -0.5 * LOG2E
    r0, r1, r2, r3 = DA_WIDTH, 2 * DA_WIDTH, 2 * DA_WIDTH + WG_Q, 2 * DA_WIDTH + WG_Q + WG_KV
    qa_ref[...] = (_nt_dot(wt_ref[0:r0, :], xb) * da_scale).astype(jnp.bfloat16)
    va_ref[...] = _nt_dot(wt_ref[r0:r1, :], xb).astype(jnp.bfloat16)
    qb_ref[...] = (_nt_dot(wt_ref[r1:r2, :], xb) * wg_scale).astype(jnp.bfloat16)
    kat = _nt_dot(wt_ref[r3:, :], xb)
    kat_ref[...] = kat.astype(jnp.bfloat16)
    nsq = jnp.sum((kat * kat).reshape(2 * DA_HEADS, DA_HEAD_DIM, ROW_TILE), axis=1)
    vb = _nt_dot(wt_ref[r2:r3, :], xb).astype(jnp.bfloat16)
    for t in range(ROW_TILE // LANES):
        vb_ref[t] = vb[:, t * LANES:(t + 1) * LANES]
    kb_ref[...] = _dot(xb, wkb_ref[...]).astype(jnp.bfloat16)
    pos = pl.program_id(1) * ROW_TILE + lax.broadcasted_iota(jnp.int32, (ROW_TILE, DA_KCOLS), 0)
    lane = lax.broadcasted_iota(jnp.int32, (ROW_TILE, DA_KCOLS), 1)
    pos_hi = ((pos // LANES) * LANES).astype(jnp.float32)
    pos_lo = (pos % LANES).astype(jnp.float32)
    lower = lane < DA_HEAD_DIM

    def position_cols(c0):
        return jnp.where(lane < c0, 0.0,
                         jnp.where(lane < c0 + 3, 1.0,
                                   jnp.where(lane < c0 + 6, pos_hi,
                                             jnp.where(lane < c0 + 9, pos_lo, 0.0))))
    pat0, pat1 = position_cols(DA_HEAD_DIM), position_cols(0)
    for hd in range(DA_HEADS):
        kk = _dot(xb, wka_ref[:, hd * DA_KCOLS:(hd + 1) * DA_KCOLS])
        ka_ref[:, (2 * hd) * DA_KCOLS:(2 * hd + 1) * DA_KCOLS] = jnp.where(lower, kk, pat0).astype(jnp.bfloat16)
        ka_ref[:, (2 * hd + 1) * DA_KCOLS:(2 * hd + 2) * DA_KCOLS] = jnp.where(lower, pat1, kk).astype(jnp.bfloat16)

    @pl.when(pl.program_id(1) == 0)
    def _():
        knsq_ref[...] = nsq

    @pl.when(pl.program_id(1) > 0)
    def _():
        knsq_ref[...] = jnp.maximum(knsq_ref[...], nsq)


def _project(x, w_t, w_ka, w_kb):
    B, S, D = x.shape
    ns = S // ROW_TILE
    rows_t = w_t.shape[0]
    ka_cols = 2 * DA_HEADS * DA_KCOLS
    bf = jnp.bfloat16
    return pl.pallas_call(
        _proj_kernel,
        grid=(B, ns),
        in_specs=[
            pl.BlockSpec((None, ROW_TILE, D), lambda b, s: (b, s, 0)),
            pl.BlockSpec((rows_t, D), lambda b, s: (0, 0)),
            pl.BlockSpec((D, DA_WIDTH), lambda b, s: (0, 0)),
            pl.BlockSpec((D, WG_KV), lambda b, s: (0, 0)),
        ],
        out_specs=[
            pl.BlockSpec((None, DA_WIDTH, ROW_TILE), lambda b, s: (b, 0, s)),
            pl.BlockSpec((None, None, DA_WIDTH, ROW_TILE), lambda b, s: (b, s, 0, 0)),
            pl.BlockSpec((None, WG_Q, ROW_TILE), lambda b, s: (b, 0, s)),
            pl.BlockSpec((None, ROW_TILE // LANES, WG_KV, LANES), lambda b, s: (b, s, 0, 0)),
            pl.BlockSpec((None, DA_WIDTH, ROW_TILE), lambda b, s: (b, 0, s)),
            pl.BlockSpec((None, ROW_TILE, ka_cols), lambda b, s: (b, s, 0)),
            pl.BlockSpec((None, ROW_TILE, WG_KV), lambda b, s: (b, s, 0)),
            pl.BlockSpec((None, 2 * DA_HEADS, ROW_TILE), lambda b, s: (b, 0, 0)),
        ],
        out_shape=[
            jax.ShapeDtypeStruct((B, DA_WIDTH, S), bf),
            jax.ShapeDtypeStruct((B, ns, DA_WIDTH, ROW_TILE), bf),
            jax.ShapeDtypeStruct((B, WG_Q, S), bf),
            jax.ShapeDtypeStruct((B, S // LANES, WG_KV, LANES), bf),
            jax.ShapeDtypeStruct((B, DA_WIDTH, S), bf),
            jax.ShapeDtypeStruct((B, S, ka_cols), bf),
            jax.ShapeDtypeStruct((B, S, WG_KV), bf),
            jax.ShapeDtypeStruct((B, 2 * DA_HEADS, ROW_TILE), jnp.float32),
        ],
        compiler_params=pltpu.CompilerParams(
            dimension_semantics=("arbitrary", "arbitrary"), vmem_limit_bytes=VMEM_LIMIT_BYTES),
        name="in_proj",
    )(x, w_t, w_ka, w_kb)


def _split3(x):
    hi = x.astype(jnp.bfloat16).astype(jnp.float32)
    r1 = x - hi
    mid = r1.astype(jnp.bfloat16).astype(jnp.float32)
    return hi, mid, r1 - mid


def _diff_attn_kernel(slope_ref, islope_ref, q_ref, kt_ref, kn_ref, k_ref, v_ref,
                      lq1_ref, lk1_ref, lq2_ref, lk2_ref, g_ref,
                      o_ref, bias_ref, qaug_ref, ea_ref, eb_ref, acc_ref,
                      *, n_k, lam_init):
    tq, tk, dh = DA_TQ, DA_TK, DA_HEAD_DIM
    h = pl.program_id(1)
    qi = pl.program_id(2)
    sig = slope_ref[h]

    def k_tile(kt):
        return k_ref[pl.ds(pl.multiple_of(kt * tk, tk), tk), :]

    def col_sum8(e):
        return jnp.sum(e.reshape(tk // SUBLANES, SUBLANES, e.shape[1]), axis=0)

    def both_maps(m0, m1):
        z = jnp.zeros_like(m0)
        return jnp.concatenate([jnp.concatenate([m0, z], axis=1), jnp.concatenate([z, m1], axis=1)], axis=0)

    @pl.when(qi == 0)
    def _():
        dd = (lax.broadcasted_iota(jnp.int32, (tk, tq), 1) - lax.broadcasted_iota(jnp.int32, (tk, tq), 0))
        b = -sig * jnp.abs(dd).astype(jnp.float32)
        bias_ref[...] = jnp.concatenate([b, b], axis=1)

    q = q_ref[...]
    qc = (q[:dh], q[dh:])
    zpad = jnp.zeros((DA_KCOLS - dh, tq), q.dtype)
    qpad = both_maps(jnp.concatenate([qc[0], zpad], axis=0),
                     jnp.concatenate([zpad, qc[1]], axis=0))

    q_start = qi * tq
    kt_ov = q_start // tk

    qk_self = q.astype(jnp.float32) * kt_ref[...].astype(jnp.float32)
    r = jnp.concatenate([jnp.sum(qk_self[:dh], axis=0, keepdims=True),
                         jnp.sum(qk_self[dh:], axis=0, keepdims=True)], axis=1)

    k_norm = jnp.sqrt(jnp.max(kn_ref[...], axis=1, keepdims=True)) * 1.01
    q_sq = q.astype(jnp.float32) ** 2
    bound = jnp.concatenate([jnp.sqrt(jnp.sum(q_sq[:dh], axis=0, keepdims=True)) * k_norm[0:1],
                             jnp.sqrt(jnp.sum(q_sq[dh:], axis=0, keepdims=True)) * k_norm[1:2]], axis=1)
    gap = jnp.max(bound - r, axis=1, keepdims=True)
    reach = jnp.minimum((gap - SKIP_LOG2) * islope_ref[h], float(2 * n_k * tk))
    qs_f = q_start.astype(jnp.float32)
    lo_f = jnp.floor((qs_f - reach - 1.0) * (1.0 / tk))
    hi_f = jnp.floor((qs_f + float(tq) + reach) * (1.0 / tk))
    lo_t = jnp.minimum(jnp.clip(lo_f, 0.0, float(n_k)).astype(jnp.int32)[0, 0], kt_ov)
    hi_t = jnp.maximum(jnp.clip(hi_f, -1.0, float(n_k - 1)).astype(jnp.int32)[0, 0], kt_ov)
    n_left = kt_ov - lo_t
    n_tot = n_left + hi_t - kt_ov

    ipos = (q_start + lax.broadcasted_iota(jnp.int32, (1, tq), 1)).astype(jnp.float32)
    row = lax.broadcasted_iota(jnp.int32, (AUG_ROWS, tq), 0)
    sig_row = jnp.full((1, tq), sig, jnp.float32)
    zrest = jnp.zeros((DA_KCOLS - dh - AUG_ROWS, tq), q.dtype)
    for side, sgn in enumerate((1.0, -1.0, 0.0)):
        s3 = _split3(sgn * sig_row)
        maps = []
        for c in range(2):
            r3 = _split3(-(r[:, c * tq:(c + 1) * tq] + sgn * sig * ipos))
            slab = jnp.zeros((AUG_ROWS, tq), jnp.float32)
            for i, piece in enumerate(r3 + s3 + s3):
                slab = jnp.where(row == i, piece, slab)
            parts = [qc[c], slab.astype(q.dtype), zrest] if c == 0 else [slab.astype(q.dtype), zrest, qc[c]]
            maps.append(jnp.concatenate(parts, axis=0))
        qaug_ref[side] = both_maps(*maps)

    def tile_of(i):
        i = jnp.clip(i, 0, jnp.maximum(n_tot - 1, 0))
        kt = jnp.where(i < n_left, lo_t + i, kt_ov + 1 + i - n_left)
        return jnp.minimum(kt, n_k - 1), (i >= n_left).astype(jnp.int32)

    def scores(i):
        kt, side = tile_of(i)
        return _dot(k_tile(kt), qaug_ref[side])

    def exp_scores(i, e_ref):
        e = jnp.exp2(scores(i))
        e_ref[...] = e.astype(e_ref.dtype)
        return col_sum8(e)

    def add_av(i, e_ref):
        kt, _ = tile_of(i)
        acc_ref[...] += _dot(v_ref[kt], e_ref[...])

    e = jnp.exp2(_dot(k_tile(kt_ov), qaug_ref[2]) + bias_ref[...])
    pend = exp_scores(0, ea_ref)
    l8 = col_sum8(e)
    acc_ref[...] = _dot(v_ref[kt_ov], e.astype(jnp.bfloat16))

    def pair(p, carry):
        l8, pend = carry
        i = 2 * p
        pend_b = exp_scores(i + 1, eb_ref)
        add_av(i, ea_ref)
        pend_a = exp_scores(i + 2, ea_ref)
        add_av(i + 1, eb_ref)
        return l8 + pend + pend_b, pend_a

    ends_even = jnp.logical_and(n_tot > 0, n_tot % 2 == 0)
    n_pairs = n_tot // 2 - ends_even.astype(jnp.int32)

    def pairs(first, count, c):
        for j in range(count):
            c = pair(first + j, c)
        return c

    def last_tile(c):
        add_av(n_tot - 1, ea_ref)
        return c[0] + c[1]

    def last_two_tiles(c):
        pend_b = exp_scores(n_tot - 1, eb_ref)
        add_av(n_tot - 2, ea_ref)
        add_av(n_tot - 1, eb_ref)
        return c[0] + c[1] + pend_b

    carry = lax.fori_loop(0, n_pairs // 4, lambda t, c: pairs(4 * t, 4, c), (l8, pend))
    done = 4 * (n_pairs // 4)

    def seven_left(c):
        return last_tile(pairs(done, 3, c))

    def other_left(c):
        c = lax.cond(n_pairs % 4 >= 2, lambda c: pairs(done, 2, c), lambda c: c, c)
        c = lax.cond(n_pairs % 2 == 1, lambda c: pair(done + 2 * ((n_pairs % 4) // 2), c), lambda c: c, c)
        l8 = lax.cond(n_tot % 2 == 1, last_tile, lambda c: c[0], c)
        return lax.cond(ends_even, last_two_tiles, lambda c: c[0], (l8, c[1]))

    l8 = lax.cond(jnp.logical_and(n_pairs % 4 == 3, n_tot % 2 == 1), seven_left, other_left, carry)
    l = jnp.sum(l8, axis=0, keepdims=True)

    lam = (jnp.exp(jnp.sum(lq1_ref[...] * lk1_ref[...], axis=1, keepdims=True))
           - jnp.exp(jnp.sum(lq2_ref[...] * lk2_ref[...], axis=1, keepdims=True)) + lam_init)

    def finish(l):
        on = acc_ref[...] / l
        o = on[:, :tq] - lam * on[:, tq:]
        o = o * lax.rsqrt(jnp.mean(o * o, axis=0, keepdims=True) + LN_EPS)
        o = o * g_ref[...] * (1.0 - lam_init)
        o_ref[...] = o.T.astype(o_ref.dtype)

    finish(l)

    chk = jnp.sum(acc_ref[...] * 0.0) + jnp.sum(l * 0.0)
    overflowed = jnp.logical_not(chk == 0.0)

    def exact_path():
        dd = (lax.broadcasted_iota(jnp.int32, (tk, tq), 1)
              - lax.broadcasted_iota(jnp.int32, (tk, tq), 0)).astype(jnp.float32)
        acc_ref[...] = jnp.zeros_like(acc_ref)

        def body(kt, carry):
            m, lc = carry
            b = -sig * jnp.abs(dd + (q_start - kt * tk).astype(jnp.float32))
            u = _dot(k_tile(kt), qpad) + jnp.concatenate([b, b], axis=1)
            m_new = jnp.maximum(m, jnp.max(u, axis=0, keepdims=True))
            e = jnp.exp2(u - m_new)
            alpha = jnp.exp2(m - m_new)
            acc_ref[...] = alpha * acc_ref[...] + _dot(v_ref[kt], e.astype(jnp.bfloat16))
            return m_new, alpha * lc + jnp.sum(e, axis=0, keepdims=True)

        init = (jnp.full((1, 2 * tq), NEG_BIG, jnp.float32), jnp.zeros((1, 2 * tq), jnp.float32))
        finish(lax.fori_loop(0, n_k, body, init)[1])

    pl.when(overflowed)(exact_path)


def _diff_attention(q_t, k_t, k_nsq, k, v_t, lam_q1, lam_k1, lam_q2, lam_k2, subln_g, lam_init):
    B, _, S = q_t.shape
    assert DA_TQ == DA_TK, "the diagonal-tile bias assumes square tiles"
    hw = 2 * DA_HEAD_DIM
    n_k = S // DA_TK
    slopes2 = _alibi_slopes(DA_HEADS) * LOG2E
    vec = lambda a: a.reshape(1, DA_HEAD_DIM)
    small = pl.BlockSpec((1, DA_HEAD_DIM), lambda b, h, i, s1, s2: (0, 0))
    grid_spec = pltpu.PrefetchScalarGridSpec(
        num_scalar_prefetch=2,
        grid=(B, DA_HEADS, S // DA_TQ),
        in_specs=[
            pl.BlockSpec((None, hw, DA_TQ), lambda b, h, i, s1, s2: (b, h, i)),
            pl.BlockSpec((None, hw, DA_TQ), lambda b, h, i, s1, s2: (b, h, i)),
            pl.BlockSpec((None, None, 2, ROW_TILE), lambda b, h, i, s1, s2: (b, h, 0, 0)),
            pl.BlockSpec((None, S, 2 * DA_KCOLS), lambda b, h, i, s1, s2: (b, 0, h)),
            pl.BlockSpec((None, n_k, hw, DA_TK), lambda b, h, i, s1, s2: (b, 0, h, 0)),
            small, small, small, small,
            pl.BlockSpec((hw, 1), lambda b, h, i, s1, s2: (0, 0)),
        ],
        out_specs=pl.BlockSpec((None, DA_TQ, hw), lambda b, h, i, s1, s2: (b, i, h)),
        scratch_shapes=[
            pltpu.VMEM((DA_TK, 2 * DA_TQ), jnp.float32),
            pltpu.VMEM((3, 2 * DA_KCOLS, 2 * DA_TQ), jnp.bfloat16),
            pltpu.VMEM((DA_TK, 2 * DA_TQ), jnp.bfloat16),
            pltpu.VMEM((DA_TK, 2 * DA_TQ), jnp.bfloat16),
            pltpu.VMEM((hw, 2 * DA_TQ), jnp.float32),
        ],
    )
    return pl.pallas_call(
        functools.partial(_diff_attn_kernel, n_k=n_k, lam_init=lam_init),
        grid_spec=grid_spec,
        out_shape=jax.ShapeDtypeStruct((B, S, DA_WIDTH), jnp.bfloat16),
        compiler_params=pltpu.CompilerParams(
            dimension_semantics=("arbitrary", "arbitrary", "arbitrary"),
            vmem_limit_bytes=VMEM_LIMIT_BYTES),
        name="diff_attn",
    )(jnp.asarray(slopes2, jnp.float32), jnp.asarray(1.0 / slopes2, jnp.float32),
      q_t, k_t, k_nsq.reshape(B, DA_HEADS, 2, ROW_TILE), k, v_t,
      vec(lam_q1), vec(lam_k1), vec(lam_q2), vec(lam_k2), subln_g.reshape(hw, 1))


def _win_attn_kernel(slope_ref, sink_ref, q_ref, k_ref, v_ref, o_ref, ot_ref, *, seq_len):
    tq, band = WG_TQ, WG_BAND
    n_vt = band // LANES
    rep = WG_HEADS // WG_KV_HEADS
    qi = pl.program_id(1)
    q_start = qi * tq
    tile0 = jnp.clip(q_start // LANES - WINDOW // LANES, 0, seq_len // LANES - n_vt)
    k_start = pl.multiple_of(tile0 * LANES, LANES)

    kband = k_ref[pl.ds(k_start, band), :]
    kpos = k_start + lax.broadcasted_iota(jnp.int32, (band, tq), 0)
    qpos = q_start + lax.broadcasted_iota(jnp.int32, (band, tq), 1)
    dist_i = jnp.abs(qpos - kpos)
    dist = jnp.where(dist_i <= WINDOW, dist_i.astype(jnp.float32), -NEG_BIG)

    def group(g, shift_by_max):
        rows = slice(g * WG_HEAD_DIM, (g + 1) * WG_HEAD_DIM)
        blocks = []
        for r in range(rep):
            hd = g * rep + r
            parts = []
            if g > 0:
                parts.append(jnp.zeros((g * WG_HEAD_DIM, tq), q_ref.dtype))
            parts.append(q_ref[hd * WG_HEAD_DIM:(hd + 1) * WG_HEAD_DIM, :])
            if g < WG_KV_HEADS - 1:
                parts.append(jnp.zeros(((WG_KV_HEADS - 1 - g) * WG_HEAD_DIM, tq), q_ref.dtype))
            blocks.append(jnp.concatenate(parts, axis=0))
        s = _dot(kband, jnp.concatenate(blocks, axis=1))
        es, denoms = [], []
        for r in range(rep):
            hd = g * rep + r
            sr = s[:, r * tq:(r + 1) * tq] - slope_ref[hd] * dist
            sk = jnp.full((1, tq), sink_ref[hd] * LOG2E, jnp.float32)
            if shift_by_max:
                m = jnp.maximum(jnp.max(sr, axis=0, keepdims=True), sk)
                sr, sk = sr - m, sk - m
            denoms.append(jnp.exp2(sk))
            es.append(jnp.exp2(sr).astype(jnp.bfloat16))
        vg = jnp.concatenate([v_ref[tile0 + t, rows, :] for t in range(n_vt)], axis=1)
        vg = jnp.concatenate([vg, jnp.ones((BF16_ROWS, band), vg.dtype)], axis=0)
        av = _dot(vg, jnp.concatenate(es, axis=1))
        den = av[WG_HEAD_DIM:WG_HEAD_DIM + 1] + jnp.concatenate(denoms, axis=1)
        og = av[:WG_HEAD_DIM] / den
        for r in range(rep):
            hd = g * rep + r
            ot_ref[hd * WG_HEAD_DIM:(hd + 1) * WG_HEAD_DIM, :] = og[:, r * tq:(r + 1) * tq]
        return jnp.sum(og * 0.0) + jnp.sum(jnp.where(den >= WG_MIN_DENOM, 0.0, 1.0))

    bad = group(0, False)
    for g in range(1, WG_KV_HEADS):
        bad = bad + group(g, False)
    o_ref[...] = ot_ref[...].T.astype(o_ref.dtype)

    @pl.when(jnp.logical_not(bad == 0.0))
    def _():
        for g in range(WG_KV_HEADS):
            group(g, True)
        o_ref[...] = ot_ref[...].T.astype(o_ref.dtype)


def _window_attention(q_t, k, v_t, sink_logit):
    B, _, S = q_t.shape
    slopes2 = jnp.asarray(_alibi_slopes(WG_HEADS) * LOG2E, jnp.float32)
    grid_spec = pltpu.PrefetchScalarGridSpec(
        num_scalar_prefetch=2,
        grid=(B, S // WG_TQ),
        in_specs=[
            pl.BlockSpec((None, WG_Q, WG_TQ), lambda b, i, s1, s2: (b, 0, i)),
            pl.BlockSpec((None, S, WG_KV), lambda b, i, s1, s2: (b, 0, 0)),
            pl.BlockSpec((None, S // LANES, WG_KV, LANES), lambda b, i, s1, s2: (b, 0, 0, 0)),
        ],
        out_specs=pl.BlockSpec((None, WG_TQ, WG_Q), lambda b, i, s1, s2: (b, i, 0)),
        scratch_shapes=[pltpu.VMEM((WG_Q, WG_TQ), jnp.float32)],
    )
    return pl.pallas_call(
        functools.partial(_win_attn_kernel, seq_len=S),
        grid_spec=grid_spec,
        out_shape=jax.ShapeDtypeStruct((B, S, WG_Q), jnp.bfloat16),
        compiler_params=pltpu.CompilerParams(
            dimension_semantics=("arbitrary", "arbitrary"), vmem_limit_bytes=VMEM_LIMIT_BYTES),
        name="win_attn",
    )(slopes2, sink_logit.astype(jnp.float32), q_t, k, v_t)


def _layer_norm(x, g, b):
    mu = jnp.mean(x, axis=-1, keepdims=True)
    xc = x - mu
    var = jnp.mean(xc * xc, axis=-1, keepdims=True)
    return xc * lax.rsqrt(var + LN_EPS) * g + b


def _merge_kernel(x_ref, oa_ref, ob_ref, wg_ref, bg_ref, wa_ref, wb_ref, wo_ref, g_ref, b_ref, y_ref):
    x = x_ref[...]
    gates = jax.nn.sigmoid(_dot(x.astype(jnp.bfloat16), wg_ref[...]) + bg_ref[...])
    merged = (gates[:, :D_MODEL] * _dot(oa_ref[...], wa_ref[...])
              + gates[:, D_MODEL:] * _dot(ob_ref[...], wb_ref[...]))
    mix = _dot(merged.astype(jnp.bfloat16), wo_ref[...])
    y_ref[...] = _layer_norm(DEEPNORM_ALPHA * x + mix, g_ref[...], b_ref[...])


def _merge(x, o_a, o_b, w_gate, b_gate, w_br_a, w_br_b, w_out, ln_g, ln_b):
    B, S, D = x.shape
    tok = lambda b, s: (b, s, 0)
    const = lambda b, s: (0, 0)
    return pl.pallas_call(
        _merge_kernel,
        grid=(B, S // ROW_TILE),
        in_specs=[
            pl.BlockSpec((None, ROW_TILE, D), tok),
            pl.BlockSpec((None, ROW_TILE, DA_WIDTH), tok),
            pl.BlockSpec((None, ROW_TILE, WG_Q), tok),
            pl.BlockSpec((D, 2 * D), const),
            pl.BlockSpec((1, 2 * D), const),
            pl.BlockSpec((DA_WIDTH, D), const),
            pl.BlockSpec((WG_Q, D), const),
            pl.BlockSpec((D, D), const),
            pl.BlockSpec((1, D), const),
            pl.BlockSpec((1, D), const),
        ],
        out_specs=pl.BlockSpec((None, ROW_TILE, D), tok),
        out_shape=jax.ShapeDtypeStruct((B, S, D), jnp.float32),
        compiler_params=pltpu.CompilerParams(
            dimension_semantics=("arbitrary", "arbitrary"), vmem_limit_bytes=VMEM_LIMIT_BYTES),
        name="merge_ln1",
    )(x, o_a, o_b, w_gate, b_gate, w_br_a, w_br_b, w_out, ln_g, ln_b)


def _ffn_kernel(x_ref, w1_ref, b1_ref, w2_ref, b2_ref, g_ref, b_ref, y_ref):
    x = x_ref[...]
    h = jnp.maximum(_dot(x.astype(jnp.bfloat16), w1_ref[...]) + b1_ref[...], 0.0)
    f = _dot((h * h).astype(jnp.bfloat16), w2_ref[...]) + b2_ref[...]
    y_ref[...] = _layer_norm(DEEPNORM_ALPHA * x + f, g_ref[...], b_ref[...])


def _ffn(x, w1, b1, w2, b2, ln_g, ln_b):
    B, S, D = x.shape
    tok = lambda b, s: (b, s, 0)
    const = lambda b, s: (0, 0)
    return pl.pallas_call(
        _ffn_kernel,
        grid=(B, S // ROW_TILE),
        in_specs=[
            pl.BlockSpec((None, ROW_TILE, D), tok),
            pl.BlockSpec((D, D_FF), const),
            pl.BlockSpec((1, D_FF), const),
            pl.BlockSpec((D_FF, D), const),
            pl.BlockSpec((1, D), const),
            pl.BlockSpec((1, D), const),
            pl.BlockSpec((1, D), const),
        ],
        out_specs=pl.BlockSpec((None, ROW_TILE, D), tok),
        out_shape=jax.ShapeDtypeStruct((B, S, D), jnp.float32),
        compiler_params=pltpu.CompilerParams(
            dimension_semantics=("arbitrary", "arbitrary"), vmem_limit_bytes=VMEM_LIMIT_BYTES),
        name="ffn_ln2",
    )(x, w1, b1, w2, b2, ln_g, ln_b)


def _encoder_layer(x, l, w_in, b_gate, lam_q1, lam_k1, lam_q2, lam_k2, subln_g, sink_logit,
                   w_br_a, w_br_b, w_out, ln1_g, ln1_b, w_ff1, b_ff1, w_ff2, b_ff2, ln2_g, ln2_b):
    bf = jnp.bfloat16
    row = lambda a: a.reshape(1, -1)
    w_t = jnp.concatenate([w_in[:, OFF_DA_Q:OFF_DA_K], w_in[:, OFF_DA_V:OFF_WG_Q],
                           w_in[:, OFF_WG_Q:OFF_WG_K], w_in[:, OFF_WG_V:OFF_GATE],
                           w_in[:, OFF_DA_K:OFF_DA_V]], axis=1).T.astype(bf)
    w_ka = w_in[:, OFF_DA_K:OFF_DA_V].astype(bf)
    w_kb = w_in[:, OFF_WG_K:OFF_WG_V].astype(bf)
    w_gate = w_in[:, OFF_GATE:].astype(bf)
    lam_init = 0.8 - 0.6 * math.exp(-0.3 * l)

    qa_t, va_t, qb_t, vb_t, ka_t, k_a, k_b, ka_nsq = _project(x, w_t, w_ka, w_kb)
    o_a = _diff_attention(qa_t, ka_t, ka_nsq, k_a, va_t, lam_q1, lam_k1, lam_q2, lam_k2, subln_g, lam_init)
    o_b = _window_attention(qb_t, k_b, vb_t, sink_logit)
    x1 = _merge(x, o_a, o_b, w_gate, row(b_gate), w_br_a.astype(bf), w_br_b.astype(bf),
                w_out.astype(bf), row(ln1_g), row(ln1_b))
    return _ffn(x1, w_ff1.astype(bf), row(b_ff1), w_ff2.astype(bf), row(b_ff2), row(ln2_g), row(ln2_b))


def kernel(x_prompt, x_sample, w_in, b_gate, lam_q1, lam_k1, lam_q2, lam_k2, subln_g, sink_logit,
           w_br_a, w_br_b, w_out, ln1_g, ln1_b, w_ff1, b_ff1, w_ff2, b_ff2, ln2_g, ln2_b):
    def run(x):
        for l in range(DEPTH):
            x = _encoder_layer(x, l, w_in[l], b_gate[l], lam_q1[l], lam_k1[l], lam_q2[l], lam_k2[l],
                               subln_g[l], sink_logit[l], w_br_a[l], w_br_b[l], w_out[l],
                               ln1_g[l], ln1_b[l], w_ff1[l], b_ff1[l], w_ff2[l], b_ff2[l],
                               ln2_g[l], ln2_b[l])
        return x

    return (run(x_prompt), run(x_sample))
```
